```python
import math
import jax, jax.numpy as jnp
from jax import lax
import numpy as np

D_MODEL = 1024
BATCH = 16
SEQ = 2048
DEPTH = 2

HEAD_DIM = 64
H_A = D_MODEL // HEAD_DIM
KV_A = 4
G_A = H_A // KV_A
WINDOW_A = 128
BLOCK_A = 128
H_B = D_MODEL // HEAD_DIM
DILATED_GROUPS = ((128, 1), (512, 4), (2048, 16))
N_GROUPS_B = len(DILATED_GROUPS)
BLOCK_B = 64
D_FF = -(-8 * D_MODEL // (3 * 256)) * 256
RMS_EPS = 1e-6
NEG = -1e30

kernel_name = "hybrid_window_gqa_dilated_attn_encoder"


def rmsnorm(x, g):
    x32 = x.astype(jnp.float32)
    y = x32 * lax.rsqrt(jnp.mean(x32 * x32, axis=-1, keepdims=True) + RMS_EPS)
    return (y * g.astype(jnp.float32)).astype(x.dtype)


def alibi_slopes(n):
    return 2.0 ** (-8.0 * jnp.arange(1, n + 1, dtype=jnp.float32) / n)


def banded_attention(q, k, v, slopes, dist_unit, window, block):
    n, L, hk, g, dh = q.shape
    c = window // block
    nb = -(-L // block)
    lp = nb * block
    kw = (2 * c + 1) * block
    qb = jnp.pad(q, ((0, 0), (0, lp - L), (0, 0), (0, 0), (0, 0))).reshape(n, nb, block, hk, g, dh)
    pad_k = ((0, 0), (window, window + lp - L), (0, 0), (0, 0))
    kb = jnp.pad(k, pad_k).reshape(n, nb + 2 * c, block, hk, dh)
    vb = jnp.pad(v, pad_k).reshape(n, nb + 2 * c, block, hk, dh)
    kwin = jnp.concatenate([kb[:, j:j + nb] for j in range(2 * c + 1)], axis=2)
    vwin = jnp.concatenate([vb[:, j:j + nb] for j in range(2 * c + 1)], axis=2)
    scores = jnp.einsum('nbqhgd,nbkhd->nbhgqk', qb, kwin,
                        preferred_element_type=jnp.float32) * (dh ** -0.5)
    rel = jnp.arange(kw)[None, :] - window - jnp.arange(block)[:, None]
    key_pos = jnp.arange(nb)[:, None] * block - window + jnp.arange(kw)[None, :]
    valid = (jnp.abs(rel)[None] <= window) & ((key_pos >= 0) & (key_pos < L))[:, None, :]
    dist = jnp.abs(rel).astype(jnp.float32) * dist_unit
    scores = scores - slopes.astype(jnp.float32)[:, :, None, None] * dist
    scores = jnp.where(valid[None, :, None, None], scores, NEG)
    m = jnp.max(scores, axis=-1)
    p = jnp.exp(scores - m[..., None])
    den = jnp.sum(p, axis=-1)
    lse = m + jnp.log(den)
    out = jnp.einsum('nbhgqk,nbkhd->nbqhgd', p.astype(v.dtype), vwin,
                     preferred_element_type=jnp.float32)
    out = out / jnp.moveaxis(den, -1, 2)[..., None]
    out = out.reshape(n, lp, hk, g, dh)[:, :L]
    lse = jnp.moveaxis(lse, -1, 2).reshape(n, lp, hk, g)[:, :L]
    return out.astype(q.dtype), lse


def window_gqa_sink(h, w_qkv, w_out, sink):
    b, s, _ = h.shape
    qkv = h @ w_qkv
    q = qkv[..., :H_A * HEAD_DIM].reshape(b, s, KV_A, G_A, HEAD_DIM)
    k = qkv[..., H_A * HEAD_DIM:(H_A + KV_A) * HEAD_DIM].reshape(b, s, KV_A, HEAD_DIM)
    v = qkv[..., (H_A + KV_A) * HEAD_DIM:].reshape(b, s, KV_A, HEAD_DIM)
    slopes = alibi_slopes(H_A).reshape(KV_A, G_A)
    o, lse = banded_attention(q, k, v, slopes, 1, WINDOW_A, BLOCK_A)
    o = o * jax.nn.sigmoid(lse - sink.astype(jnp.float32).reshape(KV_A, G_A))[..., None]
    return o.astype(h.dtype).reshape(b, s, H_A * HEAD_DIM) @ w_out


def to_residue(t, dil):
    b, s, hh, dh = t.shape
    return t.reshape(b, s // dil, dil, hh, dh).transpose(0, 2, 1, 3, 4).reshape(b * dil, s // dil, hh, dh)


def from_residue(t, b, dil):
    rest = t.shape[2:]
    L = t.shape[1]
    t = t.reshape((b, dil, L) + rest)
    t = jnp.swapaxes(t, 1, 2)
    return t.reshape((b, L * dil) + rest)


def dilated_mixture_attention(h, w_qkv, w_out):
    b, s, _ = h.shape
    qkv = (h @ w_qkv).reshape(b, s, N_GROUPS_B, 3, H_B, HEAD_DIM)
    slopes = alibi_slopes(H_B)[:, None]
    outs, lses = [], []
    for gi, (win, dil) in enumerate(DILATED_GROUPS):
        q = to_residue(qkv[:, :, gi, 0], dil)[:, :, :, None]
        k = to_residue(qkv[:, :, gi, 1], dil)
        v = to_residue(qkv[:, :, gi, 2], dil)
        o, lse = banded_attention(q, k, v, slopes, dil, win // (2 * dil), BLOCK_B)
        outs.append(from_residue(o[:, :, :, 0], b, dil))
        lses.append(from_residue(lse[:, :, :, 0], b, dil))
    wts = jax.nn.softmax(jnp.stack(lses), axis=0)
    o = jnp.einsum('gbsh,gbshd->bshd', wts, jnp.stack(outs).astype(jnp.float32))
    return o.astype(h.dtype).reshape(b, s, H_B * HEAD_DIM) @ w_out


def swiglu(h, w_gate, w_up, w_down):
    return (jax.nn.silu(h @ w_gate) * (h @ w_up)) @ w_down


def setup_inputs(seed: int = 0) -> dict:
    key = jax.random.key(seed)
    ks = jax.random.split(key, 14)
    n_a = (DEPTH + 1) // 2
    n_b = DEPTH // 2
    qkv_a = (H_A + 2 * KV_A) * HEAD_DIM
    qkv_b = N_GROUPS_B * 3 * H_B * HEAD_DIM
    f32 = jnp.float32

    def w(k, shape, fan_in):
        return jax.random.normal(k, shape, f32) * fan_in ** -0.5

    return {
        "x": jax.random.normal(ks[0], (BATCH, SEQ, D_MODEL), f32),
        "norm_mix": 1.0 + 0.02 * jax.random.normal(ks[1], (DEPTH, D_MODEL), f32),
        "norm_ffn": 1.0 + 0.02 * jax.random.normal(ks[2], (DEPTH, D_MODEL), f32),
        "w_qkv_a": w(ks[3], (n_a, D_MODEL, qkv_a), D_MODEL),
        "w_out_a": w(ks[4], (n_a, H_A * HEAD_DIM, D_MODEL), H_A * HEAD_DIM),
        "sink_a": 0.5 * jax.random.normal(ks[5], (n_a, H_A), f32),
        "w_qkv_b": w(ks[6], (n_b, D_MODEL, qkv_b), D_MODEL),
        "w_out_b": w(ks[7], (n_b, H_B * HEAD_DIM, D_MODEL), H_B * HEAD_DIM),
        "w_gate": w(ks[8], (DEPTH, D_MODEL, D_FF), D_MODEL),
        "w_up": w(ks[9], (DEPTH, D_MODEL, D_FF), D_MODEL),
        "w_down": w(ks[10], (DEPTH, D_FF, D_MODEL), D_FF),
        "norm_final": 1.0 + 0.02 * jax.random.normal(ks[11], (D_MODEL,), f32),
    }


def reference(x, norm_mix, norm_ffn, w_qkv_a, w_out_a, sink_a, w_qkv_b, w_out_b,
              w_gate, w_up, w_down, norm_final):
    for i in range(DEPTH):
        h = rmsnorm(x, norm_mix[i])
        j = i // 2
        if i % 2 == 0:
            x = x + window_gqa_sink(h, w_qkv_a[j], w_out_a[j], sink_a[j])
        else:
            x = x + dilated_mixture_attention(h, w_qkv_b[j], w_out_b[j])
        h = rmsnorm(x, norm_ffn[i])
        x = x + swiglu(h, w_gate[i], w_up[i], w_down[i])
    return rmsnorm(x, norm_final)
```

```python
import functools
import math

import jax
import jax.numpy as jnp
from jax import lax
from jax.experimental import pallas as pl
from jax.experimental.pallas import tpu as pltpu

D_MODEL = 1024
HEAD_DIM = 64
N_HEADS = 16
KV_A = 4
WINDOW_A = 128
DILATED_GROUPS = ((128, 1), (512, 4), (2048, 16))
RMS_EPS = 1e-6
NEG = -1e30
LANES = 128
VMEM_LIMIT = 56 * 1024 * 1024

F32 = jnp.float32
BF16 = jnp.bfloat16


def _rms(x, g):
    ms = jnp.mean(x * x, axis=-1, keepdims=True)
    return x * lax.rsqrt(ms + RMS_EPS) * g


def _alibi_slopes(n):
    return [2.0 ** (-8.0 * (i + 1) / n) for i in range(n)]


def _const_spec(shape):
    return pl.BlockSpec(shape, lambda *_: (0,) * len(shape), pipeline_mode=pl.Buffered(1))


def _norm_matmul_kernel(x_ref, g_ref, w_ref, o_ref):
    h = _rms(x_ref[...], g_ref[...]).astype(BF16)
    o_ref[...] = jnp.dot(h, w_ref[...], preferred_element_type=F32).astype(o_ref.dtype)


def _norm_matmul(x2d, g, w, tm, tn):
    t, d = x2d.shape
    n = w.shape[1]
    return pl.pallas_call(
        _norm_matmul_kernel,
        grid=(t // tm, n // tn),
        in_specs=[
            pl.BlockSpec((tm, d), lambda i, j: (i, 0)),
            pl.BlockSpec((1, d), lambda i, j: (0, 0)),
            pl.BlockSpec((d, tn), lambda i, j: (0, j)),
        ],
        out_specs=pl.BlockSpec((tm, tn), lambda i, j: (i, j)),
        out_shape=jax.ShapeDtypeStruct((t, n), BF16),
        compiler_params=pltpu.CompilerParams(
            dimension_semantics=("parallel", "arbitrary"), vmem_limit_bytes=VMEM_LIMIT),
    )(x2d, g.reshape(1, d), w)


def _attn_kernel(*refs, tq, tk, window, seq_len, unit, n_kv, ns, tqb, with_sink, with_lse):
    refs = list(refs)
    sink_ref = refs.pop(0) if with_sink else None
    q_ref, k_ref, v_ref, o_ref = refs[:4]
    lse_ref = refs[4] if with_lse else None
    slopes = _alibi_slopes(N_HEADS)
    group = N_HEADS // n_kv
    qb = pl.program_id(1)

    def tile_body(t, carry):
        r0 = pl.multiple_of(t * tq, tq)
        q0 = qb * tqb + r0
        start = pl.multiple_of(jnp.clip(q0 - window, 0, seq_len - tk), tq)
        off = q0 - start
        row = lax.broadcasted_iota(jnp.int32, (tq, tk), 0)
        col = lax.broadcasted_iota(jnp.int32, (tq, tk), 1)
        dist = jnp.abs(col - row - off)
        negd = jnp.where(dist <= window, -(dist.astype(F32) * float(unit)), NEG)
        lane = lax.broadcasted_iota(jnp.int32, (tq, LANES), 1)
        for s in range(ns):
            q = q_ref[s, pl.ds(r0, tq), :]
            k = k_ref[s, pl.ds(start, tk), :]
            v = v_ref[s, pl.ds(start, tk), :]
            outs = []
            lse_tile = jnp.zeros((tq, LANES), F32)
            for h in range(N_HEADS):
                c = h // group
                qh = q[:, h * HEAD_DIM:(h + 1) * HEAD_DIM]
                kh = k[:, c * HEAD_DIM:(c + 1) * HEAD_DIM]
                vh = v[:, c * HEAD_DIM:(c + 1) * HEAD_DIM]
                sc = lax.dot_general(qh, kh, (((1,), (1,)), ((), ())),
                                     preferred_element_type=F32)
                sc = sc + slopes[h] * negd
                m = jnp.max(sc, axis=-1, keepdims=True)
                p = jnp.exp(sc - m)
                den = jnp.sum(p, axis=-1, keepdims=True)
                oh = jnp.dot(p.astype(BF16), vh, preferred_element_type=F32) / den
                lse = m + jnp.log(den)
                if with_sink:
                    oh = oh * (1.0 / (1.0 + jnp.exp(sink_ref[h] - lse)))
                if with_lse:
                    lse_tile = jnp.where(lane == h, lse, lse_tile)
                outs.append(oh)
            o_ref[s, pl.ds(r0, tq), :] = jnp.concatenate(outs, axis=-1).astype(o_ref.dtype)
            if with_lse:
                lse_ref[s, pl.ds(r0, tq), :] = lse_tile
        return carry

    lax.fori_loop(0, tqb // tq, tile_body, 0)


def _attention(q, k, v, *, q_col, k_col, v_col, n_kv, window, unit, tq, ns, tqb,
               out_dtype, sink=None, with_lse=False):
    n, seq_len = q.shape[0], q.shape[1]
    tk = min(tq + 2 * window, seq_len)
    dq = N_HEADS * HEAD_DIM
    dkv = n_kv * HEAD_DIM
    kern = functools.partial(
        _attn_kernel, tq=tq, tk=tk, window=window, seq_len=seq_len, unit=unit, n_kv=n_kv,
        ns=ns, tqb=tqb, with_sink=sink is not None, with_lse=with_lse)
    in_specs = [
        pl.BlockSpec((ns, tqb, dq), lambda i, j: (i, j, q_col)),
        pl.BlockSpec((ns, seq_len, dkv), lambda i, j: (i, 0, k_col)),
        pl.BlockSpec((ns, seq_len, dkv), lambda i, j: (i, 0, v_col)),
    ]
    args = [q, k, v]
    if sink is not None:
        in_specs.insert(0, pl.BlockSpec(memory_space=pltpu.SMEM))
        args.insert(0, sink)
    out_specs = [pl.BlockSpec((ns, tqb, dq), lambda i, j: (i, j, 0))]
    out_shape = [jax.ShapeDtypeStruct((n, seq_len, dq), out_dtype)]
    if with_lse:
        out_specs.append(pl.BlockSpec((ns, tqb, LANES), lambda i, j: (i, j, 0)))
        out_shape.append(jax.ShapeDtypeStruct((n, seq_len, LANES), F32))
    return pl.pallas_call(
        kern,
        grid=(n // ns, seq_len // tqb),
        in_specs=in_specs,
        out_specs=out_specs,
        out_shape=out_shape,
        compiler_params=pltpu.CompilerParams(
            dimension_semantics=("parallel", "arbitrary"), vmem_limit_bytes=VMEM_LIMIT),
    )(*args)


def _post_kernel(*refs, n_groups, final, ff_chunks):
    refs = list(refs)
    x_ref = refs.pop(0)
    o_refs = [refs.pop(0) for _ in range(n_groups)]
    lse_refs = [refs.pop(0) for _ in range(n_groups)] if n_groups > 1 else []
    wo_ref, gf_ref, wg_ref, wu_ref, wd_ref = refs[:5]
    refs = refs[5:]
    gfin_ref = refs.pop(0) if final else None
    out_ref = refs.pop(0)

    if n_groups > 1:
        lses = [r[...] for r in lse_refs]
        mx = functools.reduce(jnp.maximum, lses)
        es = [jnp.exp(l - mx) for l in lses]
        den = functools.reduce(lambda a, b: a + b, es)
        ws = [e / den for e in es]
        tm = x_ref.shape[0]
        lane = lax.broadcasted_iota(jnp.int32, (tm, LANES), 1)
        tiles = []
        for p in range(D_MODEL // LANES):
            acc = jnp.zeros((tm, LANES), F32)
            for g in range(n_groups):
                wexp = jnp.where(lane < HEAD_DIM, ws[g][:, 2 * p:2 * p + 1],
                                 ws[g][:, 2 * p + 1:2 * p + 2])
                acc = acc + wexp * o_refs[g][:, p * LANES:(p + 1) * LANES]
            tiles.append(acc)
        o = jnp.concatenate(tiles, axis=-1).astype(BF16)
    else:
        o = o_refs[0][...]

    x1 = x_ref[...] + jnp.dot(o, wo_ref[...], preferred_element_type=F32)
    h = _rms(x1, gf_ref[...]).astype(BF16)
    d_ff = wg_ref.shape[1]
    fc = d_ff // ff_chunks
    acc = x1
    for c in range(ff_chunks):
        gate = jnp.dot(h, wg_ref[:, c * fc:(c + 1) * fc], preferred_element_type=F32)
        up = jnp.dot(h, wu_ref[:, c * fc:(c + 1) * fc], preferred_element_type=F32)
        a = (gate / (1.0 + jnp.exp(-gate)) * up).astype(BF16)
        acc = acc + jnp.dot(a, wd_ref[c * fc:(c + 1) * fc, :], preferred_element_type=F32)
    if final:
        acc = _rms(acc, gfin_ref[...])
    out_ref[...] = acc


def _post(x2d, os, lses, wo, gf, wg, wu, wd, gfin, tm, ff_chunks):
    t, d = x2d.shape
    d_ff = wg.shape[1]
    n_groups = len(os)
    final = gfin is not None
    row = lambda shape: pl.BlockSpec(shape, lambda i: (i, 0))
    in_specs = [row((tm, d))] + [row((tm, d)) for _ in os] + [row((tm, LANES)) for _ in lses]
    in_specs += [_const_spec((d, d)), _const_spec((1, d)), _const_spec((d, d_ff)),
                 _const_spec((d, d_ff)), _const_spec((d_ff, d))]
    args = [x2d, *os, *lses, wo, gf.reshape(1, d), wg, wu, wd]
    if final:
        in_specs.append(_const_spec((1, d)))
        args.append(gfin.reshape(1, d))
    return pl.pallas_call(
        functools.partial(_post_kernel, n_groups=n_groups, final=final, ff_chunks=ff_chunks),
        grid=(t // tm,),
        in_specs=in_specs,
        out_specs=row((tm, d)),
        out_shape=jax.ShapeDtypeStruct((t, d), F32),
        compiler_params=pltpu.CompilerParams(
            dimension_semantics=("parallel",), vmem_limit_bytes=VMEM_LIMIT),
    )(*args)


def _to_residue(t, dil):
    b, s, c = t.shape
    return t.reshape(b, s // dil, dil, c).transpose(0, 2, 1, 3).reshape(b * dil, s // dil, c)


def _from_residue(t, b, dil):
    n, l, c = t.shape
    return t.reshape(b, dil, l, c).transpose(0, 2, 1, 3).reshape(b, l * dil, c)


def kernel(x, norm_mix, norm_ffn, w_qkv_a, w_out_a, sink_a, w_qkv_b, w_out_b,
           w_gate, w_up, w_down, norm_final):
    b, s, d = x.shape
    t = b * s
    dq = N_HEADS * HEAD_DIM
    scale = HEAD_DIM ** -0.5

    wa = w_qkv_a[0]
    wa = jnp.concatenate([wa[:, :dq] * scale, wa[:, dq:]], axis=1).astype(BF16)
    x2d = x.reshape(t, d)
    qkv = _norm_matmul(x2d, norm_mix[0], wa, tm=1024, tn=wa.shape[1]).reshape(b, s, -1)
    kv_cols = KV_A * HEAD_DIM
    (o_a,) = _attention(qkv, qkv, qkv, q_col=0, k_col=dq // kv_cols, v_col=dq // kv_cols + 1,
                        n_kv=KV_A, window=WINDOW_A, unit=1, tq=128, ns=1, tqb=512,
                        out_dtype=BF16, sink=sink_a[0])
    x2d = _post(x2d, [o_a.reshape(t, d)], [], w_out_a[0].astype(BF16), norm_ffn[0],
                w_gate[0].astype(BF16), w_up[0].astype(BF16), w_down[0].astype(BF16),
                None, tm=512, ff_chunks=2)

    n_g = len(DILATED_GROUPS)
    wb = w_qkv_b[0].reshape(d, n_g, 3, dq)
    wb = jnp.concatenate([wb[:, :, :1] * scale, wb[:, :, 1:]], axis=2).reshape(d, -1).astype(BF16)
    qkv = _norm_matmul(x2d, norm_mix[1], wb, tm=1024, tn=3 * dq).reshape(b, s, -1)
    os, lses = [], []
    for gi, (win, dil) in enumerate(DILATED_GROUPS):
        g = _to_residue(qkv[:, :, gi * 3 * dq:(gi + 1) * 3 * dq], dil)
        seq = s // dil
        window = win // (2 * dil)
        if seq >= 64 + 2 * window:
            tq, ns, tqb = 64, 1, min(seq, 512)
        else:
            tq, ns, tqb = seq, 512 // seq, seq
        o_g, lse_g = _attention(g, g, g, q_col=0, k_col=1, v_col=2, n_kv=N_HEADS,
                                window=window, unit=dil, tq=tq, ns=ns, tqb=tqb,
                                out_dtype=F32, with_lse=True)
        os.append(_from_residue(o_g, b, dil).reshape(t, d))
        lses.append(_from_residue(lse_g, b, dil).reshape(t, LANES))
    out = _post(x2d, os, lses, w_out_b[0].astype(BF16), norm_ffn[1],
                w_gate[1].astype(BF16), w_up[1].astype(BF16), w_down[1].astype(BF16),
                norm_final, tm=512, ff_chunks=2)
    return out.reshape(b, s, d)
```

```python
import functools

import jax
import jax.numpy as jnp
from jax import lax
from jax.experimental import pallas as pl
from jax.experimental.pallas import tpu as pltpu

D_MODEL = 1024
HEAD_DIM = 64
N_HEADS = 16
KV_A = 4
WINDOW_A = 128
DILATED_GROUPS = ((128, 1), (512, 4), (2048, 16))
RMS_EPS = 1e-6
NEG = -1e30
LANES = 128
MXU_DIM = 256
HEADS_PER_TILE = MXU_DIM // HEAD_DIM
N_TILES = N_HEADS // HEADS_PER_TILE
TQ = 128
SPAN = 512
LSE_PARTS = 3
VMEM_LIMIT = 56 * 1024 * 1024

F32 = jnp.float32
BF16 = jnp.bfloat16


def _rms(x, g):
    ms = jnp.mean(x * x, axis=-1, keepdims=True)
    return x * lax.rsqrt(ms + RMS_EPS) * g


def _alibi_slopes(n):
    return [2.0 ** (-8.0 * (i + 1) / n) for i in range(n)]


def _const_spec(shape):
    return pl.BlockSpec(shape, lambda *_: (0,) * len(shape), pipeline_mode=pl.Buffered(1))


def _params(*sem):
    return pltpu.CompilerParams(dimension_semantics=sem, vmem_limit_bytes=VMEM_LIMIT)


def _grid_transpose_perm(n, n_major, n_minor):
    row = lax.broadcasted_iota(jnp.int32, (n, n), 0)
    col = lax.broadcasted_iota(jnp.int32, (n, n), 1)
    src = (row % n_minor) * n_major + row // n_minor
    return jnp.where(col == src, 1.0, 0.0).astype(BF16)


def _permute_rows(perm, x):
    return jnp.dot(perm, x, preferred_element_type=F32).astype(BF16)


def _norm_matmul_kernel(x_ref, g_ref, w_ref, o_ref):
    h = _rms(x_ref[...], g_ref[...]).astype(BF16)
    o_ref[...] = jnp.dot(h, w_ref[...], preferred_element_type=F32).astype(o_ref.dtype)


def _norm_matmul(x2d, g, w, tm):
    t, d = x2d.shape
    n = w.shape[1]
    return pl.pallas_call(
        _norm_matmul_kernel,
        grid=(t // tm,),
        in_specs=[pl.BlockSpec((tm, d), lambda i: (i, 0)), _const_spec((1, d)), _const_spec((d, n))],
        out_specs=pl.BlockSpec((tm, n), lambda i: (i, 0)),
        out_shape=jax.ShapeDtypeStruct((t, n), BF16),
        compiler_params=_params("parallel"),
    )(x2d, g.reshape(1, d), w)


def _norm_matmul_dilated_kernel(x_ref, g_ref, w_ref, *o_refs, dils):
    h = _rms(x_ref[...], g_ref[...]).astype(BF16)
    for gi, (dil, o_ref) in enumerate(zip(dils, o_refs)):
        hg = h if dil == 1 else _permute_rows(_grid_transpose_perm(SPAN, dil, SPAN // dil), h)
        y = jnp.dot(hg, w_ref[gi], preferred_element_type=F32).astype(o_ref.dtype)
        o_ref[0] = y.reshape(o_ref.shape[1:])


def _norm_matmul_dilated(x, g, w, dils):
    b, s, d = x.shape
    n = w.shape[2]
    out_specs = [pl.BlockSpec((1, dil, SPAN // dil, n), lambda bi, i: (bi, 0, i, 0)) for dil in dils]
    out_shape = [jax.ShapeDtypeStruct((b, dil, s // dil, n), BF16) for dil in dils]
    return pl.pallas_call(
        functools.partial(_norm_matmul_dilated_kernel, dils=dils),
        grid=(b, s // SPAN),
        in_specs=[pl.BlockSpec((None, SPAN, d), lambda bi, i: (bi, i, 0)),
                  _const_spec((1, d)), _const_spec(w.shape)],
        out_specs=out_specs,
        out_shape=out_shape,
        compiler_params=_params("parallel", "parallel"),
    )(x, g.reshape(1, d), w)


def _lane_group(shape):
    return lax.broadcasted_iota(jnp.int32, shape, 1) // HEAD_DIM


def _fill_bias(bias_ref, *, tk, window, unit, offsets):
    slopes = _alibi_slopes(N_HEADS)
    row = lax.broadcasted_iota(jnp.int32, (TQ, tk), 0)
    col = lax.broadcasted_iota(jnp.int32, (TQ, tk), 1)
    for v, off in enumerate(offsets):
        dist = jnp.abs(col - row - off)
        negd = jnp.where(dist <= window, -(dist.astype(F32) * float(unit)), NEG)
        for c in range(N_TILES):
            for j in range(HEADS_PER_TILE):
                bias_ref[v, c, j * TQ:(j + 1) * TQ, :] = slopes[HEADS_PER_TILE * c + j] * negd


def _softmax_pv(q_stack, k4, v4, bias):
    sc = lax.dot_general(q_stack, k4, (((1,), (1,)), ((), ())), preferred_element_type=F32) + bias
    m = jnp.max(sc, axis=-1, keepdims=True)
    p = jnp.exp(sc - m)
    den = jnp.sum(p, axis=-1, keepdims=True)
    pv = jnp.dot(p.astype(BF16), v4, preferred_element_type=F32)
    return pv, 1.0 / den, m + jnp.log(den)


def _window_start(q0, window, tk, seq_len):
    start = jnp.clip(q0 - window, 0, seq_len - tk)
    return pl.multiple_of(start, min(window, TQ)), (q0 - start) // window


def _split_f32(x):
    parts = []
    for _ in range(LSE_PARTS):
        p = x.astype(BF16)
        parts.append(p)
        x = x - p.astype(F32)
    return parts


def _attn_b_tile(qkv_ref, idx, r0, q0, bias_ref, stage_ref, s0, *, tk, window, seq_len):
    dq = N_HEADS * HEAD_DIM
    start, var = _window_start(q0, window, tk, seq_len)
    lg = _lane_group((TQ, MXU_DIM))
    lane = lax.broadcasted_iota(jnp.int32, (TQ, LANES), 1)
    lse_tile = jnp.zeros((TQ, LANES), F32)
    for c in range(N_TILES):
        lo = c * MXU_DIM
        q4 = qkv_ref[idx + (pl.ds(r0, TQ), slice(lo, lo + MXU_DIM))]
        k4 = qkv_ref[idx + (pl.ds(start, tk), slice(dq + lo, dq + lo + MXU_DIM))]
        v4 = qkv_ref[idx + (pl.ds(start, tk), slice(2 * dq + lo, 2 * dq + lo + MXU_DIM))]
        q_stack = jnp.concatenate(
            [jnp.where(lg == j, q4, jnp.zeros_like(q4)) for j in range(HEADS_PER_TILE)], axis=0)
        pv, rden, lse = _softmax_pv(q_stack, k4, v4, bias_ref[var, c])
        out = jnp.zeros((TQ, MXU_DIM), F32)
        for j in range(HEADS_PER_TILE):
            rows = slice(j * TQ, (j + 1) * TQ)
            out = jnp.where(lg == j, pv[rows] * rden[rows], out)
            lse_tile = jnp.where(lane % N_HEADS == HEADS_PER_TILE * c + j, lse[rows], lse_tile)
        stage_ref[pl.ds(s0, TQ), lo:lo + MXU_DIM] = out.astype(BF16)
    pieces = _split_f32(lse_tile)
    packed = jnp.zeros((TQ, LANES), BF16)
    for k, piece in enumerate(pieces):
        packed = jnp.where(lane // N_HEADS == k, piece, packed)
    stage_ref[pl.ds(s0, TQ), dq:dq + LANES] = packed


def _attn_b_kernel(qkv_ref, o_ref, lse_ref, bias_ref, stage_ref, *, n_cls, seq_len, tk, window, unit,
                   offsets, dil):
    @pl.when((pl.program_id(0) == 0) & (pl.program_id(1) == 0))
    def _():
        _fill_bias(bias_ref, tk=tk, window=window, unit=unit, offsets=offsets)

    i = pl.program_id(1)
    dq = N_HEADS * HEAD_DIM
    kw = dict(tk=tk, window=window, seq_len=seq_len)

    def tile(t, carry):
        s0 = pl.multiple_of(t * TQ, TQ)
        if dil == 1:
            r0 = pl.multiple_of((i * n_cls + t) * TQ, TQ)
            _attn_b_tile(qkv_ref, (), r0, r0, bias_ref, stage_ref, s0, **kw)
        else:
            q0 = (i * TQ) % seq_len
            _attn_b_tile(qkv_ref, (t,), pl.multiple_of(q0, TQ), q0, bias_ref, stage_ref, s0, **kw)
        return carry

    lax.fori_loop(0, n_cls, tile, 0)

    if dil == 1:
        o_ref[...] = stage_ref[:, :dq]
        lse_ref[...] = stage_ref[:, dq:]
    else:
        per_cls = SPAN // dil
        perm = _grid_transpose_perm(SPAN, per_cls, dil)
        for k in range(n_cls * TQ // SPAN):
            rows = [stage_ref[r * TQ + k * per_cls:r * TQ + (k + 1) * per_cls, :] for r in range(dil)]
            nat = _permute_rows(perm, jnp.concatenate(rows, axis=0))
            o_ref[k * SPAN:(k + 1) * SPAN, :] = nat[:, :dq]
            lse_ref[k * SPAN:(k + 1) * SPAN, :] = nat[:, dq:]


def _attention_b(qkv, *, dil, window):
    batch, _, seq_len, width = qkv.shape
    dq = N_HEADS * HEAD_DIM
    seq = seq_len * dil
    tk = min(TQ + 2 * window, seq_len)
    offsets = (0,) if tk == seq_len else (0, window, 2 * window)
    if dil == 1:
        n_cls, n_span = SPAN // TQ, seq // SPAN
        in_spec = pl.BlockSpec((None, None, seq_len, width), lambda b, i: (b, 0, 0, 0))
    else:
        n_cls = dil
        n_span = seq_len // TQ
        in_spec = pl.BlockSpec((None, dil, seq_len, width), lambda b, i: (b, 0, 0, 0))
    rows = n_cls * TQ
    kern = functools.partial(_attn_b_kernel, n_cls=n_cls, seq_len=seq_len, tk=tk, window=window,
                             unit=dil, offsets=offsets, dil=dil)
    return pl.pallas_call(
        kern,
        grid=(batch, n_span),
        in_specs=[in_spec],
        out_specs=[pl.BlockSpec((rows, dq), lambda b, i: (b * n_span + i, 0)),
                   pl.BlockSpec((rows, LANES), lambda b, i: (b * n_span + i, 0))],
        out_shape=[jax.ShapeDtypeStruct((batch * seq, dq), BF16),
                   jax.ShapeDtypeStruct((batch * seq, LANES), BF16)],
        scratch_shapes=[pltpu.VMEM((len(offsets), N_TILES, HEADS_PER_TILE * TQ, tk), F32),
                        pltpu.VMEM((rows, dq + LANES), BF16)],
        compiler_params=_params("arbitrary", "arbitrary"),
    )(qkv)


def _attn_a_kernel(sink_ref, q_ref, k_ref, v_ref, o_ref, bias_ref, *, n_tiles, seq_len, tk, window, offsets):
    @pl.when((pl.program_id(0) == 0) & (pl.program_id(1) == 0))
    def _():
        _fill_bias(bias_ref, tk=tk, window=window, unit=1, offsets=offsets)

    i = pl.program_id(1)
    lg = _lane_group((TQ, MXU_DIM))

    def tile(t, carry):
        o0 = pl.multiple_of(t * TQ, TQ)
        q0 = pl.multiple_of((i * n_tiles + t) * TQ, TQ)
        start, var = _window_start(q0, window, tk, seq_len)
        k4 = k_ref[pl.ds(start, tk), :]
        v4 = v_ref[pl.ds(start, tk), :]
        qs = [q_ref[pl.ds(o0, TQ), j * MXU_DIM:(j + 1) * MXU_DIM] for j in range(HEADS_PER_TILE)]
        outs = [jnp.zeros((TQ, MXU_DIM), F32) for _ in range(HEADS_PER_TILE)]
        for c in range(KV_A):
            q_stack = jnp.concatenate([jnp.where(lg == c, q, jnp.zeros_like(q)) for q in qs], axis=0)
            pv, rden, lse = _softmax_pv(q_stack, k4, v4, bias_ref[var, c])
            for j in range(HEADS_PER_TILE):
                rows = slice(j * TQ, (j + 1) * TQ)
                gate = rden[rows] / (1.0 + jnp.exp(sink_ref[HEADS_PER_TILE * c + j] - lse[rows]))
                outs[j] = jnp.where(lg == c, pv[rows] * gate, outs[j])
        for j in range(HEADS_PER_TILE):
            o_ref[pl.ds(o0, TQ), j * MXU_DIM:(j + 1) * MXU_DIM] = outs[j].astype(o_ref.dtype)
        return carry

    lax.fori_loop(0, n_tiles, tile, 0)


def _attention_a(qkv, sink):
    b, s, _ = qkv.shape
    dq = N_HEADS * HEAD_DIM
    dkv = KV_A * HEAD_DIM
    n_tiles = SPAN // TQ
    tk = TQ + 2 * WINDOW_A
    offsets = (0, WINDOW_A, 2 * WINDOW_A)
    kern = functools.partial(_attn_a_kernel, n_tiles=n_tiles, seq_len=s, tk=tk, window=WINDOW_A,
                             offsets=offsets)
    return pl.pallas_call(
        kern,
        grid=(b, s // SPAN),
        in_specs=[pl.BlockSpec(memory_space=pltpu.SMEM),
                  pl.BlockSpec((None, SPAN, dq), lambda bi, i: (bi, i, 0)),
                  pl.BlockSpec((None, s, dkv), lambda bi, i: (bi, 0, dq // dkv)),
                  pl.BlockSpec((None, s, dkv), lambda bi, i: (bi, 0, dq // dkv + 1))],
        out_specs=pl.BlockSpec((None, SPAN, dq), lambda bi, i: (bi, i, 0)),
        out_shape=jax.ShapeDtypeStruct((b, s, dq), BF16),
        scratch_shapes=[pltpu.VMEM((len(offsets), KV_A, HEADS_PER_TILE * TQ, tk), F32)],
        compiler_params=_params("arbitrary", "arbitrary"),
    )(sink, qkv, qkv, qkv)


def _post_kernel(*refs, n_groups, final, ff_chunks):
    refs = list(refs)
    x_ref = refs.pop(0)
    o_refs = [refs.pop(0) for _ in range(n_groups)]
    lse_refs = [refs.pop(0) for _ in range(n_groups)] if n_groups > 1 else []
    wo_ref, gf_ref, wg_ref, wu_ref, wd_ref = refs[:5]
    refs = refs[5:]
    gfin_ref = refs.pop(0) if final else None
    out_ref = refs.pop(0)

    if n_groups > 1:
        src = lax.broadcasted_iota(jnp.int32, (LANES, D_MODEL), 0)
        dst = lax.broadcasted_iota(jnp.int32, (LANES, D_MODEL), 1)
        expand = jnp.where((src % N_HEADS == dst // HEAD_DIM) & (src < LSE_PARTS * N_HEADS), 1.0, 0.0)
        expand = expand.astype(BF16)
        lses = [jnp.dot(r[...], expand, preferred_element_type=F32) for r in lse_refs]
        mx = functools.reduce(jnp.maximum, lses)
        es = [jnp.exp(l - mx) for l in lses]
        num = functools.reduce(lambda a, b: a + b, [e * r[...].astype(F32) for e, r in zip(es, o_refs)])
        den = functools.reduce(lambda a, b: a + b, es)
        o = (num / den).astype(BF16)
    else:
        o = o_refs[0][...]

    x1 = x_ref[...] + jnp.dot(o, wo_ref[...], preferred_element_type=F32)
    h = _rms(x1, gf_ref[...]).astype(BF16)
    d_ff = wg_ref.shape[1]
    fc = d_ff // ff_chunks
    acc = x1
    for c in range(ff_chunks):
        gate = jnp.dot(h, wg_ref[:, c * fc:(c + 1) * fc], preferred_element_type=F32)
        up = jnp.dot(h, wu_ref[:, c * fc:(c + 1) * fc], preferred_element_type=F32)
        a = (gate / (1.0 + jnp.exp(-gate)) * up).astype(BF16)
        acc = acc + jnp.dot(a, wd_ref[c * fc:(c + 1) * fc, :], preferred_element_type=F32)
    if final:
        acc = _rms(acc, gfin_ref[...])
    out_ref[...] = acc


def _post(x2d, os, lses, wo, gf, wg, wu, wd, gfin, tm, ff_chunks):
    t, d = x2d.shape
    d_ff = wg.shape[1]
    final = gfin is not None
    row = lambda n: pl.BlockSpec((tm, n), lambda i: (i, 0))
    in_specs = [row(d)] * (1 + len(os)) + [row(LANES)] * len(lses)
    in_specs += [_const_spec((d, d)), _const_spec((1, d)), _const_spec((d, d_ff)),
                 _const_spec((d, d_ff)), _const_spec((d_ff, d))]
    args = [x2d, *os, *lses, wo, gf.reshape(1, d), wg, wu, wd]
    if final:
        in_specs.append(_const_spec((1, d)))
        args.append(gfin.reshape(1, d))
    return pl.pallas_call(
        functools.partial(_post_kernel, n_groups=len(os), final=final, ff_chunks=ff_chunks),
        grid=(t // tm,),
        in_specs=in_specs,
        out_specs=row(d),
        out_shape=jax.ShapeDtypeStruct((t, d), F32),
        compiler_params=_params("parallel"),
    )(*args)


def kernel(x, norm_mix, norm_ffn, w_qkv_a, w_out_a, sink_a, w_qkv_b, w_out_b,
           w_gate, w_up, w_down, norm_final):
    b, s, d = x.shape
    t = b * s
    dq = N_HEADS * HEAD_DIM
    scale = HEAD_DIM ** -0.5

    wa = w_qkv_a[0]
    wq = (wa[:, :dq] * scale).reshape(d, KV_A, HEADS_PER_TILE, HEAD_DIM).transpose(0, 2, 1, 3)
    wa = jnp.concatenate([wq.reshape(d, dq), wa[:, dq:]], axis=1).astype(BF16)
    wo_a = w_out_a[0].reshape(KV_A, HEADS_PER_TILE, HEAD_DIM, d).transpose(1, 0, 2, 3).reshape(dq, d)
    x2d = x.reshape(t, d)
    qkv = _norm_matmul(x2d, norm_mix[0], wa, tm=1024).reshape(b, s, -1)
    o_a = _attention_a(qkv, sink_a[0])
    x2d = _post(x2d, [o_a.reshape(t, d)], [], wo_a.astype(BF16), norm_ffn[0],
                w_gate[0].astype(BF16), w_up[0].astype(BF16), w_down[0].astype(BF16),
                None, tm=512, ff_chunks=2)

    n_g = len(DILATED_GROUPS)
    wb = w_qkv_b[0].reshape(d, n_g, 3, dq)
    wb = jnp.concatenate([wb[:, :, :1] * scale, wb[:, :, 1:]], axis=2).reshape(d, n_g, 3 * dq)
    wb = wb.transpose(1, 0, 2).astype(BF16)
    dils = tuple(dil for _, dil in DILATED_GROUPS)
    qkvs = _norm_matmul_dilated(x2d.reshape(b, s, d), norm_mix[1], wb, dils)
    os, lses = [], []
    for qkv_g, (win, dil) in zip(qkvs, DILATED_GROUPS):
        o_g, lse_g = _attention_b(qkv_g, dil=dil, window=win // (2 * dil))
        os.append(o_g)
        lses.append(lse_g)
    out = _post(x2d, os, lses, w_out_b[0].astype(BF16), norm_ffn[1],
                w_gate[1].astype(BF16), w_up[1].astype(BF16), w_down[1].astype(BF16),
                norm_final, tm=512, ff_chunks=2)
    return out.reshape(b, s, d)
```

```python
import functools

import jax
import jax.numpy as jnp
from jax import lax
from jax.experimental import pallas as pl
from jax.experimental.pallas import tpu as pltpu

D_MODEL = 1024
HEAD_DIM = 64
N_HEADS = 16
KV_A = 4
WINDOW_A = 128
DILATED_GROUPS = ((128, 1), (512, 4), (2048, 16))
RMS_EPS = 1e-6
NEG = -1e30
LANES = 128
MXU_DIM = 256
HEADS_PER_TILE = MXU_DIM // HEAD_DIM
N_TILES = N_HEADS // HEADS_PER_TILE
TQ = 128
TILES_PER_GROUP = 4
SPAN = 512
LSE_PARTS = 3
VMEM_LIMIT = 56 * 1024 * 1024

F32 = jnp.float32
BF16 = jnp.bfloat16


def _rms(x, g):
    ms = jnp.mean(x * x, axis=-1, keepdims=True)
    return x * lax.rsqrt(ms + RMS_EPS) * g


def _alibi_slopes(n):
    return [2.0 ** (-8.0 * (i + 1) / n) for i in range(n)]


def _const_spec(shape):
    return pl.BlockSpec(shape, lambda *_: (0,) * len(shape), pipeline_mode=pl.Buffered(1))


def _params(*sem):
    return pltpu.CompilerParams(dimension_semantics=sem, vmem_limit_bytes=VMEM_LIMIT)


def _grid_transpose_perm(n, n_major, n_minor):
    row = lax.broadcasted_iota(jnp.int32, (n, n), 0)
    col = lax.broadcasted_iota(jnp.int32, (n, n), 1)
    src = (row % n_minor) * n_major + row // n_minor
    return jnp.where(col == src, 1.0, 0.0).astype(BF16)


def _permute_rows(perm, x):
    return jnp.dot(perm, x, preferred_element_type=F32).astype(BF16)


def _norm_matmul_kernel(x_ref, g_ref, w_ref, o_ref):
    h = _rms(x_ref[...], g_ref[...]).astype(BF16)
    o_ref[...] = jnp.dot(h, w_ref[...], preferred_element_type=F32).astype(o_ref.dtype)


def _norm_matmul(x2d, g, w, tm):
    t, d = x2d.shape
    n = w.shape[1]
    return pl.pallas_call(
        _norm_matmul_kernel,
        grid=(t // tm,),
        in_specs=[pl.BlockSpec((tm, d), lambda i: (i, 0)), _const_spec((1, d)), _const_spec((d, n))],
        out_specs=pl.BlockSpec((tm, n), lambda i: (i, 0)),
        out_shape=jax.ShapeDtypeStruct((t, n), BF16),
        compiler_params=_params("parallel"),
    )(x2d, g.reshape(1, d), w)


def _norm_matmul_dilated_kernel(x_ref, g_ref, w_ref, *o_refs, dils):
    h = _rms(x_ref[...], g_ref[...]).astype(BF16)
    for gi, (dil, o_ref) in enumerate(zip(dils, o_refs)):
        hg = h if dil == 1 else _permute_rows(_grid_transpose_perm(SPAN, dil, SPAN // dil), h)
        y = jnp.dot(hg, w_ref[gi], preferred_element_type=F32).astype(o_ref.dtype)
        o_ref[0] = y.reshape(o_ref.shape[1:])


def _norm_matmul_dilated(x, g, w, dils):
    b, s, d = x.shape
    n = w.shape[2]
    out_specs = [pl.BlockSpec((1, dil, SPAN // dil, n), lambda bi, i: (bi, 0, i, 0)) for dil in dils]
    out_shape = [jax.ShapeDtypeStruct((b, dil, s // dil, n), BF16) for dil in dils]
    return pl.pallas_call(
        functools.partial(_norm_matmul_dilated_kernel, dils=dils),
        grid=(b, s // SPAN),
        in_specs=[pl.BlockSpec((None, SPAN, d), lambda bi, i: (bi, i, 0)),
                  _const_spec((1, d)), _const_spec(w.shape)],
        out_specs=out_specs,
        out_shape=out_shape,
        compiler_params=_params("parallel", "parallel"),
    )(x, g.reshape(1, d), w)


def _lane_group(shape):
    return lax.broadcasted_iota(jnp.int32, shape, 1) // HEAD_DIM


def _fill_bias(bias_ref, *, tk, window, unit, offsets):
    slopes = _alibi_slopes(N_HEADS)
    key = lax.broadcasted_iota(jnp.int32, (tk, TQ), 0)
    qry = lax.broadcasted_iota(jnp.int32, (tk, TQ), 1)
    for v, off in enumerate(offsets):
        dist = jnp.abs(key - qry - off)
        negd = jnp.where(dist <= window, -(dist.astype(F32) * float(unit)), NEG)
        for c in range(N_TILES):
            for j in range(HEADS_PER_TILE):
                bias_ref[v, c, :, j * TQ:(j + 1) * TQ] = slopes[HEADS_PER_TILE * c + j] * negd


def _stack_heads(tiles, groups):
    lg = _lane_group((TQ, MXU_DIM))
    return jnp.concatenate(
        [jnp.where(lg == g, t, jnp.zeros_like(t)) for t, g in zip(tiles, groups)], axis=0)


def _scores(q_stack, k4, bias):
    return lax.dot_general(k4, q_stack, (((1,), (1,)), ((), ())), preferred_element_type=F32) + bias


def _softmax_pv(sc, v4t):
    m = jnp.max(sc, axis=0, keepdims=True)
    p = jnp.exp(sc - m)
    den = jnp.sum(p, axis=0, keepdims=True)
    pv = jnp.dot(v4t, p.astype(BF16), preferred_element_type=F32)
    return pv, m, den


def _pipeline(stages, start_fn, finish_fn):
    nxt = start_fn(stages[0])
    for k, st in enumerate(stages):
        cur = nxt
        if k + 1 < len(stages):
            nxt = start_fn(stages[k + 1])
        finish_fn(st, cur)


def _head_block(x, row_group, col_block):
    return x[row_group * HEAD_DIM:(row_group + 1) * HEAD_DIM, col_block * TQ:(col_block + 1) * TQ]


def _window_start(q0, window, tk, seq_len):
    start = jnp.clip(q0 - window, 0, seq_len - tk)
    return pl.multiple_of(start, min(window, TQ)), (q0 - start) // window


def _split_f32(x):
    parts = []
    for _ in range(LSE_PARTS):
        p = x.astype(BF16)
        parts.append(p)
        x = x - p.astype(F32)
    return parts


def _attn_b_tiles(qkv_ref, tiles, bias_ref, stage_ref, *, tk, window, seq_len):
    dq = N_HEADS * HEAD_DIM
    head_row = lax.broadcasted_iota(jnp.int32, (N_HEADS, TQ), 0)
    windows = [_window_start(q0, window, tk, seq_len) for _, _, q0, _ in tiles]
    acc = {}

    def start(stage):
        t, c = stage
        idx, r0, _, _ = tiles[t]
        win0, var = windows[t]
        lo = c * MXU_DIM
        q4 = qkv_ref[idx + (pl.ds(r0, TQ), slice(lo, lo + MXU_DIM))]
        k4 = qkv_ref[idx + (pl.ds(win0, tk), slice(dq + lo, dq + lo + MXU_DIM))]
        v4 = qkv_ref[idx + (pl.ds(win0, tk), slice(2 * dq + lo, 2 * dq + lo + MXU_DIM))]
        q_stack = _stack_heads([q4] * HEADS_PER_TILE, range(HEADS_PER_TILE))
        return _scores(q_stack, k4, bias_ref[var, c]), v4.T

    def finish(stage, started):
        t, c = stage
        pv, m, den = _softmax_pv(*started)
        rden = 1.0 / den
        lse = m + jnp.log(den)
        outs, lse16 = acc.get(t, ([], jnp.zeros((N_HEADS, TQ), F32)))
        for j in range(HEADS_PER_TILE):
            cols = slice(j * TQ, (j + 1) * TQ)
            outs.append((_head_block(pv, j, j) * rden[:, cols]).astype(BF16))
            lse16 = jnp.where(head_row == HEADS_PER_TILE * c + j, lse[:, cols], lse16)
        acc[t] = (outs, lse16)
        if c == N_TILES - 1:
            pad = jnp.zeros((LANES - LSE_PARTS * N_HEADS, TQ), BF16)
            tile_t = jnp.concatenate(outs + _split_f32(lse16) + [pad], axis=0)
            stage_ref[pl.ds(tiles[t][3], TQ), :] = tile_t.T

    _pipeline([(t, c) for t in range(len(tiles)) for c in range(N_TILES)], start, finish)


def _attn_b_kernel(qkv_ref, o_ref, lse_ref, bias_ref, stage_ref, *, n_cls, seq_len, tk, window, unit,
                   offsets, dil):
    @pl.when((pl.program_id(0) == 0) & (pl.program_id(1) == 0))
    def _():
        _fill_bias(bias_ref, tk=tk, window=window, unit=unit, offsets=offsets)

    i = pl.program_id(1)
    dq = N_HEADS * HEAD_DIM
    kw = dict(tk=tk, window=window, seq_len=seq_len)

    def tile_group(g, carry):
        tiles = []
        for u in range(TILES_PER_GROUP):
            t = g * TILES_PER_GROUP + u
            s0 = pl.multiple_of(t * TQ, TQ)
            if dil == 1:
                r0 = pl.multiple_of((i * n_cls + t) * TQ, TQ)
                tiles.append(((), r0, r0, s0))
            else:
                q0 = (i * TQ) % seq_len
                tiles.append(((t,), pl.multiple_of(q0, TQ), q0, s0))
        _attn_b_tiles(qkv_ref, tiles, bias_ref, stage_ref, **kw)
        return carry

    if n_cls == TILES_PER_GROUP:
        tile_group(0, 0)
    else:
        lax.fori_loop(0, n_cls // TILES_PER_GROUP, tile_group, 0)

    if dil == 1:
        o_ref[...] = stage_ref[:, :dq]
        lse_ref[...] = stage_ref[:, dq:]
    else:
        per_cls = SPAN // dil
        perm = _grid_transpose_perm(SPAN, per_cls, dil)
        for k in range(n_cls * TQ // SPAN):
            rows = [stage_ref[r * TQ + k * per_cls:r * TQ + (k + 1) * per_cls, :] for r in range(dil)]
            nat = _permute_rows(perm, jnp.concatenate(rows, axis=0))
            o_ref[k * SPAN:(k + 1) * SPAN, :] = nat[:, :dq]
            lse_ref[k * SPAN:(k + 1) * SPAN, :] = nat[:, dq:]


def _attention_b(qkv, *, dil, window):
    batch, _, seq_len, width = qkv.shape
    dq = N_HEADS * HEAD_DIM
    seq = seq_len * dil
    tk = min(TQ + 2 * window, seq_len)
    offsets = (0,) if tk == seq_len else (0, window, 2 * window)
    if dil == 1:
        n_cls, n_span = SPAN // TQ, seq // SPAN
        in_spec = pl.BlockSpec((None, None, seq_len, width), lambda b, i: (b, 0, 0, 0))
    else:
        n_cls = dil
        n_span = seq_len // TQ
        in_spec = pl.BlockSpec((None, dil, seq_len, width), lambda b, i: (b, 0, 0, 0))
    rows = n_cls * TQ
    kern = functools.partial(_attn_b_kernel, n_cls=n_cls, seq_len=seq_len, tk=tk, window=window,
                             unit=dil, offsets=offsets, dil=dil)
    return pl.pallas_call(
        kern,
        grid=(batch, n_span),
        in_specs=[in_spec],
        out_specs=[pl.BlockSpec((rows, dq), lambda b, i: (b * n_span + i, 0)),
                   pl.BlockSpec((rows, LANES), lambda b, i: (b * n_span + i, 0))],
        out_shape=[jax.ShapeDtypeStruct((batch * seq, dq), BF16),
                   jax.ShapeDtypeStruct((batch * seq, LANES), BF16)],
        scratch_shapes=[pltpu.VMEM((len(offsets), N_TILES, tk, HEADS_PER_TILE * TQ), F32),
                        pltpu.VMEM((rows, dq + LANES), BF16)],
        compiler_params=_params("arbitrary", "arbitrary"),
    )(qkv)


def _attn_a_kernel(sink_ref, q_ref, k_ref, v_ref, o_ref, bias_ref, *, n_tiles, seq_len, tk, window, offsets):
    @pl.when((pl.program_id(0) == 0) & (pl.program_id(1) == 0))
    def _():
        _fill_bias(bias_ref, tk=tk, window=window, unit=1, offsets=offsets)

    i = pl.program_id(1)
    qblk = lax.broadcasted_iota(jnp.int32, (1, HEADS_PER_TILE * TQ), 1) // TQ

    tiles = []
    for t in range(n_tiles):
        q0 = pl.multiple_of((i * n_tiles + t) * TQ, TQ)
        tiles.append(_window_start(q0, window, tk, seq_len))
    shared = {}
    blocks = {}

    def start(stage):
        t, c = stage
        win0, var = tiles[t]
        if c == 0:
            qs = [q_ref[t * TQ:(t + 1) * TQ, j * MXU_DIM:(j + 1) * MXU_DIM] for j in range(HEADS_PER_TILE)]
            shared[t] = (qs, k_ref[pl.ds(win0, tk), :], v_ref[pl.ds(win0, tk), :].T)
        qs, k4, v4t = shared[t]
        return _scores(_stack_heads(qs, [c] * HEADS_PER_TILE), k4, bias_ref[var, c]), v4t

    def finish(stage, started):
        t, c = stage
        pv, m, den = _softmax_pv(*started)
        sink = jnp.zeros_like(m)
        for j in range(HEADS_PER_TILE):
            sink = jnp.where(qblk == j, sink_ref[HEADS_PER_TILE * c + j], sink)
        gate = 1.0 / (den + jnp.exp(sink - m))
        for j in range(HEADS_PER_TILE):
            blocks[t, j, c] = (_head_block(pv, c, j) * gate[:, j * TQ:(j + 1) * TQ]).astype(o_ref.dtype)
        if c == KV_A - 1:
            tile_t = jnp.concatenate(
                [blocks.pop((t, j, cc)) for j in range(HEADS_PER_TILE) for cc in range(KV_A)], axis=0)
            o_ref[t * TQ:(t + 1) * TQ, :] = tile_t.T

    _pipeline([(t, c) for t in range(n_tiles) for c in range(KV_A)], start, finish)


def _attention_a(qkv, sink):
    b, s, _ = qkv.shape
    dq = N_HEADS * HEAD_DIM
    dkv = KV_A * HEAD_DIM
    n_tiles = SPAN // TQ
    tk = TQ + 2 * WINDOW_A
    offsets = (0, WINDOW_A, 2 * WINDOW_A)
    kern = functools.partial(_attn_a_kernel, n_tiles=n_tiles, seq_len=s, tk=tk, window=WINDOW_A,
                             offsets=offsets)
    return pl.pallas_call(
        kern,
        grid=(b, s // SPAN),
        in_specs=[pl.BlockSpec(memory_space=pltpu.SMEM),
                  pl.BlockSpec((None, SPAN, dq), lambda bi, i: (bi, i, 0)),
                  pl.BlockSpec((None, s, dkv), lambda bi, i: (bi, 0, dq // dkv)),
                  pl.BlockSpec((None, s, dkv), lambda bi, i: (bi, 0, dq // dkv + 1))],
        out_specs=pl.BlockSpec((None, SPAN, dq), lambda bi, i: (bi, i, 0)),
        out_shape=jax.ShapeDtypeStruct((b, s, dq), BF16),
        scratch_shapes=[pltpu.VMEM((len(offsets), KV_A, tk, HEADS_PER_TILE * TQ), F32)],
        compiler_params=_params("arbitrary", "arbitrary"),
    )(sink, qkv, qkv, qkv)


def _post_kernel(*refs, n_groups, final, ff_chunks):
    refs = list(refs)
    x_ref = refs.pop(0)
    o_refs = [refs.pop(0) for _ in range(n_groups)]
    lse_refs = [refs.pop(0) for _ in range(n_groups)] if n_groups > 1 else []
    wo_ref, gf_ref, wg_ref, wu_ref, wd_ref = refs[:5]
    refs = refs[5:]
    gfin_ref = refs.pop(0) if final else None
    out_ref = refs.pop(0)

    if n_groups > 1:
        src = lax.broadcasted_iota(jnp.int32, (LANES, D_MODEL), 0)
        dst = lax.broadcasted_iota(jnp.int32, (LANES, D_MODEL), 1)
        expand = jnp.where((src % N_HEADS == dst // HEAD_DIM) & (src < LSE_PARTS * N_HEADS), 1.0, 0.0)
        expand = expand.astype(BF16)
        lses = [jnp.dot(r[...], expand, preferred_element_type=F32) for r in lse_refs]
        mx = functools.reduce(jnp.maximum, lses)
        es = [jnp.exp(l - mx) for l in lses]
        num = functools.reduce(lambda a, b: a + b, [e * r[...].astype(F32) for e, r in zip(es, o_refs)])
        den = functools.reduce(lambda a, b: a + b, es)
        o = (num / den).astype(BF16)
    else:
        o = o_refs[0][...]

    x1 = x_ref[...] + jnp.dot(o, wo_ref[...], preferred_element_type=F32)
    h = _rms(x1, gf_ref[...]).astype(BF16)
    d_ff = wg_ref.shape[1]
    fc = d_ff // ff_chunks
    acc = x1
    for c in range(ff_chunks):
        gate = jnp.dot(h, wg_ref[:, c * fc:(c + 1) * fc], preferred_element_type=F32)
        up = jnp.dot(h, wu_ref[:, c * fc:(c + 1) * fc], preferred_element_type=F32)
        a = (gate / (1.0 + jnp.exp(-gate)) * up).astype(BF16)
        acc = acc + jnp.dot(a, wd_ref[c * fc:(c + 1) * fc, :], preferred_element_type=F32)
    if final:
        acc = _rms(acc, gfin_ref[...])
    out_ref[...] = acc


def _post(x2d, os, lses, wo, gf, wg, wu, wd, gfin, tm, ff_chunks):
    t, d = x2d.shape
    d_ff = wg.shape[1]
    final = gfin is not None
    row = lambda n: pl.BlockSpec((tm, n), lambda i: (i, 0))
    in_specs = [row(d)] * (1 + len(os)) + [row(LANES)] * len(lses)
    in_specs += [_const_spec((d, d)), _const_spec((1, d)), _const_spec((d, d_ff)),
                 _const_spec((d, d_ff)), _const_spec((d_ff, d))]
    args = [x2d, *os, *lses, wo, gf.reshape(1, d), wg, wu, wd]
    if final:
        in_specs.append(_const_spec((1, d)))
        args.append(gfin.reshape(1, d))
    return pl.pallas_call(
        functools.partial(_post_kernel, n_groups=len(os), final=final, ff_chunks=ff_chunks),
        grid=(t // tm,),
        in_specs=in_specs,
        out_specs=row(d),
        out_shape=jax.ShapeDtypeStruct((t, d), F32),
        compiler_params=_params("parallel"),
    )(*args)


def kernel(x, norm_mix, norm_ffn, w_qkv_a, w_out_a, sink_a, w_qkv_b, w_out_b,
           w_gate, w_up, w_down, norm_final):
    b, s, d = x.shape
    t = b * s
    dq = N_HEADS * HEAD_DIM
    scale = HEAD_DIM ** -0.5

    wa = w_qkv_a[0]
    wq = (wa[:, :dq] * scale).reshape(d, KV_A, HEADS_PER_TILE, HEAD_DIM).transpose(0, 2, 1, 3)
    wa = jnp.concatenate([wq.reshape(d, dq), wa[:, dq:]], axis=1).astype(BF16)
    wo_a = w_out_a[0].reshape(KV_A, HEADS_PER_TILE, HEAD_DIM, d).transpose(1, 0, 2, 3).reshape(dq, d)
    x2d = x.reshape(t, d)
    qkv = _norm_matmul(x2d, norm_mix[0], wa, tm=1024).reshape(b, s, -1)
    o_a = _attention_a(qkv, sink_a[0])
    x2d = _post(x2d, [o_a.reshape(t, d)], [], wo_a.astype(BF16), norm_ffn[0],
                w_gate[0].astype(BF16), w_up[0].astype(BF16), w_down[0].astype(BF16),
                None, tm=512, ff_chunks=2)

    n_g = len(DILATED_GROUPS)
    wb = w_qkv_b[0].reshape(d, n_g, 3, dq)
    wb = jnp.concatenate([wb[:, :, :1] * scale, wb[:, :, 1:]], axis=2).reshape(d, n_g, 3 * dq)
    wb = wb.transpose(1, 0, 2).astype(BF16)
    dils = tuple(dil for _, dil in DILATED_GROUPS)
    qkvs = _norm_matmul_dilated(x2d.reshape(b, s, d), norm_mix[1], wb, dils)
    os, lses = [], []
    for qkv_g, (win, dil) in zip(qkvs, DILATED_GROUPS):
        o_g, lse_g = _attention_b(qkv_g, dil=dil, window=win // (2 * dil))
        os.append(o_g)
        lses.append(lse_g)
    out = _post(x2d, os, lses, w_out_b[0].astype(BF16), norm_ffn[1],
                w_gate[1].astype(BF16), w_up[1].astype(BF16), w_down[1].astype(BF16),
                norm_final, tm=512, ff_chunks=2)
    return out.reshape(b, s, d)
```

```python
import functools

import jax
import jax.numpy as jnp
from jax import lax
from jax.experimental import pallas as pl
from jax.experimental.pallas import tpu as pltpu

D_MODEL = 1024
HEAD_DIM = 64
N_HEADS = 16
KV_A = 4
WINDOW_A = 128
DILATED_GROUPS = ((128, 1), (512, 4), (2048, 16))
RMS_EPS = 1e-6
NEG = -1e30
LANES = 128
MXU_DIM = 256
HEADS_PER_TILE = MXU_DIM // HEAD_DIM
N_TILES = N_HEADS // HEADS_PER_TILE
TQ = 128
TILES_PER_GROUP = 4
SPAN = 512
LSE_PARTS = 3
DEN_ROWS = 16
LOG2E = 1.4426950408889634
LN2 = 0.6931471805599453
VMEM_LIMIT = 56 * 1024 * 1024

F32 = jnp.float32
BF16 = jnp.bfloat16


def _rms(x, g):
    ms = jnp.mean(x * x, axis=-1, keepdims=True)
    return x * lax.rsqrt(ms + RMS_EPS) * g


def _alibi_slopes(n):
    return [2.0 ** (-8.0 * (i + 1) / n) for i in range(n)]


def _const_spec(shape):
    return pl.BlockSpec(shape, lambda *_: (0,) * len(shape), pipeline_mode=pl.Buffered(1))


def _params(*sem):
    return pltpu.CompilerParams(dimension_semantics=sem, vmem_limit_bytes=VMEM_LIMIT)


def _grid_transpose_perm(n, n_major, n_minor):
    row = lax.broadcasted_iota(jnp.int32, (n, n), 0)
    col = lax.broadcasted_iota(jnp.int32, (n, n), 1)
    src = (row % n_minor) * n_major + row // n_minor
    return jnp.where(col == src, 1.0, 0.0).astype(BF16)


def _permute_rows(perm, x):
    return jnp.dot(perm, x, preferred_element_type=F32).astype(BF16)


def _norm_matmul_kernel(x_ref, g_ref, w_ref, o_ref):
    h = _rms(x_ref[...], g_ref[...]).astype(BF16)
    o_ref[...] = jnp.dot(h, w_ref[...], preferred_element_type=F32).astype(o_ref.dtype)


def _norm_matmul(x2d, g, w, tm):
    t, d = x2d.shape
    n = w.shape[1]
    return pl.pallas_call(
        _norm_matmul_kernel,
        grid=(t // tm,),
        in_specs=[pl.BlockSpec((tm, d), lambda i: (i, 0)), _const_spec((1, d)), _const_spec((d, n))],
        out_specs=pl.BlockSpec((tm, n), lambda i: (i, 0)),
        out_shape=jax.ShapeDtypeStruct((t, n), BF16),
        compiler_params=_params("parallel"),
    )(x2d, g.reshape(1, d), w)


def _norm_matmul_dilated_kernel(x_ref, g_ref, w_ref, *o_refs, dils):
    h = _rms(x_ref[...], g_ref[...]).astype(BF16)
    for gi, (dil, o_ref) in enumerate(zip(dils, o_refs)):
        hg = h if dil == 1 else _permute_rows(_grid_transpose_perm(SPAN, dil, SPAN // dil), h)
        y = jnp.dot(hg, w_ref[gi], preferred_element_type=F32).astype(o_ref.dtype)
        o_ref[0] = y.reshape(o_ref.shape[1:])


def _norm_matmul_dilated(x, g, w, dils):
    b, s, d = x.shape
    n = w.shape[2]
    out_specs = [pl.BlockSpec((1, dil, SPAN // dil, n), lambda bi, i: (bi, 0, i, 0)) for dil in dils]
    out_shape = [jax.ShapeDtypeStruct((b, dil, s // dil, n), BF16) for dil in dils]
    return pl.pallas_call(
        functools.partial(_norm_matmul_dilated_kernel, dils=dils),
        grid=(b, s // SPAN),
        in_specs=[pl.BlockSpec((None, SPAN, d), lambda bi, i: (bi, i, 0)),
                  _const_spec((1, d)), _const_spec(w.shape)],
        out_specs=out_specs,
        out_shape=out_shape,
        compiler_params=_params("parallel", "parallel"),
    )(x, g.reshape(1, d), w)


def _lane_group(shape):
    return lax.broadcasted_iota(jnp.int32, shape, 1) // HEAD_DIM


def _fill_bias(bias_ref, *, tk, window, unit, offsets):
    slopes = _alibi_slopes(N_HEADS)
    key = lax.broadcasted_iota(jnp.int32, (tk, TQ), 0)
    qry = lax.broadcasted_iota(jnp.int32, (tk, TQ), 1)
    for v, off in enumerate(offsets):
        dist = jnp.abs(key - qry - off)
        negd = jnp.where(dist <= window, -(dist.astype(F32) * float(unit)), NEG)
        for c in range(N_TILES):
            for j in range(HEADS_PER_TILE):
                bias_ref[v, c, :, j * TQ:(j + 1) * TQ] = (slopes[HEADS_PER_TILE * c + j] * LOG2E) * negd


def _stack_heads(tiles, groups):
    lg = _lane_group((TQ, MXU_DIM))
    return jnp.concatenate(
        [jnp.where(lg == g, t, jnp.zeros_like(t)) for t, g in zip(tiles, groups)], axis=0)


def _scores(q_stack, k4, bias):
    return lax.dot_general(k4, q_stack, (((1,), (1,)), ((), ())), preferred_element_type=F32) + bias


def _values_t(v4):
    return jnp.concatenate([v4.T, jnp.ones((DEN_ROWS, v4.shape[0]), v4.dtype)], axis=0)


def _softmax_pv(sc, v4t):
    m = jnp.max(sc, axis=0, keepdims=True)
    p = jnp.exp2(sc - m)
    pv = jnp.dot(v4t, p.astype(BF16), preferred_element_type=F32)
    return pv[:MXU_DIM], m, pv[MXU_DIM:MXU_DIM + 1]


def _pipeline(stages, start_fn, finish_fn):
    nxt = start_fn(stages[0])
    for k, st in enumerate(stages):
        cur = nxt
        if k + 1 < len(stages):
            nxt = start_fn(stages[k + 1])
        finish_fn(st, cur)


def _head_block(x, row_group, col_block):
    return x[row_group * HEAD_DIM:(row_group + 1) * HEAD_DIM, col_block * TQ:(col_block + 1) * TQ]


def _window_start(q0, window, tk, seq_len):
    start = jnp.clip(q0 - window, 0, seq_len - tk)
    return pl.multiple_of(start, min(window, TQ)), (q0 - start) // window


def _split_f32(x):
    parts = []
    for _ in range(LSE_PARTS):
        p = x.astype(BF16)
        parts.append(p)
        x = x - p.astype(F32)
    return parts


def _attn_b_tiles(qkv_ref, tiles, bias_ref, stage_ref, *, tk, window, seq_len):
    dq = N_HEADS * HEAD_DIM
    head_row = lax.broadcasted_iota(jnp.int32, (N_HEADS, TQ), 0)
    windows = [_window_start(q0, window, tk, seq_len) for _, _, q0, _ in tiles]
    acc = {}

    def start(stage):
        t, c = stage
        idx, r0, _, _ = tiles[t]
        win0, var = windows[t]
        lo = c * MXU_DIM
        q4 = qkv_ref[idx + (pl.ds(r0, TQ), slice(lo, lo + MXU_DIM))]
        k4 = qkv_ref[idx + (pl.ds(win0, tk), slice(dq + lo, dq + lo + MXU_DIM))]
        v4 = qkv_ref[idx + (pl.ds(win0, tk), slice(2 * dq + lo, 2 * dq + lo + MXU_DIM))]
        q_stack = _stack_heads([q4] * HEADS_PER_TILE, range(HEADS_PER_TILE))
        return _scores(q_stack, k4, bias_ref[var, c]), _values_t(v4)

    def finish(stage, started):
        t, c = stage
        pv, m, den = _softmax_pv(*started)
        rden = 1.0 / den
        lse = (m + jnp.log2(den)) * LN2
        outs, lse16 = acc.get(t, ([], jnp.zeros((N_HEADS, TQ), F32)))
        for j in range(HEADS_PER_TILE):
            cols = slice(j * TQ, (j + 1) * TQ)
            outs.append((_head_block(pv, j, j) * rden[:, cols]).astype(BF16))
            lse16 = jnp.where(head_row == HEADS_PER_TILE * c + j, lse[:, cols], lse16)
        acc[t] = (outs, lse16)
        if c == N_TILES - 1:
            pad = jnp.zeros((LANES - LSE_PARTS * N_HEADS, TQ), BF16)
            tile_t = jnp.concatenate(outs + _split_f32(lse16) + [pad], axis=0)
            stage_ref[pl.ds(tiles[t][3], TQ), :] = tile_t.T

    _pipeline([(t, c) for t in range(len(tiles)) for c in range(N_TILES)], start, finish)


def _attn_b_kernel(qkv_ref, o_ref, lse_ref, bias_ref, stage_ref, *, n_cls, seq_len, tk, window, unit,
                   offsets, dil):
    @pl.when((pl.program_id(0) == 0) & (pl.program_id(1) == 0))
    def _():
        _fill_bias(bias_ref, tk=tk, window=window, unit=unit, offsets=offsets)

    i = pl.program_id(1)
    dq = N_HEADS * HEAD_DIM
    kw = dict(tk=tk, window=window, seq_len=seq_len)

    def tile_group(g, carry):
        tiles = []
        for u in range(TILES_PER_GROUP):
            t = g * TILES_PER_GROUP + u
            s0 = pl.multiple_of(t * TQ, TQ)
            if dil == 1:
                r0 = pl.multiple_of((i * n_cls + t) * TQ, TQ)
                tiles.append(((), r0, r0, s0))
            else:
                q0 = (i * TQ) % seq_len
                tiles.append(((t,), pl.multiple_of(q0, TQ), q0, s0))
        _attn_b_tiles(qkv_ref, tiles, bias_ref, stage_ref, **kw)
        return carry

    if n_cls == TILES_PER_GROUP:
        tile_group(0, 0)
    else:
        lax.fori_loop(0, n_cls // TILES_PER_GROUP, tile_group, 0)

    if dil == 1:
        o_ref[...] = stage_ref[:, :dq]
        lse_ref[...] = stage_ref[:, dq:]
    else:
        per_cls = SPAN // dil
        perm = _grid_transpose_perm(SPAN, per_cls, dil)
        for k in range(n_cls * TQ // SPAN):
            rows = [stage_ref[r * TQ + k * per_cls:r * TQ + (k + 1) * per_cls, :] for r in range(dil)]
            nat = _permute_rows(perm, jnp.concatenate(rows, axis=0))
            o_ref[k * SPAN:(k + 1) * SPAN, :] = nat[:, :dq]
            lse_ref[k * SPAN:(k + 1) * SPAN, :] = nat[:, dq:]


def _attention_b(qkv, *, dil, window):
    batch, _, seq_len, width = qkv.shape
    dq = N_HEADS * HEAD_DIM
    seq = seq_len * dil
    tk = min(TQ + 2 * window, seq_len)
    offsets = (0,) if tk == seq_len else (0, window, 2 * window)
    if dil == 1:
        n_cls, n_span = SPAN // TQ, seq // SPAN
        in_spec = pl.BlockSpec((None, None, seq_len, width), lambda b, i: (b, 0, 0, 0))
    else:
        n_cls = dil
        n_span = seq_len // TQ
        in_spec = pl.BlockSpec((None, dil, seq_len, width), lambda b, i: (b, 0, 0, 0))
    rows = n_cls * TQ
    kern = functools.partial(_attn_b_kernel, n_cls=n_cls, seq_len=seq_len, tk=tk, window=window,
                             unit=dil, offsets=offsets, dil=dil)
    return pl.pallas_call(
        kern,
        grid=(batch, n_span),
        in_specs=[in_spec],
        out_specs=[pl.BlockSpec((rows, dq), lambda b, i: (b * n_span + i, 0)),
                   pl.BlockSpec((rows, LANES), lambda b, i: (b * n_span + i, 0))],
        out_shape=[jax.ShapeDtypeStruct((batch * seq, dq), BF16),
                   jax.ShapeDtypeStruct((batch * seq, LANES), BF16)],
        scratch_shapes=[pltpu.VMEM((len(offsets), N_TILES, tk, HEADS_PER_TILE * TQ), F32),
                        pltpu.VMEM((rows, dq + LANES), BF16)],
        compiler_params=_params("arbitrary", "arbitrary"),
    )(qkv)


def _attn_a_kernel(sink_ref, q_ref, k_ref, v_ref, o_ref, bias_ref, *, n_tiles, seq_len, tk, window, offsets):
    @pl.when((pl.program_id(0) == 0) & (pl.program_id(1) == 0))
    def _():
        _fill_bias(bias_ref, tk=tk, window=window, unit=1, offsets=offsets)

    i = pl.program_id(1)
    qblk = lax.broadcasted_iota(jnp.int32, (1, HEADS_PER_TILE * TQ), 1) // TQ

    tiles = []
    for t in range(n_tiles):
        q0 = pl.multiple_of((i * n_tiles + t) * TQ, TQ)
        tiles.append(_window_start(q0, window, tk, seq_len))
    shared = {}
    blocks = {}

    def start(stage):
        t, c = stage
        win0, var = tiles[t]
        if c == 0:
            qs = [q_ref[t * TQ:(t + 1) * TQ, j * MXU_DIM:(j + 1) * MXU_DIM] for j in range(HEADS_PER_TILE)]
            shared[t] = (qs, k_ref[pl.ds(win0, tk), :], _values_t(v_ref[pl.ds(win0, tk), :]))
        qs, k4, v4t = shared[t]
        return _scores(_stack_heads(qs, [c] * HEADS_PER_TILE), k4, bias_ref[var, c]), v4t

    def finish(stage, started):
        t, c = stage
        pv, m, den = _softmax_pv(*started)
        sink = jnp.zeros_like(m)
        for j in range(HEADS_PER_TILE):
            sink = jnp.where(qblk == j, sink_ref[HEADS_PER_TILE * c + j] * LOG2E, sink)
        gate = 1.0 / (den + jnp.exp2(sink - m))
        for j in range(HEADS_PER_TILE):
            blocks[t, j, c] = (_head_block(pv, c, j) * gate[:, j * TQ:(j + 1) * TQ]).astype(o_ref.dtype)
        if c == KV_A - 1:
            tile_t = jnp.concatenate(
                [blocks.pop((t, j, cc)) for j in range(HEADS_PER_TILE) for cc in range(KV_A)], axis=0)
            o_ref[t * TQ:(t + 1) * TQ, :] = tile_t.T

    _pipeline([(t, c) for t in range(n_tiles) for c in range(KV_A)], start, finish)


def _attention_a(qkv, sink):
    b, s, _ = qkv.shape
    dq = N_HEADS * HEAD_DIM
    dkv = KV_A * HEAD_DIM
    n_tiles = SPAN // TQ
    tk = TQ + 2 * WINDOW_A
    offsets = (0, WINDOW_A, 2 * WINDOW_A)
    kern = functools.partial(_attn_a_kernel, n_tiles=n_tiles, seq_len=s, tk=tk, window=WINDOW_A,
                             offsets=offsets)
    return pl.pallas_call(
        kern,
        grid=(b, s // SPAN),
        in_specs=[pl.BlockSpec(memory_space=pltpu.SMEM),
                  pl.BlockSpec((None, SPAN, dq), lambda bi, i: (bi, i, 0)),
                  pl.BlockSpec((None, s, dkv), lambda bi, i: (bi, 0, dq // dkv)),
                  pl.BlockSpec((None, s, dkv), lambda bi, i: (bi, 0, dq // dkv + 1))],
        out_specs=pl.BlockSpec((None, SPAN, dq), lambda bi, i: (bi, i, 0)),
        out_shape=jax.ShapeDtypeStruct((b, s, dq), BF16),
        scratch_shapes=[pltpu.VMEM((len(offsets), KV_A, tk, HEADS_PER_TILE * TQ), F32)],
        compiler_params=_params("arbitrary", "arbitrary"),
    )(sink, qkv, qkv, qkv)


def _post_kernel(*refs, n_groups, final, ff_chunks):
    refs = list(refs)
    x_ref = refs.pop(0)
    o_refs = [refs.pop(0) for _ in range(n_groups)]
    lse_refs = [refs.pop(0) for _ in range(n_groups)] if n_groups > 1 else []
    wo_ref, gf_ref, wg_ref, wu_ref, wd_ref = refs[:5]
    refs = refs[5:]
    gfin_ref = refs.pop(0) if final else None
    out_ref = refs.pop(0)

    if n_groups > 1:
        src = lax.broadcasted_iota(jnp.int32, (LANES, D_MODEL), 0)
        dst = lax.broadcasted_iota(jnp.int32, (LANES, D_MODEL), 1)
        expand = jnp.where((src % N_HEADS == dst // HEAD_DIM) & (src < LSE_PARTS * N_HEADS), 1.0, 0.0)
        expand = expand.astype(BF16)
        lses = [jnp.dot(r[...], expand, preferred_element_type=F32) for r in lse_refs]
        mx = functools.reduce(jnp.maximum, lses)
        es = [jnp.exp(l - mx) for l in lses]
        num = functools.reduce(lambda a, b: a + b, [e * r[...].astype(F32) for e, r in zip(es, o_refs)])
        den = functools.reduce(lambda a, b: a + b, es)
        o = (num / den).astype(BF16)
    else:
        o = o_refs[0][...]

    x1 = x_ref[...] + jnp.dot(o, wo_ref[...], preferred_element_type=F32)
    h = _rms(x1, gf_ref[...]).astype(BF16)
    acc = x1
    lo = 0
    for fc in ff_chunks:
        gate = jnp.dot(h, wg_ref[:, lo:lo + fc], preferred_element_type=F32)
        up = jnp.dot(h, wu_ref[:, lo:lo + fc], preferred_element_type=F32)
        a = (gate / (1.0 + jnp.exp(-gate)) * up).astype(BF16)
        acc = acc + jnp.dot(a, wd_ref[lo:lo + fc, :], preferred_element_type=F32)
        lo += fc
    if final:
        acc = _rms(acc, gfin_ref[...])
    out_ref[...] = acc


def _ff_chunks(d_ff, n):
    tiles, rem = divmod(d_ff, MXU_DIM)
    assert rem == 0
    return tuple((tiles // n + (k < tiles % n)) * MXU_DIM for k in range(n))


def _post(x2d, os, lses, wo, gf, wg, wu, wd, gfin, tm, ff_chunks):
    t, d = x2d.shape
    d_ff = wg.shape[1]
    ff_chunks = _ff_chunks(d_ff, ff_chunks)
    final = gfin is not None
    row = lambda n: pl.BlockSpec((tm, n), lambda i: (i, 0))
    in_specs = [row(d)] * (1 + len(os)) + [row(LANES)] * len(lses)
    in_specs += [_const_spec((d, d)), _const_spec((1, d)), _const_spec((d, d_ff)),
                 _const_spec((d, d_ff)), _const_spec((d_ff, d))]
    args = [x2d, *os, *lses, wo, gf.reshape(1, d), wg, wu, wd]
    if final:
        in_specs.append(_const_spec((1, d)))
        args.append(gfin.reshape(1, d))
    return pl.pallas_call(
        functools.partial(_post_kernel, n_groups=len(os), final=final, ff_chunks=ff_chunks),
        grid=(t // tm,),
        in_specs=in_specs,
        out_specs=row(d),
        out_shape=jax.ShapeDtypeStruct((t, d), F32),
        compiler_params=_params("parallel"),
    )(*args)


def kernel(x, norm_mix, norm_ffn, w_qkv_a, w_out_a, sink_a, w_qkv_b, w_out_b,
           w_gate, w_up, w_down, norm_final):
    b, s, d = x.shape
    t = b * s
    dq = N_HEADS * HEAD_DIM
    scale = HEAD_DIM ** -0.5 * LOG2E

    wa = w_qkv_a[0]
    wq = (wa[:, :dq] * scale).reshape(d, KV_A, HEADS_PER_TILE, HEAD_DIM).transpose(0, 2, 1, 3)
    wa = jnp.concatenate([wq.reshape(d, dq), wa[:, dq:]], axis=1).astype(BF16)
    wo_a = w_out_a[0].reshape(KV_A, HEADS_PER_TILE, HEAD_DIM, d).transpose(1, 0, 2, 3).reshape(dq, d)
    x2d = x.reshape(t, d)
    qkv = _norm_matmul(x2d, norm_mix[0], wa, tm=1024).reshape(b, s, -1)
    o_a = _attention_a(qkv, sink_a[0])
    x2d = _post(x2d, [o_a.reshape(t, d)], [], wo_a.astype(BF16), norm_ffn[0],
                w_gate[0].astype(BF16), w_up[0].astype(BF16), w_down[0].astype(BF16),
                None, tm=512, ff_chunks=2)

    n_g = len(DILATED_GROUPS)
    wb = w_qkv_b[0].reshape(d, n_g, 3, dq)
    wb = jnp.concatenate([wb[:, :, :1] * scale, wb[:, :, 1:]], axis=2).reshape(d, n_g, 3 * dq)
    wb = wb.transpose(1, 0, 2).astype(BF16)
    dils = tuple(dil for _, dil in DILATED_GROUPS)
    qkvs = _norm_matmul_dilated(x2d.reshape(b, s, d), norm_mix[1], wb, dils)
    os, lses = [], []
    for qkv_g, (win, dil) in zip(qkvs, DILATED_GROUPS):
        o_g, lse_g = _attention_b(qkv_g, dil=dil, window=win // (2 * dil))
        os.append(o_g)
        lses.append(lse_g)
    out = _post(x2d, os, lses, w_out_b[0].astype(BF16), norm_ffn[1],
                w_gate[1].astype(BF16), w_up[1].astype(BF16), w_down[1].astype(BF16),
                norm_final, tm=512, ff_chunks=2)
    return out.reshape(b, s, d)
```

```python
import functools

import jax
import jax.numpy as jnp
import numpy as np
from jax import lax
from jax.experimental import pallas as pl
from jax.experimental.pallas import tpu as pltpu

D_MODEL = 1024
HEAD_DIM = 64
N_HEADS = 16
KV_A = 4
WINDOW_A = 128
DILATED_GROUPS = ((128, 1), (512, 4), (2048, 16))
RMS_EPS = 1e-6
NEG = -1e30
LANES = 128
MXU_DIM = 256
HEADS_PER_TILE = MXU_DIM // HEAD_DIM
N_TILES = N_HEADS // HEADS_PER_TILE
TQ = 128
TILES_PER_GROUP = 4
SPAN = 512
REGROUP = MXU_DIM
LSE_PARTS = 3
DEN_ROWS = 16
LOG2E = 1.4426950408889634
VMEM_LIMIT = 56 * 1024 * 1024

F32 = jnp.float32
BF16 = jnp.bfloat16


def _rms(x, g):
    ms = jnp.mean(x * x, axis=-1, keepdims=True)
    return x * lax.rsqrt(ms + RMS_EPS) * g


def _alibi_slopes(n):
    return [2.0 ** (-8.0 * (i + 1) / n) for i in range(n)]


def _const_spec(shape):
    return pl.BlockSpec(shape, lambda *_: (0,) * len(shape), pipeline_mode=pl.Buffered(1))


def _params(*sem):
    return pltpu.CompilerParams(dimension_semantics=sem, vmem_limit_bytes=VMEM_LIMIT)


def _grid_transpose_perm(n, n_major, n_minor):
    row = lax.broadcasted_iota(jnp.int32, (n, n), 0)
    col = lax.broadcasted_iota(jnp.int32, (n, n), 1)
    src = (row % n_minor) * n_major + row // n_minor
    return jnp.where(col == src, 1.0, 0.0).astype(BF16)


def _permute_rows(perm, x):
    return jnp.dot(perm, x, preferred_element_type=F32).astype(BF16)


def _norm_matmul_kernel(x_ref, g_ref, w_ref, o_ref):
    h = _rms(x_ref[...], g_ref[...]).astype(BF16)
    o_ref[...] = jnp.dot(h, w_ref[...], preferred_element_type=F32).astype(o_ref.dtype)


def _norm_matmul(x2d, g, w, tm):
    t, d = x2d.shape
    n = w.shape[1]
    return pl.pallas_call(
        _norm_matmul_kernel,
        grid=(t // tm,),
        in_specs=[pl.BlockSpec((tm, d), lambda i: (i, 0)), _const_spec((1, d)), _const_spec((d, n))],
        out_specs=pl.BlockSpec((tm, n), lambda i: (i, 0)),
        out_shape=jax.ShapeDtypeStruct((t, n), BF16),
        compiler_params=_params("parallel"),
    )(x2d, g.reshape(1, d), w)


def _norm_matmul_dilated_kernel(x_ref, g_ref, w_ref, *o_refs, dils):
    h = _rms(x_ref[...], g_ref[...]).astype(BF16)
    n = o_refs[0].shape[-1]
    for gi, (dil, o_ref) in enumerate(zip(dils, o_refs)):
        hg = h if dil == 1 else _to_classes(h, dil)
        y = jnp.dot(hg, w_ref[:, gi * n:(gi + 1) * n], preferred_element_type=F32).astype(o_ref.dtype)
        o_ref[0] = y.reshape(o_ref.shape[1:])


def _to_classes(h, dil):
    per = REGROUP // dil
    perm = _grid_transpose_perm(REGROUP, dil, per)
    parts = [_permute_rows(perm, h[b * REGROUP:(b + 1) * REGROUP]) for b in range(SPAN // REGROUP)]
    return jnp.concatenate([p[r * per:(r + 1) * per] for r in range(dil) for p in parts], axis=0)


def _norm_matmul_dilated(x, g, w, dils):
    b, s, d = x.shape
    n = w.shape[1] // len(dils)
    out_specs = [pl.BlockSpec((1, dil, SPAN // dil, n), lambda bi, i: (bi, 0, i, 0)) for dil in dils]
    out_shape = [jax.ShapeDtypeStruct((b, dil, s // dil, n), BF16) for dil in dils]
    return pl.pallas_call(
        functools.partial(_norm_matmul_dilated_kernel, dils=dils),
        grid=(b, s // SPAN),
        in_specs=[pl.BlockSpec((None, SPAN, d), lambda bi, i: (bi, i, 0)),
                  _const_spec((1, d)), _const_spec(w.shape)],
        out_specs=out_specs,
        out_shape=out_shape,
        compiler_params=_params("parallel", "parallel"),
    )(x, g.reshape(1, d), w)


def _lane_group(shape):
    return lax.broadcasted_iota(jnp.int32, shape, 1) // HEAD_DIM


def _fill_bias(bias_ref, *, tk, window, unit, offsets):
    slopes = _alibi_slopes(N_HEADS)
    key = lax.broadcasted_iota(jnp.int32, (tk, TQ), 0)
    qry = lax.broadcasted_iota(jnp.int32, (tk, TQ), 1)
    for v, off in enumerate(offsets):
        dist = jnp.abs(key - qry - off)
        negd = jnp.where(dist <= window, -(dist.astype(F32) * float(unit)), NEG)
        for c in range(N_TILES):
            for j in range(HEADS_PER_TILE):
                bias_ref[v, c, :, j * TQ:(j + 1) * TQ] = (slopes[HEADS_PER_TILE * c + j] * LOG2E) * negd


def _stack_heads(tiles, groups):
    lg = _lane_group((TQ, MXU_DIM))
    return jnp.concatenate(
        [jnp.where(lg == g, t, jnp.zeros_like(t)) for t, g in zip(tiles, groups)], axis=0)


def _scores(q_stack, k4, bias):
    return lax.dot_general(k4, q_stack, (((1,), (1,)), ((), ())), preferred_element_type=F32) + bias


def _values_t(v4):
    return jnp.concatenate([v4.T, jnp.ones((DEN_ROWS, v4.shape[0]), v4.dtype)], axis=0)


def _softmax_pv(sc, v4t):
    m = jnp.max(sc, axis=0, keepdims=True)
    p = jnp.exp2(sc - m)
    pv = jnp.dot(v4t, p.astype(BF16), preferred_element_type=F32)
    return pv[:MXU_DIM], m, pv[MXU_DIM:MXU_DIM + 1]


def _pipeline(stages, start_fn, finish_fn):
    nxt = start_fn(stages[0])
    for k, st in enumerate(stages):
        cur = nxt
        if k + 1 < len(stages):
            nxt = start_fn(stages[k + 1])
        finish_fn(st, cur)


def _head_block(x, row_group, col_block):
    return x[row_group * HEAD_DIM:(row_group + 1) * HEAD_DIM, col_block * TQ:(col_block + 1) * TQ]


def _window_start(q0, window, tk, seq_len):
    start = jnp.clip(q0 - window, 0, seq_len - tk)
    return pl.multiple_of(start, min(window, TQ)), (q0 - start) // window


def _split_f32(x):
    parts = []
    for _ in range(LSE_PARTS):
        p = x.astype(BF16)
        parts.append(p)
        x = x - p.astype(F32)
    return parts


def _attn_b_tiles(qkv_ref, tiles, bias_ref, stage_ref, *, tk, window, seq_len):
    dq = N_HEADS * HEAD_DIM
    head_row = lax.broadcasted_iota(jnp.int32, (N_HEADS, TQ), 0)
    windows = [_window_start(q0, window, tk, seq_len) for _, _, q0, _ in tiles]
    acc = {}

    def start(stage):
        t, c = stage
        idx, r0, _, _ = tiles[t]
        win0, var = windows[t]
        lo = c * MXU_DIM
        q4 = qkv_ref[idx + (pl.ds(r0, TQ), slice(lo, lo + MXU_DIM))]
        k4 = qkv_ref[idx + (pl.ds(win0, tk), slice(dq + lo, dq + lo + MXU_DIM))]
        v4 = qkv_ref[idx + (pl.ds(win0, tk), slice(2 * dq + lo, 2 * dq + lo + MXU_DIM))]
        q_stack = _stack_heads([q4] * HEADS_PER_TILE, range(HEADS_PER_TILE))
        return _scores(q_stack, k4, bias_ref[var, c]), _values_t(v4)

    def finish(stage, started):
        t, c = stage
        pv, m, den = _softmax_pv(*started)
        rden = 1.0 / den
        lse = m + jnp.log2(den)
        outs, lse16 = acc.get(t, ([], jnp.zeros((N_HEADS, TQ), F32)))
        for j in range(HEADS_PER_TILE):
            cols = slice(j * TQ, (j + 1) * TQ)
            outs.append((_head_block(pv, j, j) * rden[:, cols]).astype(BF16))
            lse16 = jnp.where(head_row == HEADS_PER_TILE * c + j, lse[:, cols], lse16)
        acc[t] = (outs, lse16)
        if c == N_TILES - 1:
            pad = jnp.zeros((LANES - LSE_PARTS * N_HEADS, TQ), BF16)
            tile_t = jnp.concatenate(outs + _split_f32(lse16) + [pad], axis=0)
            stage_ref[pl.ds(tiles[t][3], TQ), :] = tile_t.T

    _pipeline([(t, c) for t in range(len(tiles)) for c in range(N_TILES)], start, finish)


def _attn_b_kernel(qkv_ref, o_ref, lse_ref, bias_ref, stage_ref, *, n_cls, seq_len, tk, window, unit,
                   offsets, dil):
    @pl.when((pl.program_id(0) == 0) & (pl.program_id(1) == 0))
    def _():
        _fill_bias(bias_ref, tk=tk, window=window, unit=unit, offsets=offsets)

    i = pl.program_id(1)
    dq = N_HEADS * HEAD_DIM
    kw = dict(tk=tk, window=window, seq_len=seq_len)

    def tile_group(g, carry):
        tiles = []
        for u in range(TILES_PER_GROUP):
            t = g * TILES_PER_GROUP + u
            s0 = pl.multiple_of(t * TQ, TQ)
            if dil == 1:
                r0 = pl.multiple_of((i * n_cls + t) * TQ, TQ)
                tiles.append(((), r0, r0, s0))
            else:
                q0 = (i * TQ) % seq_len
                tiles.append(((t,), pl.multiple_of(q0, TQ), q0, s0))
        _attn_b_tiles(qkv_ref, tiles, bias_ref, stage_ref, **kw)
        return carry

    if n_cls == TILES_PER_GROUP:
        tile_group(0, 0)
    else:
        lax.fori_loop(0, n_cls // TILES_PER_GROUP, tile_group, 0)

    if dil == 1:
        o_ref[...] = stage_ref[:, :dq]
        lse_ref[...] = stage_ref[:, dq:]
    else:
        per_cls = REGROUP // dil
        perm = _grid_transpose_perm(REGROUP, per_cls, dil)
        for k in range(n_cls * TQ // REGROUP):
            rows = [stage_ref[r * TQ + k * per_cls:r * TQ + (k + 1) * per_cls, :] for r in range(dil)]
            nat = _permute_rows(perm, jnp.concatenate(rows, axis=0))
            o_ref[k * REGROUP:(k + 1) * REGROUP, :] = nat[:, :dq]
            lse_ref[k * REGROUP:(k + 1) * REGROUP, :] = nat[:, dq:]


def _attention_b(qkv, *, dil, window):
    batch, _, seq_len, width = qkv.shape
    dq = N_HEADS * HEAD_DIM
    seq = seq_len * dil
    tk = min(TQ + 2 * window, seq_len)
    offsets = (0,) if tk == seq_len else (0, window, 2 * window)
    if dil == 1:
        n_cls, n_span = SPAN // TQ, seq // SPAN
        in_spec = pl.BlockSpec((None, None, seq_len, width), lambda b, i: (b, 0, 0, 0))
    else:
        n_cls = dil
        n_span = seq_len // TQ
        in_spec = pl.BlockSpec((None, dil, seq_len, width), lambda b, i: (b, 0, 0, 0))
    rows = n_cls * TQ
    kern = functools.partial(_attn_b_kernel, n_cls=n_cls, seq_len=seq_len, tk=tk, window=window,
                             unit=dil, offsets=offsets, dil=dil)
    return pl.pallas_call(
        kern,
        grid=(batch, n_span),
        in_specs=[in_spec],
        out_specs=[pl.BlockSpec((rows, dq), lambda b, i: (b * n_span + i, 0)),
                   pl.BlockSpec((rows, LANES), lambda b, i: (b * n_span + i, 0))],
        out_shape=[jax.ShapeDtypeStruct((batch * seq, dq), BF16),
                   jax.ShapeDtypeStruct((batch * seq, LANES), BF16)],
        scratch_shapes=[pltpu.VMEM((len(offsets), N_TILES, tk, HEADS_PER_TILE * TQ), F32),
                        pltpu.VMEM((rows, dq + LANES), BF16)],
        compiler_params=_params("arbitrary", "arbitrary"),
    )(qkv)


def _attn_a_kernel(sink_ref, q_ref, k_ref, v_ref, o_ref, bias_ref, *, n_tiles, seq_len, tk, window, offsets):
    @pl.when((pl.program_id(0) == 0) & (pl.program_id(1) == 0))
    def _():
        _fill_bias(bias_ref, tk=tk, window=window, unit=1, offsets=offsets)

    i = pl.program_id(1)
    qblk = lax.broadcasted_iota(jnp.int32, (1, HEADS_PER_TILE * TQ), 1) // TQ

    tiles = []
    for t in range(n_tiles):
        q0 = pl.multiple_of((i * n_tiles + t) * TQ, TQ)
        tiles.append(_window_start(q0, window, tk, seq_len))
    shared = {}
    blocks = {}

    def start(stage):
        t, c = stage
        win0, var = tiles[t]
        if c == 0:
            qs = [q_ref[t * TQ:(t + 1) * TQ, j * MXU_DIM:(j + 1) * MXU_DIM] for j in range(HEADS_PER_TILE)]
            shared[t] = (qs, k_ref[pl.ds(win0, tk), :], _values_t(v_ref[pl.ds(win0, tk), :]))
        qs, k4, v4t = shared[t]
        return _scores(_stack_heads(qs, [c] * HEADS_PER_TILE), k4, bias_ref[var, c]), v4t

    def finish(stage, started):
        t, c = stage
        pv, m, den = _softmax_pv(*started)
        sink = jnp.zeros_like(m)
        for j in range(HEADS_PER_TILE):
            sink = jnp.where(qblk == j, sink_ref[HEADS_PER_TILE * c + j] * LOG2E, sink)
        gate = 1.0 / (den + jnp.exp2(sink - m))
        for j in range(HEADS_PER_TILE):
            blocks[t, j, c] = (_head_block(pv, c, j) * gate[:, j * TQ:(j + 1) * TQ]).astype(o_ref.dtype)
        if c == KV_A - 1:
            tile_t = jnp.concatenate(
                [blocks.pop((t, j, cc)) for j in range(HEADS_PER_TILE) for cc in range(KV_A)], axis=0)
            o_ref[t * TQ:(t + 1) * TQ, :] = tile_t.T

    _pipeline([(t, c) for t in range(n_tiles) for c in range(KV_A)], start, finish)


def _attention_a(qkv, sink):
    b, s, _ = qkv.shape
    dq = N_HEADS * HEAD_DIM
    dkv = KV_A * HEAD_DIM
    n_tiles = SPAN // TQ
    tk = TQ + 2 * WINDOW_A
    offsets = (0, WINDOW_A, 2 * WINDOW_A)
    kern = functools.partial(_attn_a_kernel, n_tiles=n_tiles, seq_len=s, tk=tk, window=WINDOW_A,
                             offsets=offsets)
    return pl.pallas_call(
        kern,
        grid=(b, s // SPAN),
        in_specs=[pl.BlockSpec(memory_space=pltpu.SMEM),
                  pl.BlockSpec((None, SPAN, dq), lambda bi, i: (bi, i, 0)),
                  pl.BlockSpec((None, s, dkv), lambda bi, i: (bi, 0, dq // dkv)),
                  pl.BlockSpec((None, s, dkv), lambda bi, i: (bi, 0, dq // dkv + 1))],
        out_specs=pl.BlockSpec((None, SPAN, dq), lambda bi, i: (bi, i, 0)),
        out_shape=jax.ShapeDtypeStruct((b, s, dq), BF16),
        scratch_shapes=[pltpu.VMEM((len(offsets), KV_A, tk, HEADS_PER_TILE * TQ), F32)],
        compiler_params=_params("arbitrary", "arbitrary"),
    )(sink, qkv, qkv, qkv)


def _post_kernel(*refs, n_groups, final, ff_chunks):
    refs = list(refs)
    x_ref = refs.pop(0)
    o_refs = [refs.pop(0) for _ in range(n_groups)]
    lse_refs = [refs.pop(0) for _ in range(n_groups)] if n_groups > 1 else []
    wo_ref, gf_ref, wg_ref, wu_ref, wd_ref = refs[:5]
    refs = refs[5:]
    gfin_ref = refs.pop(0) if final else None
    out_ref = refs.pop(0)

    if n_groups > 1:
        src = lax.broadcasted_iota(jnp.int32, (LANES, LANES), 0)
        dst = lax.broadcasted_iota(jnp.int32, (LANES, LANES), 1)
        fold = jnp.where((src % N_HEADS == dst % N_HEADS) & (src < LSE_PARTS * N_HEADS), 1.0, 0.0).astype(BF16)
        src = lax.broadcasted_iota(jnp.int32, (LANES, D_MODEL), 0)
        dst = lax.broadcasted_iota(jnp.int32, (LANES, D_MODEL), 1)
        expand = jnp.where((src % N_HEADS == dst // HEAD_DIM) & (src < LSE_PARTS * N_HEADS), 1.0, 0.0)
        expand = expand.astype(BF16)
        lses = [jnp.dot(r[...], fold, preferred_element_type=F32) for r in lse_refs]
        mx = functools.reduce(jnp.maximum, lses)
        es = [jnp.exp2(l - mx) for l in lses]
        den = functools.reduce(lambda a, b: a + b, es)
        piece_of_lane = lax.broadcasted_iota(jnp.int32, es[0].shape, 1) // N_HEADS
        o = None
        for e, o_ref in zip(es, o_refs):
            packed = jnp.zeros(e.shape, BF16)
            for k, piece in enumerate(_split_f32(e / den)):
                packed = jnp.where(piece_of_lane == k, piece, packed)
            term = jnp.dot(packed, expand, preferred_element_type=F32) * o_ref[...].astype(F32)
            o = term if o is None else o + term
        o = o.astype(BF16)
    else:
        o = o_refs[0][...]

    x1 = x_ref[...] + jnp.dot(o, wo_ref[...], preferred_element_type=F32)
    h = _rms(x1, gf_ref[...]).astype(BF16)
    acc = x1
    lo = 0
    for fc in ff_chunks:
        gate = jnp.dot(h, wg_ref[:, lo:lo + fc], preferred_element_type=F32)
        up = jnp.dot(h, wu_ref[:, lo:lo + fc], preferred_element_type=F32)
        a = (gate / (1.0 + jnp.exp(-gate)) * up).astype(BF16)
        acc = acc + jnp.dot(a, wd_ref[lo:lo + fc, :], preferred_element_type=F32)
        lo += fc
    if final:
        acc = _rms(acc, gfin_ref[...])
    out_ref[...] = acc


def _ff_chunks(d_ff, n):
    tiles, rem = divmod(d_ff, MXU_DIM)
    assert rem == 0
    return tuple((tiles // n + (k < tiles % n)) * MXU_DIM for k in range(n))


def _layer_spec(shape, layer):
    return pl.BlockSpec((None,) + tuple(shape[1:]), lambda *_: (layer,) + (0,) * (len(shape) - 1),
                        pipeline_mode=pl.Buffered(1))


def _post(x2d, os, lses, wo, gf, wg, wu, wd, layer, gfin, tm, ff_chunks):
    t, d = x2d.shape
    ff_chunks = _ff_chunks(wg.shape[2], ff_chunks)
    final = gfin is not None
    row = lambda n: pl.BlockSpec((tm, n), lambda i: (i, 0))
    in_specs = [row(d)] * (1 + len(os)) + [row(LANES)] * len(lses)
    in_specs += [_const_spec((d, d))] + [_layer_spec(a.shape, layer) for a in (gf, wg, wu, wd)]
    args = [x2d, *os, *lses, wo, gf, wg, wu, wd]
    if final:
        in_specs.append(_const_spec((1, d)))
        args.append(gfin.reshape(1, d))
    return pl.pallas_call(
        functools.partial(_post_kernel, n_groups=len(os), final=final, ff_chunks=ff_chunks),
        grid=(t // tm,),
        in_specs=in_specs,
        out_specs=row(d),
        out_shape=jax.ShapeDtypeStruct((t, d), F32),
        compiler_params=_params("parallel"),
    )(*args)


def _q_column_scale(n_cols, is_query_col):
    cols = np.arange(n_cols)
    return jnp.asarray(np.where(is_query_col(cols), HEAD_DIM ** -0.5 * LOG2E, 1.0), F32)


def kernel(x, norm_mix, norm_ffn, w_qkv_a, w_out_a, sink_a, w_qkv_b, w_out_b,
           w_gate, w_up, w_down, norm_final):
    b, s, d = x.shape
    t = b * s
    dq = N_HEADS * HEAD_DIM
    ffn = (norm_ffn.reshape(-1, 1, d), w_gate.astype(BF16), w_up.astype(BF16), w_down.astype(BF16))

    wa = w_qkv_a[0] * _q_column_scale(w_qkv_a.shape[2], lambda c: c < dq)
    wq = wa[:, :dq].reshape(d, KV_A, HEADS_PER_TILE, HEAD_DIM).transpose(0, 2, 1, 3).reshape(d, dq)
    wa = jnp.concatenate([wq, wa[:, dq:]], axis=1).astype(BF16)
    wo_a = w_out_a[0].reshape(KV_A, HEADS_PER_TILE, HEAD_DIM, d).transpose(1, 0, 2, 3).reshape(dq, d)
    x2d = x.reshape(t, d)
    qkv = _norm_matmul(x2d, norm_mix[0], wa, tm=1024).reshape(b, s, -1)
    o_a = _attention_a(qkv, sink_a[0])
    x2d = _post(x2d, [o_a.reshape(t, d)], [], wo_a.astype(BF16), *ffn, 0, None, tm=512, ff_chunks=2)

    wb = (w_qkv_b[0] * _q_column_scale(w_qkv_b.shape[2], lambda c: c % (3 * dq) < dq)).astype(BF16)
    dils = tuple(dil for _, dil in DILATED_GROUPS)
    qkvs = _norm_matmul_dilated(x2d.reshape(b, s, d), norm_mix[1], wb, dils)
    os, lses = [], []
    for qkv_g, (win, dil) in zip(qkvs, DILATED_GROUPS):
        o_g, lse_g = _attention_b(qkv_g, dil=dil, window=win // (2 * dil))
        os.append(o_g)
        lses.append(lse_g)
    out = _post(x2d, os, lses, w_out_b[0].astype(BF16), *ffn, 1, norm_final, tm=512, ff_chunks=2)
    return out.reshape(b, s, d)
```

```python
import functools

import jax
import jax.numpy as jnp
import numpy as np
from jax import lax
from jax.experimental import pallas as pl
from jax.experimental.pallas import tpu as pltpu

D_MODEL = 1024
HEAD_DIM = 64
N_HEADS = 16
KV_A = 4
WINDOW_A = 128
DILATED_GROUPS = ((128, 1), (512, 4), (2048, 16))
RMS_EPS = 1e-6
NEG = -1e30
LANES = 128
MXU_DIM = 256
HEADS_PER_TILE = MXU_DIM // HEAD_DIM
N_TILES = N_HEADS // HEADS_PER_TILE
TQ = 128
TILES_PER_GROUP = 4
PIPELINE_DEPTH = 4
B_HEADS = 2
B_COLS = B_HEADS * HEAD_DIM
B_STAGES = N_HEADS // B_HEADS
SPAN = 512
REGROUP = MXU_DIM
LSE_PARTS = 3
DEN_ROWS = 16
LOG2E = 1.4426950408889634
VMEM_LIMIT = 56 * 1024 * 1024

F32 = jnp.float32
BF16 = jnp.bfloat16


def _rms(x, g):
    ms = jnp.mean(x * x, axis=-1, keepdims=True)
    return x * lax.rsqrt(ms + RMS_EPS) * g


def _alibi_slopes(n):
    return [2.0 ** (-8.0 * (i + 1) / n) for i in range(n)]


def _const_spec(shape):
    return pl.BlockSpec(shape, lambda *_: (0,) * len(shape), pipeline_mode=pl.Buffered(1))


def _params(*sem):
    return pltpu.CompilerParams(dimension_semantics=sem, vmem_limit_bytes=VMEM_LIMIT)


def _grid_transpose_perm(n, n_major, n_minor):
    row = lax.broadcasted_iota(jnp.int32, (n, n), 0)
    col = lax.broadcasted_iota(jnp.int32, (n, n), 1)
    src = (row % n_minor) * n_major + row // n_minor
    return jnp.where(col == src, 1.0, 0.0).astype(BF16)


def _permute_rows(perm, x):
    return jnp.dot(perm, x, preferred_element_type=F32).astype(BF16)


def _norm_matmul_kernel(x_ref, g_ref, w_ref, o_ref):
    h = _rms(x_ref[...], g_ref[...]).astype(BF16)
    o_ref[...] = jnp.dot(h, w_ref[...], preferred_element_type=F32).astype(o_ref.dtype)


def _norm_matmul(x2d, g, w, tm):
    t, d = x2d.shape
    n = w.shape[1]
    return pl.pallas_call(
        _norm_matmul_kernel,
        grid=(t // tm,),
        in_specs=[pl.BlockSpec((tm, d), lambda i: (i, 0)), _const_spec((1, d)), _const_spec((d, n))],
        out_specs=pl.BlockSpec((tm, n), lambda i: (i, 0)),
        out_shape=jax.ShapeDtypeStruct((t, n), BF16),
        compiler_params=_params("parallel"),
    )(x2d, g.reshape(1, d), w)


def _norm_matmul_dilated_kernel(x_ref, g_ref, w_ref, *o_refs, dils):
    h = _rms(x_ref[...], g_ref[...]).astype(BF16)
    n = o_refs[0].shape[-1]
    for gi, (dil, o_ref) in enumerate(zip(dils, o_refs)):
        hg = h if dil == 1 else _to_classes(h, dil)
        y = jnp.dot(hg, w_ref[:, gi * n:(gi + 1) * n], preferred_element_type=F32).astype(o_ref.dtype)
        o_ref[0] = y.reshape(o_ref.shape[1:])


def _to_classes(h, dil):
    per = REGROUP // dil
    perm = _grid_transpose_perm(REGROUP, dil, per)
    parts = [_permute_rows(perm, h[b * REGROUP:(b + 1) * REGROUP]) for b in range(SPAN // REGROUP)]
    return jnp.concatenate([p[r * per:(r + 1) * per] for r in range(dil) for p in parts], axis=0)


def _norm_matmul_dilated(x, g, w, dils):
    b, s, d = x.shape
    n = w.shape[1] // len(dils)
    out_specs = [pl.BlockSpec((1, dil, SPAN // dil, n), lambda bi, i: (bi, 0, i, 0)) for dil in dils]
    out_shape = [jax.ShapeDtypeStruct((b, dil, s // dil, n), BF16) for dil in dils]
    return pl.pallas_call(
        functools.partial(_norm_matmul_dilated_kernel, dils=dils),
        grid=(b, s // SPAN),
        in_specs=[pl.BlockSpec((None, SPAN, d), lambda bi, i: (bi, i, 0)),
                  _const_spec((1, d)), _const_spec(w.shape)],
        out_specs=out_specs,
        out_shape=out_shape,
        compiler_params=_params("parallel", "parallel"),
    )(x, g.reshape(1, d), w)


def _lane_group(shape):
    return lax.broadcasted_iota(jnp.int32, shape, 1) // HEAD_DIM


def _fill_bias(bias_ref, heads, *, tk, window, unit, offsets):
    slopes = _alibi_slopes(N_HEADS)
    key = lax.broadcasted_iota(jnp.int32, (tk, TQ), 0)
    qry = lax.broadcasted_iota(jnp.int32, (tk, TQ), 1)
    for v, off in enumerate(offsets):
        dist = jnp.abs(key - qry - off)
        negd = jnp.where(dist <= window, -(dist.astype(F32) * float(unit)), NEG)
        for c in range(N_HEADS // heads):
            for j in range(heads):
                bias_ref[v, c, :, j * TQ:(j + 1) * TQ] = (slopes[heads * c + j] * LOG2E) * negd


def _stack_heads(tiles, groups):
    lg = _lane_group(tiles[0].shape)
    return jnp.concatenate(
        [jnp.where(lg == g, t, jnp.zeros_like(t)) for t, g in zip(tiles, groups)], axis=0)


def _scores(q_stack, k, bias):
    return lax.dot_general(k, q_stack, (((1,), (1,)), ((), ())), preferred_element_type=F32) + bias


def _with_ones(vt):
    return jnp.concatenate([vt, jnp.ones((DEN_ROWS, vt.shape[1]), vt.dtype)], axis=0)


def _softmax_pv(sc, vt1):
    m = jnp.max(sc, axis=0, keepdims=True)
    p = jnp.exp2(sc - m)
    pv = jnp.dot(vt1, p.astype(BF16), preferred_element_type=F32)
    rows = vt1.shape[0] - DEN_ROWS
    return pv[:rows], m, pv[rows:rows + 1]


def _pipeline(stages, start_fn, finish_fn):
    started = [start_fn(st) for st in stages[:PIPELINE_DEPTH]]
    for k, st in enumerate(stages):
        cur = started.pop(0)
        if k + PIPELINE_DEPTH < len(stages):
            started.append(start_fn(stages[k + PIPELINE_DEPTH]))
        finish_fn(st, cur)


def _window_start(q0, window, tk, seq_len):
    start = jnp.clip(q0 - window, 0, seq_len - tk)
    return pl.multiple_of(start, min(window, TQ)), (q0 - start) // window


def _split_f32(x):
    parts = []
    for _ in range(LSE_PARTS):
        p = x.astype(BF16)
        parts.append(p)
        x = x - p.astype(F32)
    return parts


def _attn_b_tiles(qkv_ref, tiles, bias_ref, stage_ref, *, tk, window, seq_len):
    dq = N_HEADS * HEAD_DIM
    head_row = lax.broadcasted_iota(jnp.int32, (N_HEADS, TQ), 0)
    windows = [_window_start(q0, window, tk, seq_len) for _, _, q0, _ in tiles]
    acc = {}

    def start(stage):
        t, c = stage
        idx, r0, _, _ = tiles[t]
        win0, var = windows[t]
        lo = c * B_COLS
        q = qkv_ref[idx + (pl.ds(r0, TQ), slice(lo, lo + B_COLS))]
        k = qkv_ref[idx + (pl.ds(win0, tk), slice(dq + lo, dq + lo + B_COLS))]
        v = qkv_ref[idx + (pl.ds(win0, tk), slice(2 * dq + lo, 2 * dq + lo + B_COLS))]
        q_stack = _stack_heads([q] * B_HEADS, range(B_HEADS))
        return _scores(q_stack, k, bias_ref[var, c]), _with_ones(v.T)

    def finish(stage, started):
        t, c = stage
        pv, m, den = _softmax_pv(*started)
        rden = 1.0 / den
        lse = m + jnp.log2(den)
        outs, lse16 = acc.get(t, ([], jnp.zeros((N_HEADS, TQ), F32)))
        for j in range(B_HEADS):
            cols = slice(j * TQ, (j + 1) * TQ)
            outs.append((pv[j * HEAD_DIM:(j + 1) * HEAD_DIM, cols] * rden[:, cols]).astype(BF16))
            lse16 = jnp.where(head_row == B_HEADS * c + j, lse[:, cols], lse16)
        acc[t] = (outs, lse16)
        if c == B_STAGES - 1:
            pad = jnp.zeros((LANES - LSE_PARTS * N_HEADS, TQ), BF16)
            tile_t = jnp.concatenate(outs + _split_f32(lse16) + [pad], axis=0)
            stage_ref[pl.ds(tiles[t][3], TQ), :] = tile_t.T

    _pipeline([(t, c) for t in range(len(tiles)) for c in range(B_STAGES)], start, finish)


def _attn_b_kernel(qkv_ref, o_ref, lse_ref, bias_ref, stage_ref, *, n_cls, seq_len, tk, window, unit,
                   offsets, dil):
    @pl.when((pl.program_id(0) == 0) & (pl.program_id(1) == 0))
    def _():
        _fill_bias(bias_ref, B_HEADS, tk=tk, window=window, unit=unit, offsets=offsets)

    i = pl.program_id(1)
    dq = N_HEADS * HEAD_DIM
    kw = dict(tk=tk, window=window, seq_len=seq_len)

    def tile_group(g, carry):
        tiles = []
        for u in range(TILES_PER_GROUP):
            t = g * TILES_PER_GROUP + u
            s0 = pl.multiple_of(t * TQ, TQ)
            if dil == 1:
                r0 = pl.multiple_of((i * n_cls + t) * TQ, TQ)
                tiles.append(((), r0, r0, s0))
            else:
                q0 = (i * TQ) % seq_len
                tiles.append(((t,), pl.multiple_of(q0, TQ), q0, s0))
        _attn_b_tiles(qkv_ref, tiles, bias_ref, stage_ref, **kw)
        return carry

    if n_cls == TILES_PER_GROUP:
        tile_group(0, 0)
    else:
        lax.fori_loop(0, n_cls // TILES_PER_GROUP, tile_group, 0)

    if dil == 1:
        o_ref[...] = stage_ref[:, :dq]
        lse_ref[...] = stage_ref[:, dq:]
    else:
        per_cls = REGROUP // dil
        perm = _grid_transpose_perm(REGROUP, per_cls, dil)
        for k in range(n_cls * TQ // REGROUP):
            rows = [stage_ref[r * TQ + k * per_cls:r * TQ + (k + 1) * per_cls, :] for r in range(dil)]
            nat = _permute_rows(perm, jnp.concatenate(rows, axis=0))
            o_ref[k * REGROUP:(k + 1) * REGROUP, :] = nat[:, :dq]
            lse_ref[k * REGROUP:(k + 1) * REGROUP, :] = nat[:, dq:]


def _attention_b(qkv, *, dil, window):
    batch, _, seq_len, width = qkv.shape
    dq = N_HEADS * HEAD_DIM
    seq = seq_len * dil
    tk = min(TQ + 2 * window, seq_len)
    offsets = (0,) if tk == seq_len else (0, window, 2 * window)
    if dil == 1:
        n_cls, n_span = SPAN // TQ, seq // SPAN
        in_spec = pl.BlockSpec((None, None, seq_len, width), lambda b, i: (b, 0, 0, 0))
    else:
        n_cls = dil
        n_span = seq_len // TQ
        in_spec = pl.BlockSpec((None, dil, seq_len, width), lambda b, i: (b, 0, 0, 0))
    rows = n_cls * TQ
    kern = functools.partial(_attn_b_kernel, n_cls=n_cls, seq_len=seq_len, tk=tk, window=window,
                             unit=dil, offsets=offsets, dil=dil)
    return pl.pallas_call(
        kern,
        grid=(batch, n_span),
        in_specs=[in_spec],
        out_specs=[pl.BlockSpec((rows, dq), lambda b, i: (b * n_span + i, 0)),
                   pl.BlockSpec((rows, LANES), lambda b, i: (b * n_span + i, 0))],
        out_shape=[jax.ShapeDtypeStruct((batch * seq, dq), BF16),
                   jax.ShapeDtypeStruct((batch * seq, LANES), BF16)],
        scratch_shapes=[pltpu.VMEM((len(offsets), B_STAGES, tk, B_HEADS * TQ), F32),
                        pltpu.VMEM((rows, dq + LANES), BF16)],
        compiler_params=_params("arbitrary", "arbitrary"),
    )(qkv)


def _attn_a_kernel(sink_ref, q_ref, k_ref, v_ref, o_ref, bias_ref, *, n_tiles, seq_len, tk, window, offsets):
    @pl.when((pl.program_id(0) == 0) & (pl.program_id(1) == 0))
    def _():
        _fill_bias(bias_ref, HEADS_PER_TILE, tk=tk, window=window, unit=1, offsets=offsets)

    i = pl.program_id(1)
    qblk = lax.broadcasted_iota(jnp.int32, (1, HEADS_PER_TILE * TQ), 1) // TQ

    tiles = []
    for t in range(n_tiles):
        q0 = pl.multiple_of((i * n_tiles + t) * TQ, TQ)
        tiles.append(_window_start(q0, window, tk, seq_len))
    shared = {}
    blocks = {}

    def start(stage):
        t, c = stage
        win0, var = tiles[t]
        if c == 0:
            qs = [q_ref[t * TQ:(t + 1) * TQ, j * MXU_DIM:(j + 1) * MXU_DIM] for j in range(HEADS_PER_TILE)]
            shared[t] = (qs, k_ref[pl.ds(win0, tk), :], v_ref[pl.ds(win0, tk), :].T)
        qs, k4, vt = shared[t]
        vt_c = _with_ones(vt[c * HEAD_DIM:(c + 1) * HEAD_DIM])
        return _scores(_stack_heads(qs, [c] * HEADS_PER_TILE), k4, bias_ref[var, c]), vt_c

    def finish(stage, started):
        t, c = stage
        pv, m, den = _softmax_pv(*started)
        sink = jnp.zeros_like(m)
        for j in range(HEADS_PER_TILE):
            sink = jnp.where(qblk == j, sink_ref[HEADS_PER_TILE * c + j] * LOG2E, sink)
        gate = 1.0 / (den + jnp.exp2(sink - m))
        for j in range(HEADS_PER_TILE):
            cols = slice(j * TQ, (j + 1) * TQ)
            blocks[t, j, c] = (pv[:, cols] * gate[:, cols]).astype(o_ref.dtype)
        if c == KV_A - 1:
            tile_t = jnp.concatenate(
                [blocks.pop((t, j, cc)) for j in range(HEADS_PER_TILE) for cc in range(KV_A)], axis=0)
            o_ref[t * TQ:(t + 1) * TQ, :] = tile_t.T

    _pipeline([(t, c) for t in range(n_tiles) for c in range(KV_A)], start, finish)


def _attention_a(qkv, sink):
    b, s, _ = qkv.shape
    dq = N_HEADS * HEAD_DIM
    dkv = KV_A * HEAD_DIM
    n_tiles = SPAN // TQ
    tk = TQ + 2 * WINDOW_A
    offsets = (0, WINDOW_A, 2 * WINDOW_A)
    kern = functools.partial(_attn_a_kernel, n_tiles=n_tiles, seq_len=s, tk=tk, window=WINDOW_A,
                             offsets=offsets)
    return pl.pallas_call(
        kern,
        grid=(b, s // SPAN),
        in_specs=[pl.BlockSpec(memory_space=pltpu.SMEM),
                  pl.BlockSpec((None, SPAN, dq), lambda bi, i: (bi, i, 0)),
                  pl.BlockSpec((None, s, dkv), lambda bi, i: (bi, 0, dq // dkv)),
                  pl.BlockSpec((None, s, dkv), lambda bi, i: (bi, 0, dq // dkv + 1))],
        out_specs=pl.BlockSpec((None, SPAN, dq), lambda bi, i: (bi, i, 0)),
        out_shape=jax.ShapeDtypeStruct((b, s, dq), BF16),
        scratch_shapes=[pltpu.VMEM((len(offsets), KV_A, tk, HEADS_PER_TILE * TQ), F32)],
        compiler_params=_params("arbitrary", "arbitrary"),
    )(sink, qkv, qkv, qkv)


def _post_kernel(*refs, n_groups, final, ff_chunks):
    refs = list(refs)
    x_ref = refs.pop(0)
    o_refs = [refs.pop(0) for _ in range(n_groups)]
    lse_refs = [refs.pop(0) for _ in range(n_groups)] if n_groups > 1 else []
    wo_ref, gf_ref, wg_ref, wu_ref, wd_ref = refs[:5]
    refs = refs[5:]
    gfin_ref = refs.pop(0) if final else None
    out_ref = refs.pop(0)

    if n_groups > 1:
        src = lax.broadcasted_iota(jnp.int32, (LANES, LANES), 0)
        dst = lax.broadcasted_iota(jnp.int32, (LANES, LANES), 1)
        fold = jnp.where((src % N_HEADS == dst % N_HEADS) & (src < LSE_PARTS * N_HEADS), 1.0, 0.0).astype(BF16)
        src = lax.broadcasted_iota(jnp.int32, (LANES, D_MODEL), 0)
        dst = lax.broadcasted_iota(jnp.int32, (LANES, D_MODEL), 1)
        expand = jnp.where((src % N_HEADS == dst // HEAD_DIM) & (src < LSE_PARTS * N_HEADS), 1.0, 0.0)
        expand = expand.astype(BF16)
        lses = [jnp.dot(r[...], fold, preferred_element_type=F32) for r in lse_refs]
        mx = functools.reduce(jnp.maximum, lses)
        es = [jnp.exp2(l - mx) for l in lses]
        den = functools.reduce(lambda a, b: a + b, es)
        piece_of_lane = lax.broadcasted_iota(jnp.int32, es[0].shape, 1) // N_HEADS
        o = None
        for e, o_ref in zip(es, o_refs):
            packed = jnp.zeros(e.shape, BF16)
            for k, piece in enumerate(_split_f32(e / den)):
                packed = jnp.where(piece_of_lane == k, piece, packed)
            term = jnp.dot(packed, expand, preferred_element_type=F32) * o_ref[...].astype(F32)
            o = term if o is None else o + term
        o = o.astype(BF16)
    else:
        o = o_refs[0][...]

    x1 = x_ref[...] + jnp.dot(o, wo_ref[...], preferred_element_type=F32)
    h = _rms(x1, gf_ref[...]).astype(BF16)
    acc = x1
    lo = 0
    for fc in ff_chunks:
        gate = jnp.dot(h, wg_ref[:, lo:lo + fc], preferred_element_type=F32)
        up = jnp.dot(h, wu_ref[:, lo:lo + fc], preferred_element_type=F32)
        a = (gate / (1.0 + jnp.exp(-gate)) * up).astype(BF16)
        acc = acc + jnp.dot(a, wd_ref[lo:lo + fc, :], preferred_element_type=F32)
        lo += fc
    if final:
        acc = _rms(acc, gfin_ref[...])
    out_ref[...] = acc


def _ff_chunks(d_ff, n):
    tiles, rem = divmod(d_ff, MXU_DIM)
    assert rem == 0
    return tuple((tiles // n + (k < tiles % n)) * MXU_DIM for k in range(n))


def _layer_spec(shape, layer):
    return pl.BlockSpec((None,) + tuple(shape[1:]), lambda *_: (layer,) + (0,) * (len(shape) - 1),
                        pipeline_mode=pl.Buffered(1))


def _post(x2d, os, lses, wo, gf, wg, wu, wd, layer, gfin, tm, ff_chunks):
    t, d = x2d.shape
    ff_chunks = _ff_chunks(wg.shape[2], ff_chunks)
    final = gfin is not None
    row = lambda n: pl.BlockSpec((tm, n), lambda i: (i, 0))
    in_specs = [row(d)] * (1 + len(os)) + [row(LANES)] * len(lses)
    in_specs += [_const_spec((d, d))] + [_layer_spec(a.shape, layer) for a in (gf, wg, wu, wd)]
    args = [x2d, *os, *lses, wo, gf, wg, wu, wd]
    if final:
        in_specs.append(_const_spec((1, d)))
        args.append(gfin.reshape(1, d))
    return pl.pallas_call(
        functools.partial(_post_kernel, n_groups=len(os), final=final, ff_chunks=ff_chunks),
        grid=(t // tm,),
        in_specs=in_specs,
        out_specs=row(d),
        out_shape=jax.ShapeDtypeStruct((t, d), F32),
        compiler_params=_params("parallel"),
    )(*args)


def _q_column_scale(n_cols, is_query_col):
    cols = np.arange(n_cols)
    return jnp.asarray(np.where(is_query_col(cols), HEAD_DIM ** -0.5 * LOG2E, 1.0), F32)


def kernel(x, norm_mix, norm_ffn, w_qkv_a, w_out_a, sink_a, w_qkv_b, w_out_b,
           w_gate, w_up, w_down, norm_final):
    b, s, d = x.shape
    t = b * s
    dq = N_HEADS * HEAD_DIM
    ffn = (norm_ffn.reshape(-1, 1, d), w_gate.astype(BF16), w_up.astype(BF16), w_down.astype(BF16))

    wa = w_qkv_a[0] * _q_column_scale(w_qkv_a.shape[2], lambda c: c < dq)
    wq = wa[:, :dq].reshape(d, KV_A, HEADS_PER_TILE, HEAD_DIM).transpose(0, 2, 1, 3).reshape(d, dq)
    wa = jnp.concatenate([wq, wa[:, dq:]], axis=1).astype(BF16)
    wo_a = w_out_a[0].reshape(KV_A, HEADS_PER_TILE, HEAD_DIM, d).transpose(1, 0, 2, 3).reshape(dq, d)
    x2d = x.reshape(t, d)
    qkv = _norm_matmul(x2d, norm_mix[0], wa, tm=1024).reshape(b, s, -1)
    o_a = _attention_a(qkv, sink_a[0])
    x2d = _post(x2d, [o_a.reshape(t, d)], [], wo_a.astype(BF16), *ffn, 0, None, tm=512, ff_chunks=2)

    wb = (w_qkv_b[0] * _q_column_scale(w_qkv_b.shape[2], lambda c: c % (3 * dq) < dq)).astype(BF16)
    dils = tuple(dil for _, dil in DILATED_GROUPS)
    qkvs = _norm_matmul_dilated(x2d.reshape(b, s, d), norm_mix[1], wb, dils)
    os, lses = [], []
    for qkv_g, (win, dil) in zip(qkvs, DILATED_GROUPS):
        o_g, lse_g = _attention_b(qkv_g, dil=dil, window=win // (2 * dil))
        os.append(o_g)
        lses.append(lse_g)
    out = _post(x2d, os, lses, w_out_b[0].astype(BF16), *ffn, 1, norm_final, tm=512, ff_chunks=2)
    return out.reshape(b, s, d)
```

```python
import functools

import jax
import jax.numpy as jnp
import numpy as np
from jax import lax
from jax.experimental import pallas as pl
from jax.experimental.pallas import tpu as pltpu

D_MODEL = 1024
HEAD_DIM = 64
N_HEADS = 16
KV_A = 4
WINDOW_A = 128
DILATED_GROUPS = ((128, 1), (512, 4), (2048, 16))
RMS_EPS = 1e-6
NEG = -1e30
LANES = 128
MXU_DIM = 256
HEADS_PER_TILE = MXU_DIM // HEAD_DIM
N_TILES = N_HEADS // HEADS_PER_TILE
TQ = 128
TILES_PER_GROUP = 4
PIPELINE_DEPTH = 4
B_HEADS = 2
B_COLS = B_HEADS * HEAD_DIM
B_STAGES = N_HEADS // B_HEADS
SPAN = 512
REGROUP = MXU_DIM
LSE_PARTS = 3
DEN_ROWS = 16
LOG2E = 1.4426950408889634
VMEM_LIMIT = 56 * 1024 * 1024

F32 = jnp.float32
BF16 = jnp.bfloat16


def _rms(x, g):
    ms = jnp.mean(x * x, axis=-1, keepdims=True)
    return x * lax.rsqrt(ms + RMS_EPS) * g


def _alibi_slopes(n):
    return [2.0 ** (-8.0 * (i + 1) / n) for i in range(n)]


def _const_spec(shape):
    return pl.BlockSpec(shape, lambda *_: (0,) * len(shape), pipeline_mode=pl.Buffered(1))


def _params(*sem):
    return pltpu.CompilerParams(dimension_semantics=sem, vmem_limit_bytes=VMEM_LIMIT)


def _grid_transpose_perm(n, n_major, n_minor):
    row = lax.broadcasted_iota(jnp.int32, (n, n), 0)
    col = lax.broadcasted_iota(jnp.int32, (n, n), 1)
    src = (row % n_minor) * n_major + row // n_minor
    return jnp.where(col == src, 1.0, 0.0).astype(BF16)


def _permute_rows(perm, x):
    return jnp.dot(perm, x, preferred_element_type=F32).astype(BF16)


def _norm_matmul_kernel(x_ref, g_ref, w_ref, o_ref):
    h = _rms(x_ref[...], g_ref[...]).astype(BF16)
    o_ref[...] = jnp.dot(h, w_ref[...], preferred_element_type=F32).astype(o_ref.dtype)


def _norm_matmul(x2d, g, w, tm):
    t, d = x2d.shape
    n = w.shape[1]
    return pl.pallas_call(
        _norm_matmul_kernel,
        grid=(t // tm,),
        in_specs=[pl.BlockSpec((tm, d), lambda i: (i, 0)), _const_spec((1, d)), _const_spec((d, n))],
        out_specs=pl.BlockSpec((tm, n), lambda i: (i, 0)),
        out_shape=jax.ShapeDtypeStruct((t, n), BF16),
        compiler_params=_params("parallel"),
    )(x2d, g.reshape(1, d), w)


def _norm_matmul_dilated_kernel(x_ref, g_ref, w_ref, *o_refs, dils):
    h = _rms(x_ref[...], g_ref[...]).astype(BF16)
    n = o_refs[0].shape[-1]
    for gi, (dil, o_ref) in enumerate(zip(dils, o_refs)):
        hg = h if dil == 1 else _to_classes(h, dil)
        y = jnp.dot(hg, w_ref[:, gi * n:(gi + 1) * n], preferred_element_type=F32).astype(o_ref.dtype)
        o_ref[0] = y.reshape(o_ref.shape[1:])


def _to_classes(h, dil):
    per = REGROUP // dil
    perm = _grid_transpose_perm(REGROUP, dil, per)
    parts = [_permute_rows(perm, h[b * REGROUP:(b + 1) * REGROUP]) for b in range(SPAN // REGROUP)]
    return jnp.concatenate([p[r * per:(r + 1) * per] for r in range(dil) for p in parts], axis=0)


def _norm_matmul_dilated(x, g, w, dils):
    b, s, d = x.shape
    n = w.shape[1] // len(dils)
    out_specs = [pl.BlockSpec((1, dil, SPAN // dil, n), lambda bi, i: (bi, 0, i, 0)) for dil in dils]
    out_shape = [jax.ShapeDtypeStruct((b, dil, s // dil, n), BF16) for dil in dils]
    return pl.pallas_call(
        functools.partial(_norm_matmul_dilated_kernel, dils=dils),
        grid=(b, s // SPAN),
        in_specs=[pl.BlockSpec((None, SPAN, d), lambda bi, i: (bi, i, 0)),
                  _const_spec((1, d)), _const_spec(w.shape)],
        out_specs=out_specs,
        out_shape=out_shape,
        compiler_params=_params("parallel", "parallel"),
    )(x, g.reshape(1, d), w)


def _lane_group(shape):
    return lax.broadcasted_iota(jnp.int32, shape, 1) // HEAD_DIM


def _fill_bias(bias_ref, heads, *, tk, window, unit, offsets):
    slopes = _alibi_slopes(N_HEADS)
    key = lax.broadcasted_iota(jnp.int32, (tk, TQ), 0)
    qry = lax.broadcasted_iota(jnp.int32, (tk, TQ), 1)
    for v, off in enumerate(offsets):
        dist = jnp.abs(key - qry - off)
        negd = jnp.where(dist <= window, -(dist.astype(F32) * float(unit)), NEG)
        for c in range(N_HEADS // heads):
            for j in range(heads):
                bias_ref[v, c, :, j * TQ:(j + 1) * TQ] = (slopes[heads * c + j] * LOG2E) * negd


def _stack_heads(tiles, groups):
    lg = _lane_group(tiles[0].shape)
    return jnp.concatenate(
        [jnp.where(lg == g, t, jnp.zeros_like(t)) for t, g in zip(tiles, groups)], axis=0)


def _scores(q_stack, k, bias):
    return lax.dot_general(k, q_stack, (((1,), (1,)), ((), ())), preferred_element_type=F32) + bias


def _with_ones(vt):
    return jnp.concatenate([vt, jnp.ones((DEN_ROWS, vt.shape[1]), vt.dtype)], axis=0)


def _softmax_pv(sc, vt1):
    m = jnp.max(sc, axis=0, keepdims=True)
    p = jnp.exp2(sc - m)
    pv = jnp.dot(vt1, p.astype(BF16), preferred_element_type=F32)
    rows = vt1.shape[0] - DEN_ROWS
    return pv[:rows], m, pv[rows:rows + 1]


def _pipeline(stages, start_fn, finish_fn):
    started = [start_fn(st) for st in stages[:PIPELINE_DEPTH]]
    for k, st in enumerate(stages):
        cur = started.pop(0)
        if k + PIPELINE_DEPTH < len(stages):
            started.append(start_fn(stages[k + PIPELINE_DEPTH]))
        finish_fn(st, cur)


def _window_start(q0, window, tk, seq_len):
    start = jnp.clip(q0 - window, 0, seq_len - tk)
    return pl.multiple_of(start, min(window, TQ)), (q0 - start) // window


def _split_f32(x):
    parts = []
    for _ in range(LSE_PARTS):
        p = x.astype(BF16)
        parts.append(p)
        x = x - p.astype(F32)
    return parts


def _attn_b_tiles(qkv_ref, tiles, bias_ref, stage_ref, *, tk, window, seq_len):
    dq = N_HEADS * HEAD_DIM
    head_row = lax.broadcasted_iota(jnp.int32, (N_HEADS, TQ), 0)
    windows = [_window_start(q0, window, tk, seq_len) for _, _, q0, _ in tiles]
    acc = {}

    def start(stage):
        t, c = stage
        idx, r0, _, _ = tiles[t]
        win0, var = windows[t]
        lo = c * B_COLS
        q = qkv_ref[idx + (pl.ds(r0, TQ), slice(lo, lo + B_COLS))]
        k = qkv_ref[idx + (pl.ds(win0, tk), slice(dq + lo, dq + lo + B_COLS))]
        v = qkv_ref[idx + (pl.ds(win0, tk), slice(2 * dq + lo, 2 * dq + lo + B_COLS))]
        q_stack = _stack_heads([q] * B_HEADS, range(B_HEADS))
        return _scores(q_stack, k, bias_ref[var, c]), _with_ones(v.T)

    def finish(stage, started):
        t, c = stage
        pv, m, den = _softmax_pv(*started)
        rden = 1.0 / den
        lse = m + jnp.log2(den)
        outs, lse16 = acc.get(t, ([], jnp.zeros((N_HEADS, TQ), F32)))
        for j in range(B_HEADS):
            cols = slice(j * TQ, (j + 1) * TQ)
            outs.append((pv[j * HEAD_DIM:(j + 1) * HEAD_DIM, cols] * rden[:, cols]).astype(BF16))
            lse16 = jnp.where(head_row == B_HEADS * c + j, lse[:, cols], lse16)
        acc[t] = (outs, lse16)
        if c == B_STAGES - 1:
            pad = jnp.zeros((LANES - LSE_PARTS * N_HEADS, TQ), BF16)
            tile_t = jnp.concatenate(outs + _split_f32(lse16) + [pad], axis=0)
            stage_ref[pl.ds(tiles[t][3], TQ), :] = tile_t.T

    _pipeline([(t, c) for t in range(len(tiles)) for c in range(B_STAGES)], start, finish)


def _attn_b_kernel(qkv_ref, o_ref, lse_ref, bias_ref, stage_ref, *, seq_len, tk, window, unit, offsets, dil):
    @pl.when(pl.program_id(0) == 0)
    def _():
        _fill_bias(bias_ref, B_HEADS, tk=tk, window=window, unit=unit, offsets=offsets)

    dq = N_HEADS * HEAD_DIM
    n_tiles = dil * seq_len // TQ

    def tile_group(g, carry):
        tiles = []
        for u in range(TILES_PER_GROUP):
            t = g * TILES_PER_GROUP + u
            s0 = pl.multiple_of(t * TQ, TQ)
            if dil == 1:
                tiles.append(((), s0, s0, s0))
            else:
                q0 = pl.multiple_of((t // dil) * TQ, TQ)
                tiles.append(((t % dil,), q0, q0, s0))
        _attn_b_tiles(qkv_ref, tiles, bias_ref, stage_ref, tk=tk, window=window, seq_len=seq_len)
        return carry

    lax.fori_loop(0, n_tiles // TILES_PER_GROUP, tile_group, 0)

    if dil == 1:
        o_ref[...] = stage_ref[:, :dq]
        lse_ref[...] = stage_ref[:, dq:]
    else:
        per_cls = REGROUP // dil
        perm = _grid_transpose_perm(REGROUP, per_cls, dil)
        for k in range(n_tiles * TQ // REGROUP):
            span, off = divmod(k * per_cls, TQ)
            rows = [stage_ref[(span * dil + r) * TQ + off:(span * dil + r) * TQ + off + per_cls, :]
                    for r in range(dil)]
            nat = _permute_rows(perm, jnp.concatenate(rows, axis=0))
            o_ref[k * REGROUP:(k + 1) * REGROUP, :] = nat[:, :dq]
            lse_ref[k * REGROUP:(k + 1) * REGROUP, :] = nat[:, dq:]


def _attention_b(qkv, *, dil, window):
    batch, _, seq_len, width = qkv.shape
    dq = N_HEADS * HEAD_DIM
    seq = seq_len * dil
    tk = min(TQ + 2 * window, seq_len)
    offsets = (0,) if tk == seq_len else (0, window, 2 * window)
    if dil == 1:
        in_spec = pl.BlockSpec((None, None, seq_len, width), lambda b: (b, 0, 0, 0))
    else:
        in_spec = pl.BlockSpec((None, dil, seq_len, width), lambda b: (b, 0, 0, 0))
    kern = functools.partial(_attn_b_kernel, seq_len=seq_len, tk=tk, window=window, unit=dil,
                             offsets=offsets, dil=dil)
    return pl.pallas_call(
        kern,
        grid=(batch,),
        in_specs=[in_spec],
        out_specs=[pl.BlockSpec((seq, dq), lambda b: (b, 0)), pl.BlockSpec((seq, LANES), lambda b: (b, 0))],
        out_shape=[jax.ShapeDtypeStruct((batch * seq, dq), BF16),
                   jax.ShapeDtypeStruct((batch * seq, LANES), BF16)],
        scratch_shapes=[pltpu.VMEM((len(offsets), B_STAGES, tk, B_HEADS * TQ), F32),
                        pltpu.VMEM((seq, dq + LANES), BF16)],
        compiler_params=_params("arbitrary"),
    )(qkv)


def _attn_a_kernel(sink_ref, q_ref, k_ref, v_ref, o_ref, bias_ref, *, seq_len, tk, window, offsets):
    @pl.when(pl.program_id(0) == 0)
    def _():
        _fill_bias(bias_ref, HEADS_PER_TILE, tk=tk, window=window, unit=1, offsets=offsets)

    lax.fori_loop(0, seq_len // (TILES_PER_GROUP * TQ),
                  functools.partial(_attn_a_tile_group, sink_ref, q_ref, k_ref, v_ref, o_ref, bias_ref,
                                    seq_len=seq_len, tk=tk, window=window), 0)


def _attn_a_tile_group(sink_ref, q_ref, k_ref, v_ref, o_ref, bias_ref, g, carry, *, seq_len, tk, window):
    qblk = lax.broadcasted_iota(jnp.int32, (1, HEADS_PER_TILE * TQ), 1) // TQ
    rows = [pl.multiple_of((g * TILES_PER_GROUP + t) * TQ, TQ) for t in range(TILES_PER_GROUP)]
    tiles = [_window_start(q0, window, tk, seq_len) for q0 in rows]
    shared = {}
    blocks = {}

    def start(stage):
        t, c = stage
        win0, var = tiles[t]
        if c == 0:
            qs = [q_ref[pl.ds(rows[t], TQ), j * MXU_DIM:(j + 1) * MXU_DIM] for j in range(HEADS_PER_TILE)]
            shared[t] = (qs, k_ref[pl.ds(win0, tk), :], v_ref[pl.ds(win0, tk), :].T)
        qs, k4, vt = shared[t]
        vt_c = _with_ones(vt[c * HEAD_DIM:(c + 1) * HEAD_DIM])
        return _scores(_stack_heads(qs, [c] * HEADS_PER_TILE), k4, bias_ref[var, c]), vt_c

    def finish(stage, started):
        t, c = stage
        pv, m, den = _softmax_pv(*started)
        sink = jnp.zeros_like(m)
        for j in range(HEADS_PER_TILE):
            sink = jnp.where(qblk == j, sink_ref[HEADS_PER_TILE * c + j] * LOG2E, sink)
        gate = 1.0 / (den + jnp.exp2(sink - m))
        for j in range(HEADS_PER_TILE):
            cols = slice(j * TQ, (j + 1) * TQ)
            blocks[t, j, c] = (pv[:, cols] * gate[:, cols]).astype(o_ref.dtype)
        if c == KV_A - 1:
            tile_t = jnp.concatenate(
                [blocks.pop((t, j, cc)) for j in range(HEADS_PER_TILE) for cc in range(KV_A)], axis=0)
            o_ref[pl.ds(rows[t], TQ), :] = tile_t.T

    _pipeline([(t, c) for t in range(TILES_PER_GROUP) for c in range(KV_A)], start, finish)
    return carry


def _attention_a(qkv, sink):
    b, s, _ = qkv.shape
    dq = N_HEADS * HEAD_DIM
    dkv = KV_A * HEAD_DIM
    tk = TQ + 2 * WINDOW_A
    offsets = (0, WINDOW_A, 2 * WINDOW_A)
    kern = functools.partial(_attn_a_kernel, seq_len=s, tk=tk, window=WINDOW_A, offsets=offsets)
    return pl.pallas_call(
        kern,
        grid=(b,),
        in_specs=[pl.BlockSpec(memory_space=pltpu.SMEM),
                  pl.BlockSpec((None, s, dq), lambda bi: (bi, 0, 0)),
                  pl.BlockSpec((None, s, dkv), lambda bi: (bi, 0, dq // dkv)),
                  pl.BlockSpec((None, s, dkv), lambda bi: (bi, 0, dq // dkv + 1))],
        out_specs=pl.BlockSpec((None, s, dq), lambda bi: (bi, 0, 0)),
        out_shape=jax.ShapeDtypeStruct((b, s, dq), BF16),
        scratch_shapes=[pltpu.VMEM((len(offsets), KV_A, tk, HEADS_PER_TILE * TQ), F32)],
        compiler_params=_params("arbitrary"),
    )(sink, qkv, qkv, qkv)


def _post_kernel(*refs, n_groups, final, ff_chunks):
    refs = list(refs)
    x_ref = refs.pop(0)
    o_refs = [refs.pop(0) for _ in range(n_groups)]
    lse_refs = [refs.pop(0) for _ in range(n_groups)] if n_groups > 1 else []
    wo_ref, gf_ref, wg_ref, wu_ref, wd_ref = refs[:5]
    refs = refs[5:]
    gfin_ref = refs.pop(0) if final else None
    out_ref = refs.pop(0)

    if n_groups > 1:
        src = lax.broadcasted_iota(jnp.int32, (LANES, LANES), 0)
        dst = lax.broadcasted_iota(jnp.int32, (LANES, LANES), 1)
        fold = jnp.where((src % N_HEADS == dst % N_HEADS) & (src < LSE_PARTS * N_HEADS), 1.0, 0.0).astype(BF16)
        src = lax.broadcasted_iota(jnp.int32, (LANES, D_MODEL), 0)
        dst = lax.broadcasted_iota(jnp.int32, (LANES, D_MODEL), 1)
        expand = jnp.where((src % N_HEADS == dst // HEAD_DIM) & (src < LSE_PARTS * N_HEADS), 1.0, 0.0)
        expand = expand.astype(BF16)
        lses = [jnp.dot(r[...], fold, preferred_element_type=F32) for r in lse_refs]
        mx = functools.reduce(jnp.maximum, lses)
        es = [jnp.exp2(l - mx) for l in lses]
        den = functools.reduce(lambda a, b: a + b, es)
        piece_of_lane = lax.broadcasted_iota(jnp.int32, es[0].shape, 1) // N_HEADS
        o = None
        for e, o_ref in zip(es, o_refs):
            packed = jnp.zeros(e.shape, BF16)
            for k, piece in enumerate(_split_f32(e / den)):
                packed = jnp.where(piece_of_lane == k, piece, packed)
            term = jnp.dot(packed, expand, preferred_element_type=F32) * o_ref[...].astype(F32)
            o = term if o is None else o + term
        o = o.astype(BF16)
    else:
        o = o_refs[0][...]

    x1 = x_ref[...] + jnp.dot(o, wo_ref[...], preferred_element_type=F32)
    h = _rms(x1, gf_ref[...]).astype(BF16)
    acc = x1
    lo = 0
    for fc in ff_chunks:
        gate = jnp.dot(h, wg_ref[:, lo:lo + fc], preferred_element_type=F32)
        up = jnp.dot(h, wu_ref[:, lo:lo + fc], preferred_element_type=F32)
        a = (gate / (1.0 + jnp.exp(-gate)) * up).astype(BF16)
        acc = acc + jnp.dot(a, wd_ref[lo:lo + fc, :], preferred_element_type=F32)
        lo += fc
    if final:
        acc = _rms(acc, gfin_ref[...])
    out_ref[...] = acc


def _ff_chunks(d_ff, n):
    tiles, rem = divmod(d_ff, MXU_DIM)
    assert rem == 0
    return tuple((tiles // n + (k < tiles % n)) * MXU_DIM for k in range(n))


def _layer_spec(shape, layer):
    return pl.BlockSpec((None,) + tuple(shape[1:]), lambda *_: (layer,) + (0,) * (len(shape) - 1),
                        pipeline_mode=pl.Buffered(1))


def _post(x2d, os, lses, wo, gf, wg, wu, wd, layer, gfin, tm, ff_chunks):
    t, d = x2d.shape
    ff_chunks = _ff_chunks(wg.shape[2], ff_chunks)
    final = gfin is not None
    row = lambda n: pl.BlockSpec((tm, n), lambda i: (i, 0))
    in_specs = [row(d)] * (1 + len(os)) + [row(LANES)] * len(lses)
    in_specs += [_const_spec((d, d))] + [_layer_spec(a.shape, layer) for a in (gf, wg, wu, wd)]
    args = [x2d, *os, *lses, wo, gf, wg, wu, wd]
    if final:
        in_specs.append(_const_spec((1, d)))
        args.append(gfin.reshape(1, d))
    return pl.pallas_call(
        functools.partial(_post_kernel, n_groups=len(os), final=final, ff_chunks=ff_chunks),
        grid=(t // tm,),
        in_specs=in_specs,
        out_specs=row(d),
        out_shape=jax.ShapeDtypeStruct((t, d), F32),
        compiler_params=_params("parallel"),
    )(*args)


def _q_column_scale(n_cols, is_query_col):
    cols = np.arange(n_cols)
    return jnp.asarray(np.where(is_query_col(cols), HEAD_DIM ** -0.5 * LOG2E, 1.0), F32)


def kernel(x, norm_mix, norm_ffn, w_qkv_a, w_out_a, sink_a, w_qkv_b, w_out_b,
           w_gate, w_up, w_down, norm_final):
    b, s, d = x.shape
    t = b * s
    dq = N_HEADS * HEAD_DIM
    ffn = (norm_ffn.reshape(-1, 1, d), w_gate.astype(BF16), w_up.astype(BF16), w_down.astype(BF16))

    wa = w_qkv_a[0] * _q_column_scale(w_qkv_a.shape[2], lambda c: c < dq)
    wq = wa[:, :dq].reshape(d, KV_A, HEADS_PER_TILE, HEAD_DIM).transpose(0, 2, 1, 3).reshape(d, dq)
    wa = jnp.concatenate([wq, wa[:, dq:]], axis=1).astype(BF16)
    wo_a = w_out_a[0].reshape(KV_A, HEADS_PER_TILE, HEAD_DIM, d).transpose(1, 0, 2, 3).reshape(dq, d)
    x2d = x.reshape(t, d)
    qkv = _norm_matmul(x2d, norm_mix[0], wa, tm=1024).reshape(b, s, -1)
    o_a = _attention_a(qkv, sink_a[0])
    x2d = _post(x2d, [o_a.reshape(t, d)], [], wo_a.astype(BF16), *ffn, 0, None, tm=512, ff_chunks=2)

    wb = (w_qkv_b[0] * _q_column_scale(w_qkv_b.shape[2], lambda c: c % (3 * dq) < dq)).astype(BF16)
    dils = tuple(dil for _, dil in DILATED_GROUPS)
    qkvs = _norm_matmul_dilated(x2d.reshape(b, s, d), norm_mix[1], wb, dils)
    os, lses = [], []
    for qkv_g, (win, dil) in zip(qkvs, DILATED_GROUPS):
        o_g, lse_g = _attention_b(qkv_g, dil=dil, window=win // (2 * dil))
        os.append(o_g)
        lses.append(lse_g)
    out = _post(x2d, os, lses, w_out_b[0].astype(BF16), *ffn, 1, norm_final, tm=512, ff_chunks=2)
    return out.reshape(b, s, d)
```

```python
import functools

import jax
import jax.numpy as jnp
import numpy as np
from jax import lax
from jax.experimental import pallas as pl
from jax.experimental.pallas import tpu as pltpu

D_MODEL = 1024
HEAD_DIM = 64
N_HEADS = 16
KV_A = 4
WINDOW_A = 128
DILATED_GROUPS = ((128, 1), (512, 4), (2048, 16))
RMS_EPS = 1e-6
NEG = -1e30
LANES = 128
MXU_DIM = 256
HEADS_PER_TILE = MXU_DIM // HEAD_DIM
N_TILES = N_HEADS // HEADS_PER_TILE
TQ = 128
TILES_PER_GROUP = 8
PIPELINE_DEPTH = 4
A_HEADS = 4
B_HEADS = 2
B_COLS = B_HEADS * HEAD_DIM
B_STAGES = N_HEADS // B_HEADS
SPAN = 512
REGROUP = MXU_DIM
LSE_PARTS = 3
DEN_ROWS = 16
LOG2E = 1.4426950408889634
VMEM_LIMIT = 56 * 1024 * 1024

F32 = jnp.float32
BF16 = jnp.bfloat16


def _rms(x, g):
    ms = jnp.mean(x * x, axis=-1, keepdims=True)
    return x * lax.rsqrt(ms + RMS_EPS) * g


def _alibi_slopes(n):
    return [2.0 ** (-8.0 * (i + 1) / n) for i in range(n)]


def _const_spec(shape):
    return pl.BlockSpec(shape, lambda *_: (0,) * len(shape), pipeline_mode=pl.Buffered(1))


def _params(*sem):
    return pltpu.CompilerParams(dimension_semantics=sem, vmem_limit_bytes=VMEM_LIMIT)


def _grid_transpose_perm(n, n_major, n_minor):
    row = lax.broadcasted_iota(jnp.int32, (n, n), 0)
    col = lax.broadcasted_iota(jnp.int32, (n, n), 1)
    src = (row % n_minor) * n_major + row // n_minor
    return jnp.where(col == src, 1.0, 0.0).astype(BF16)


def _permute_rows(perm, x):
    return jnp.dot(perm, x, preferred_element_type=F32).astype(BF16)


def _norm_matmul_kernel(x_ref, g_ref, w_ref, o_ref):
    h = _rms(x_ref[...], g_ref[...]).astype(BF16)
    o_ref[...] = jnp.dot(h, w_ref[...], preferred_element_type=F32).astype(o_ref.dtype)


def _norm_matmul(x2d, g, w, tm):
    t, d = x2d.shape
    n = w.shape[1]
    return pl.pallas_call(
        _norm_matmul_kernel,
        grid=(t // tm,),
        in_specs=[pl.BlockSpec((tm, d), lambda i: (i, 0)), _const_spec((1, d)), _const_spec((d, n))],
        out_specs=pl.BlockSpec((tm, n), lambda i: (i, 0)),
        out_shape=jax.ShapeDtypeStruct((t, n), BF16),
        compiler_params=_params("parallel"),
    )(x2d, g.reshape(1, d), w)


def _norm_matmul_dilated_kernel(x_ref, g_ref, w_ref, *o_refs, dils):
    h = _rms(x_ref[...], g_ref[...]).astype(BF16)
    n = o_refs[0].shape[-1]
    for gi, (dil, o_ref) in enumerate(zip(dils, o_refs)):
        hg = h if dil == 1 else _to_classes(h, dil)
        y = jnp.dot(hg, w_ref[:, gi * n:(gi + 1) * n], preferred_element_type=F32).astype(o_ref.dtype)
        o_ref[0] = y.reshape(o_ref.shape[1:])


def _to_classes(h, dil):
    per = REGROUP // dil
    perm = _grid_transpose_perm(REGROUP, dil, per)
    parts = [_permute_rows(perm, h[b * REGROUP:(b + 1) * REGROUP]) for b in range(SPAN // REGROUP)]
    return jnp.concatenate([p[r * per:(r + 1) * per] for r in range(dil) for p in parts], axis=0)


def _norm_matmul_dilated(x, g, w, dils):
    b, s, d = x.shape
    n = w.shape[1] // len(dils)
    out_specs = [pl.BlockSpec((1, dil, SPAN // dil, n), lambda bi, i: (bi, 0, i, 0)) for dil in dils]
    out_shape = [jax.ShapeDtypeStruct((b, dil, s // dil, n), BF16) for dil in dils]
    return pl.pallas_call(
        functools.partial(_norm_matmul_dilated_kernel, dils=dils),
        grid=(b, s // SPAN),
        in_specs=[pl.BlockSpec((None, SPAN, d), lambda bi, i: (bi, i, 0)),
                  _const_spec((1, d)), _const_spec(w.shape)],
        out_specs=out_specs,
        out_shape=out_shape,
        compiler_params=_params("parallel", "parallel"),
    )(x, g.reshape(1, d), w)


def _lane_group(shape):
    return lax.broadcasted_iota(jnp.int32, shape, 1) // HEAD_DIM


def _fill_bias(bias_ref, heads, *, tk, window, unit, offsets):
    slopes = _alibi_slopes(N_HEADS)
    key = lax.broadcasted_iota(jnp.int32, (tk, TQ), 0)
    qry = lax.broadcasted_iota(jnp.int32, (tk, TQ), 1)
    for v, off in enumerate(offsets):
        dist = jnp.abs(key - qry - off)
        negd = jnp.where(dist <= window, -(dist.astype(F32) * float(unit)), NEG)
        for c in range(N_HEADS // heads):
            for j in range(heads):
                bias_ref[v, c, :, j * TQ:(j + 1) * TQ] = (slopes[heads * c + j] * LOG2E) * negd


def _stack_heads(tiles, groups):
    lg = _lane_group(tiles[0].shape)
    return jnp.concatenate(
        [jnp.where(lg == g, t, jnp.zeros_like(t)) for t, g in zip(tiles, groups)], axis=0)


def _scores(q_stack, k, bias):
    return lax.dot_general(k, q_stack, (((1,), (1,)), ((), ())), preferred_element_type=F32) + bias


def _with_ones(vt):
    return jnp.concatenate([vt, jnp.ones((DEN_ROWS, vt.shape[1]), vt.dtype)], axis=0)


def _softmax_pv(sc, vt1):
    m = jnp.max(sc, axis=0, keepdims=True)
    p = jnp.exp2(sc - m)
    pv = jnp.dot(vt1, p.astype(BF16), preferred_element_type=F32)
    rows = vt1.shape[0] - DEN_ROWS
    return pv[:rows], m, pv[rows:rows + 1]


def _pipeline(stages, start_fn, finish_fn):
    started = [start_fn(st) for st in stages[:PIPELINE_DEPTH]]
    for k, st in enumerate(stages):
        cur = started.pop(0)
        if k + PIPELINE_DEPTH < len(stages):
            started.append(start_fn(stages[k + PIPELINE_DEPTH]))
        finish_fn(st, cur)


def _window_start(q0, window, tk, seq_len):
    start = jnp.clip(q0 - window, 0, seq_len - tk)
    return pl.multiple_of(start, min(window, TQ)), (q0 - start) // window


def _split_f32(x):
    parts = []
    for _ in range(LSE_PARTS):
        p = x.astype(BF16)
        parts.append(p)
        x = x - p.astype(F32)
    return parts


def _attn_b_tiles(qkv_ref, tiles, bias_ref, stage_ref, *, tk, window, seq_len):
    dq = N_HEADS * HEAD_DIM
    head_row = lax.broadcasted_iota(jnp.int32, (N_HEADS, TQ), 0)
    windows = [_window_start(q0, window, tk, seq_len) for _, _, q0, _ in tiles]
    acc = {}

    def start(stage):
        t, c = stage
        idx, r0, _, _ = tiles[t]
        win0, var = windows[t]
        lo = c * B_COLS
        q = qkv_ref[idx + (pl.ds(r0, TQ), slice(lo, lo + B_COLS))]
        k = qkv_ref[idx + (pl.ds(win0, tk), slice(dq + lo, dq + lo + B_COLS))]
        v = qkv_ref[idx + (pl.ds(win0, tk), slice(2 * dq + lo, 2 * dq + lo + B_COLS))]
        q_stack = _stack_heads([q] * B_HEADS, range(B_HEADS))
        return _scores(q_stack, k, bias_ref[var, c]), _with_ones(v.T)

    def finish(stage, started):
        t, c = stage
        pv, m, den = _softmax_pv(*started)
        rden = 1.0 / den
        lse = m + jnp.log2(den)
        outs, lse16 = acc.get(t, ([], jnp.zeros((N_HEADS, TQ), F32)))
        for j in range(B_HEADS):
            cols = slice(j * TQ, (j + 1) * TQ)
            outs.append((pv[j * HEAD_DIM:(j + 1) * HEAD_DIM, cols] * rden[:, cols]).astype(BF16))
            lse16 = jnp.where(head_row == B_HEADS * c + j, lse[:, cols], lse16)
        acc[t] = (outs, lse16)
        if c == B_STAGES - 1:
            pad = jnp.zeros((LANES - LSE_PARTS * N_HEADS, TQ), BF16)
            tile_t = jnp.concatenate(outs + _split_f32(lse16) + [pad], axis=0)
            stage_ref[pl.ds(tiles[t][3], TQ), :] = tile_t.T

    _pipeline([(t, c) for t in range(len(tiles)) for c in range(B_STAGES)], start, finish)


def _attn_b_kernel(qkv_ref, o_ref, lse_ref, bias_ref, stage_ref, *, seq_len, tk, window, unit, offsets, dil):
    @pl.when(pl.program_id(0) == 0)
    def _():
        _fill_bias(bias_ref, B_HEADS, tk=tk, window=window, unit=unit, offsets=offsets)

    dq = N_HEADS * HEAD_DIM
    n_tiles = dil * seq_len // TQ

    def tile_group(g, carry):
        tiles = []
        for u in range(TILES_PER_GROUP):
            t = g * TILES_PER_GROUP + u
            s0 = pl.multiple_of(t * TQ, TQ)
            if dil == 1:
                tiles.append(((), s0, s0, s0))
            else:
                q0 = pl.multiple_of((t // dil) * TQ, TQ)
                tiles.append(((t % dil,), q0, q0, s0))
        _attn_b_tiles(qkv_ref, tiles, bias_ref, stage_ref, tk=tk, window=window, seq_len=seq_len)
        return carry

    lax.fori_loop(0, n_tiles // TILES_PER_GROUP, tile_group, 0)

    if dil == 1:
        o_ref[...] = stage_ref[:, :dq]
        lse_ref[...] = stage_ref[:, dq:]
    else:
        per_cls = REGROUP // dil
        perm = _grid_transpose_perm(REGROUP, per_cls, dil)
        for k in range(n_tiles * TQ // REGROUP):
            span, off = divmod(k * per_cls, TQ)
            rows = [stage_ref[(span * dil + r) * TQ + off:(span * dil + r) * TQ + off + per_cls, :]
                    for r in range(dil)]
            nat = _permute_rows(perm, jnp.concatenate(rows, axis=0))
            o_ref[k * REGROUP:(k + 1) * REGROUP, :] = nat[:, :dq]
            lse_ref[k * REGROUP:(k + 1) * REGROUP, :] = nat[:, dq:]


def _attention_b(qkv, *, dil, window):
    batch, _, seq_len, width = qkv.shape
    dq = N_HEADS * HEAD_DIM
    seq = seq_len * dil
    tk = min(TQ + 2 * window, seq_len)
    offsets = (0,) if tk == seq_len else (0, window, 2 * window)
    if dil == 1:
        in_spec = pl.BlockSpec((None, None, seq_len, width), lambda b: (b, 0, 0, 0))
    else:
        in_spec = pl.BlockSpec((None, dil, seq_len, width), lambda b: (b, 0, 0, 0))
    kern = functools.partial(_attn_b_kernel, seq_len=seq_len, tk=tk, window=window, unit=dil,
                             offsets=offsets, dil=dil)
    return pl.pallas_call(
        kern,
        grid=(batch,),
        in_specs=[in_spec],
        out_specs=[pl.BlockSpec((seq, dq), lambda b: (b, 0)), pl.BlockSpec((seq, LANES), lambda b: (b, 0))],
        out_shape=[jax.ShapeDtypeStruct((batch * seq, dq), BF16),
                   jax.ShapeDtypeStruct((batch * seq, LANES), BF16)],
        scratch_shapes=[pltpu.VMEM((len(offsets), B_STAGES, tk, B_HEADS * TQ), F32),
                        pltpu.VMEM((seq, dq + LANES), BF16)],
        compiler_params=_params("arbitrary"),
    )(qkv)


def _attn_a_kernel(sink_ref, q_ref, k_ref, v_ref, o_ref, bias_ref, *, seq_len, tk, window, offsets):
    @pl.when(pl.program_id(0) == 0)
    def _():
        _fill_bias(bias_ref, HEADS_PER_TILE, tk=tk, window=window, unit=1, offsets=offsets)

    lax.fori_loop(0, seq_len // (TILES_PER_GROUP * TQ),
                  functools.partial(_attn_a_tile_group, sink_ref, q_ref, k_ref, v_ref, o_ref, bias_ref,
                                    seq_len=seq_len, tk=tk, window=window), 0)


def _attn_a_tile_group(sink_ref, q_ref, k_ref, v_ref, o_ref, bias_ref, g, carry, *, seq_len, tk, window):
    qblk = lax.broadcasted_iota(jnp.int32, (1, A_HEADS * TQ), 1) // TQ
    rows = [pl.multiple_of((g * TILES_PER_GROUP + t) * TQ, TQ) for t in range(TILES_PER_GROUP)]
    tiles = [_window_start(q0, window, tk, seq_len) for q0 in rows]
    shared = {}
    blocks = {}

    def start(stage):
        t, c, half = stage
        win0, var = tiles[t]
        if (c, half) == (0, 0):
            qs = [q_ref[pl.ds(rows[t], TQ), j * MXU_DIM:(j + 1) * MXU_DIM] for j in range(HEADS_PER_TILE)]
            shared[t] = (qs, k_ref[pl.ds(win0, tk), :], v_ref[pl.ds(win0, tk), :].T)
        qs, k4, vt = shared[t]
        vt_c = _with_ones(vt[c * HEAD_DIM:(c + 1) * HEAD_DIM])
        q_stack = _stack_heads(qs[half * A_HEADS:(half + 1) * A_HEADS], [c] * A_HEADS)
        bias = bias_ref[var, c, :, half * A_HEADS * TQ:(half + 1) * A_HEADS * TQ]
        return _scores(q_stack, k4, bias), vt_c

    def finish(stage, started):
        t, c, half = stage
        pv, m, den = _softmax_pv(*started)
        sink = jnp.zeros_like(m)
        for jj in range(A_HEADS):
            sink = jnp.where(qblk == jj, sink_ref[HEADS_PER_TILE * c + half * A_HEADS + jj] * LOG2E, sink)
        gate = 1.0 / (den + jnp.exp2(sink - m))
        for jj in range(A_HEADS):
            cols = slice(jj * TQ, (jj + 1) * TQ)
            blocks[t, half * A_HEADS + jj, c] = (pv[:, cols] * gate[:, cols]).astype(o_ref.dtype)
        if (c, half) == (KV_A - 1, HEADS_PER_TILE // A_HEADS - 1):
            tile_t = jnp.concatenate(
                [blocks.pop((t, j, cc)) for j in range(HEADS_PER_TILE) for cc in range(KV_A)], axis=0)
            o_ref[pl.ds(rows[t], TQ), :] = tile_t.T

    _pipeline([(t, c, half) for t in range(TILES_PER_GROUP) for c in range(KV_A)
               for half in range(HEADS_PER_TILE // A_HEADS)], start, finish)
    return carry


def _attention_a(qkv, sink):
    b, s, _ = qkv.shape
    dq = N_HEADS * HEAD_DIM
    dkv = KV_A * HEAD_DIM
    tk = TQ + 2 * WINDOW_A
    offsets = (0, WINDOW_A, 2 * WINDOW_A)
    kern = functools.partial(_attn_a_kernel, seq_len=s, tk=tk, window=WINDOW_A, offsets=offsets)
    return pl.pallas_call(
        kern,
        grid=(b,),
        in_specs=[pl.BlockSpec(memory_space=pltpu.SMEM),
                  pl.BlockSpec((None, s, dq), lambda bi: (bi, 0, 0)),
                  pl.BlockSpec((None, s, dkv), lambda bi: (bi, 0, dq // dkv)),
                  pl.BlockSpec((None, s, dkv), lambda bi: (bi, 0, dq // dkv + 1))],
        out_specs=pl.BlockSpec((None, s, dq), lambda bi: (bi, 0, 0)),
        out_shape=jax.ShapeDtypeStruct((b, s, dq), BF16),
        scratch_shapes=[pltpu.VMEM((len(offsets), KV_A, tk, HEADS_PER_TILE * TQ), F32)],
        compiler_params=_params("arbitrary"),
    )(sink, qkv, qkv, qkv)


def _merge_groups(o_refs, lse_refs):
    src = lax.broadcasted_iota(jnp.int32, (LANES, LANES), 0)
    dst = lax.broadcasted_iota(jnp.int32, (LANES, LANES), 1)
    fold = jnp.where((src % N_HEADS == dst % N_HEADS) & (src < LSE_PARTS * N_HEADS), 1.0, 0.0).astype(BF16)
    src = lax.broadcasted_iota(jnp.int32, (LANES, D_MODEL), 0)
    dst = lax.broadcasted_iota(jnp.int32, (LANES, D_MODEL), 1)
    expand = jnp.where((src % N_HEADS == dst // HEAD_DIM) & (src < LSE_PARTS * N_HEADS), 1.0, 0.0).astype(BF16)
    lses = [jnp.dot(r[...], fold, preferred_element_type=F32) for r in lse_refs]
    mx = functools.reduce(jnp.maximum, lses)
    es = [jnp.exp2(l - mx) for l in lses]
    den = functools.reduce(lambda a, b: a + b, es)
    piece_of_lane = lax.broadcasted_iota(jnp.int32, es[0].shape, 1) // N_HEADS
    o = None
    w_sum = None
    for e, o_ref in zip(es[:-1], o_refs[:-1]):
        packed = jnp.zeros(e.shape, BF16)
        for k, piece in enumerate(_split_f32(e / den)):
            packed = jnp.where(piece_of_lane == k, piece, packed)
        w = jnp.dot(packed, expand, preferred_element_type=F32)
        w_sum = w if w_sum is None else w_sum + w
        o = w * o_ref[...].astype(F32) if o is None else o + w * o_ref[...].astype(F32)
    o = o + (1.0 - w_sum) * o_refs[-1][...].astype(F32)
    return o.astype(BF16)


def _post_kernel(*refs, n_groups, final, ff_chunks):
    refs = list(refs)
    x_ref = refs.pop(0)
    o_refs = [refs.pop(0) for _ in range(n_groups)]
    lse_refs = [refs.pop(0) for _ in range(n_groups)] if n_groups > 1 else []
    wo_ref, gf_ref, wg_ref, wu_ref, wd_ref = refs[:5]
    refs = refs[5:]
    gfin_ref = refs.pop(0) if final else None
    out_ref = refs.pop(0)

    o = _merge_groups(o_refs, lse_refs) if n_groups > 1 else o_refs[0][...]
    x1 = x_ref[...] + jnp.dot(o, wo_ref[...], preferred_element_type=F32)
    h = _rms(x1, gf_ref[...]).astype(BF16)
    acc = x1
    lo = 0
    for fc in ff_chunks:
        gate = jnp.dot(h, wg_ref[:, lo:lo + fc], preferred_element_type=F32)
        up = jnp.dot(h, wu_ref[:, lo:lo + fc], preferred_element_type=F32)
        a = (gate / (1.0 + jnp.exp(-gate)) * up).astype(BF16)
        acc = acc + jnp.dot(a, wd_ref[lo:lo + fc, :], preferred_element_type=F32)
        lo += fc
    if final:
        acc = _rms(acc, gfin_ref[...])
    out_ref[...] = acc


def _ff_chunks(d_ff, n):
    tiles, rem = divmod(d_ff, MXU_DIM)
    assert rem == 0
    return tuple((tiles // n + (k < tiles % n)) * MXU_DIM for k in range(n))


def _layer_spec(shape, layer):
    return pl.BlockSpec((None,) + tuple(shape[1:]), lambda *_: (layer,) + (0,) * (len(shape) - 1),
                        pipeline_mode=pl.Buffered(1))


def _post(x2d, os, lses, wo, gf, wg, wu, wd, layer, gfin, tm, ff_chunks):
    t, d = x2d.shape
    ff_chunks = _ff_chunks(wg.shape[2], ff_chunks)
    final = gfin is not None
    row = lambda n: pl.BlockSpec((tm, n), lambda i: (i, 0))
    in_specs = [row(d)] * (1 + len(os)) + [row(LANES)] * len(lses)
    in_specs += [_const_spec((d, d))] + [_layer_spec(a.shape, layer) for a in (gf, wg, wu, wd)]
    args = [x2d, *os, *lses, wo, gf, wg, wu, wd]
    if final:
        in_specs.append(_const_spec((1, d)))
        args.append(gfin.reshape(1, d))
    return pl.pallas_call(
        functools.partial(_post_kernel, n_groups=len(os), final=final, ff_chunks=ff_chunks),
        grid=(t // tm,),
        in_specs=in_specs,
        out_specs=row(d),
        out_shape=jax.ShapeDtypeStruct((t, d), F32),
        compiler_params=_params("parallel"),
    )(*args)


def _q_column_scale(n_cols, is_query_col):
    cols = np.arange(n_cols)
    return jnp.asarray(np.where(is_query_col(cols), HEAD_DIM ** -0.5 * LOG2E, 1.0), F32)


def kernel(x, norm_mix, norm_ffn, w_qkv_a, w_out_a, sink_a, w_qkv_b, w_out_b,
           w_gate, w_up, w_down, norm_final):
    b, s, d = x.shape
    t = b * s
    dq = N_HEADS * HEAD_DIM
    ffn = (norm_ffn.reshape(-1, 1, d), w_gate.astype(BF16), w_up.astype(BF16), w_down.astype(BF16))

    wa = w_qkv_a[0] * _q_column_scale(w_qkv_a.shape[2], lambda c: c < dq)
    wq = wa[:, :dq].reshape(d, KV_A, HEADS_PER_TILE, HEAD_DIM).transpose(0, 2, 1, 3).reshape(d, dq)
    wa = jnp.concatenate([wq, wa[:, dq:]], axis=1).astype(BF16)
    wo_a = w_out_a[0].reshape(KV_A, HEADS_PER_TILE, HEAD_DIM, d).transpose(1, 0, 2, 3).reshape(dq, d)
    x2d = x.reshape(t, d)
    qkv = _norm_matmul(x2d, norm_mix[0], wa, tm=2048).reshape(b, s, -1)
    o_a = _attention_a(qkv, sink_a[0])
    x2d = _post(x2d, [o_a.reshape(t, d)], [], wo_a.astype(BF16), *ffn, 0, None, tm=512, ff_chunks=2)

    wb = (w_qkv_b[0] * _q_column_scale(w_qkv_b.shape[2], lambda c: c % (3 * dq) < dq)).astype(BF16)
    dils = tuple(dil for _, dil in DILATED_GROUPS)
    qkvs = _norm_matmul_dilated(x2d.reshape(b, s, d), norm_mix[1], wb, dils)
    os, lses = [], []
    for qkv_g, (win, dil) in zip(qkvs, DILATED_GROUPS):
        o_g, lse_g = _attention_b(qkv_g, dil=dil, window=win // (2 * dil))
        os.append(o_g)
        lses.append(lse_g)
    out = _post(x2d, os, lses, w_out_b[0].astype(BF16), *ffn, 1, norm_final, tm=512, ff_chunks=2)
    return out.reshape(b, s, d)
```

```python
import functools

import jax
import jax.numpy as jnp
import numpy as np
from jax import lax
from jax.experimental import pallas as pl
from jax.experimental.pallas import tpu as pltpu

D_MODEL = 1024
HEAD_DIM = 64
N_HEADS = 16
KV_A = 4
WINDOW_A = 128
DILATED_GROUPS = ((128, 1), (512, 4), (2048, 16))
RMS_EPS = 1e-6
NEG = -1e30
LANES = 128
MXU_DIM = 256
HEADS_PER_TILE = MXU_DIM // HEAD_DIM
N_TILES = N_HEADS // HEADS_PER_TILE
TQ = 128
TILES_PER_GROUP = 8
PIPELINE_DEPTH = 4
A_HEADS = 4
B_HEADS = 2
B_COLS = B_HEADS * HEAD_DIM
B_STAGES = N_HEADS // B_HEADS
SPAN = 512
REGROUP = MXU_DIM
LSE_PARTS = 3
DEN_ROWS = 16
LOG2E = 1.4426950408889634
VMEM_LIMIT = 56 * 1024 * 1024

F32 = jnp.float32
BF16 = jnp.bfloat16


def _rms(x, g):
    ms = jnp.mean(x * x, axis=-1, keepdims=True)
    return x * lax.rsqrt(ms + RMS_EPS) * g


def _alibi_slopes(n):
    return [2.0 ** (-8.0 * (i + 1) / n) for i in range(n)]


def _const_spec(shape):
    return pl.BlockSpec(shape, lambda *_: (0,) * len(shape), pipeline_mode=pl.Buffered(1))


def _params(*sem):
    return pltpu.CompilerParams(dimension_semantics=sem, vmem_limit_bytes=VMEM_LIMIT)


def _grid_transpose_perm(n, n_major, n_minor):
    row = lax.broadcasted_iota(jnp.int32, (n, n), 0)
    col = lax.broadcasted_iota(jnp.int32, (n, n), 1)
    src = (row % n_minor) * n_major + row // n_minor
    return jnp.where(col == src, 1.0, 0.0).astype(BF16)


def _permute_rows(perm, x):
    return jnp.dot(perm, x, preferred_element_type=F32).astype(BF16)


def _row_chunk_spec(a2d, n_steps):
    rows, rem = divmod(a2d.shape[0], n_steps)
    assert rem == 0 and rows % 16 == 0
    return pl.BlockSpec((rows, a2d.shape[1]), lambda i: (i, 0))


def _norm_matmul_kernel(x_ref, g_ref, w_ref, *rest):
    n_casts = (len(rest) - 1) // 2
    o_ref = rest[n_casts]
    h = _rms(x_ref[...], g_ref[...]).astype(BF16)
    o_ref[...] = jnp.dot(h, w_ref[...], preferred_element_type=F32).astype(o_ref.dtype)
    for src_ref, dst_ref in zip(rest[:n_casts], rest[n_casts + 1:]):
        dst_ref[...] = src_ref[...].astype(dst_ref.dtype)


def _norm_matmul(x2d, g, w, tm, casts=()):
    t, d = x2d.shape
    n = w.shape[1]
    n_steps = t // tm
    cast_specs = [_row_chunk_spec(a, n_steps) for a in casts]
    return pl.pallas_call(
        _norm_matmul_kernel,
        grid=(n_steps,),
        in_specs=[pl.BlockSpec((tm, d), lambda i: (i, 0)), _const_spec((1, d)), _const_spec((d, n))] + cast_specs,
        out_specs=[pl.BlockSpec((tm, n), lambda i: (i, 0))] + cast_specs,
        out_shape=[jax.ShapeDtypeStruct((t, n), BF16)] + [jax.ShapeDtypeStruct(a.shape, BF16) for a in casts],
        compiler_params=_params("parallel"),
    )(x2d, g.reshape(1, d), w, *casts)


def _norm_matmul_dilated_kernel(x_ref, g_ref, w_ref, *o_refs, dils):
    h = _rms(x_ref[...], g_ref[...]).astype(BF16)
    n = o_refs[0].shape[-1]
    for gi, (dil, o_ref) in enumerate(zip(dils, o_refs)):
        hg = h if dil == 1 else _to_classes(h, dil)
        y = jnp.dot(hg, w_ref[:, gi * n:(gi + 1) * n], preferred_element_type=F32).astype(o_ref.dtype)
        o_ref[0] = y.reshape(o_ref.shape[1:])


def _to_classes(h, dil):
    per = REGROUP // dil
    perm = _grid_transpose_perm(REGROUP, dil, per)
    parts = [_permute_rows(perm, h[b * REGROUP:(b + 1) * REGROUP]) for b in range(SPAN // REGROUP)]
    return jnp.concatenate([p[r * per:(r + 1) * per] for r in range(dil) for p in parts], axis=0)


def _norm_matmul_dilated(x, g, w, dils):
    b, s, d = x.shape
    n = w.shape[1] // len(dils)
    out_specs = [pl.BlockSpec((1, dil, SPAN // dil, n), lambda bi, i: (bi, 0, i, 0)) for dil in dils]
    out_shape = [jax.ShapeDtypeStruct((b, dil, s // dil, n), BF16) for dil in dils]
    return pl.pallas_call(
        functools.partial(_norm_matmul_dilated_kernel, dils=dils),
        grid=(b, s // SPAN),
        in_specs=[pl.BlockSpec((None, SPAN, d), lambda bi, i: (bi, i, 0)),
                  _const_spec((1, d)), _const_spec(w.shape)],
        out_specs=out_specs,
        out_shape=out_shape,
        compiler_params=_params("parallel", "parallel"),
    )(x, g.reshape(1, d), w)


def _lane_group(shape):
    return lax.broadcasted_iota(jnp.int32, shape, 1) // HEAD_DIM


def _fill_bias(bias_ref, heads, *, tk, window, unit, offsets):
    slopes = _alibi_slopes(N_HEADS)
    key = lax.broadcasted_iota(jnp.int32, (tk, TQ), 0)
    qry = lax.broadcasted_iota(jnp.int32, (tk, TQ), 1)
    for v, off in enumerate(offsets):
        dist = jnp.abs(key - qry - off)
        negd = jnp.where(dist <= window, -(dist.astype(F32) * float(unit)), NEG)
        for c in range(N_HEADS // heads):
            for j in range(heads):
                bias_ref[v, c, :, j * TQ:(j + 1) * TQ] = (slopes[heads * c + j] * LOG2E) * negd


def _stack_heads(tiles, groups):
    lg = _lane_group(tiles[0].shape)
    return jnp.concatenate(
        [jnp.where(lg == g, t, jnp.zeros_like(t)) for t, g in zip(tiles, groups)], axis=0)


def _scores(q_stack, k, bias):
    return lax.dot_general(k, q_stack, (((1,), (1,)), ((), ())), preferred_element_type=F32) + bias


def _with_ones(vt):
    return jnp.concatenate([vt, jnp.ones((DEN_ROWS, vt.shape[1]), vt.dtype)], axis=0)


def _softmax_pv(sc, vt1):
    m = jnp.max(sc, axis=0, keepdims=True)
    p = jnp.exp2(sc - m)
    pv = jnp.dot(vt1, p.astype(BF16), preferred_element_type=F32)
    rows = vt1.shape[0] - DEN_ROWS
    return pv[:rows], m, pv[rows:rows + 1]


def _pipeline(stages, start_fn, finish_fn):
    started = [start_fn(st) for st in stages[:PIPELINE_DEPTH]]
    for k, st in enumerate(stages):
        cur = started.pop(0)
        if k + PIPELINE_DEPTH < len(stages):
            started.append(start_fn(stages[k + PIPELINE_DEPTH]))
        finish_fn(st, cur)


def _window_start(q0, window, tk, seq_len):
    start = jnp.clip(q0 - window, 0, seq_len - tk)
    return pl.multiple_of(start, min(window, TQ)), (q0 - start) // window


def _split_f32(x):
    parts = []
    for _ in range(LSE_PARTS):
        p = x.astype(BF16)
        parts.append(p)
        x = x - p.astype(F32)
    return parts


def _attn_b_tiles(qkv_ref, tiles, bias_ref, out_refs, *, tk, window, seq_len):
    dq = N_HEADS * HEAD_DIM
    head_row = lax.broadcasted_iota(jnp.int32, (N_HEADS, TQ), 0)
    windows = [_window_start(q0, window, tk, seq_len) for _, _, q0, _ in tiles]
    acc = {}

    def start(stage):
        t, c = stage
        idx, r0, _, _ = tiles[t]
        win0, var = windows[t]
        lo = c * B_COLS
        q = qkv_ref[idx + (pl.ds(r0, TQ), slice(lo, lo + B_COLS))]
        k = qkv_ref[idx + (pl.ds(win0, tk), slice(dq + lo, dq + lo + B_COLS))]
        v = qkv_ref[idx + (pl.ds(win0, tk), slice(2 * dq + lo, 2 * dq + lo + B_COLS))]
        q_stack = _stack_heads([q] * B_HEADS, range(B_HEADS))
        return _scores(q_stack, k, bias_ref[var, c]), _with_ones(v.T)

    def finish(stage, started):
        t, c = stage
        pv, m, den = _softmax_pv(*started)
        rden = 1.0 / den
        lse = m + jnp.log2(den)
        outs, lse16 = acc.get(t, ([], jnp.zeros((N_HEADS, TQ), F32)))
        for j in range(B_HEADS):
            cols = slice(j * TQ, (j + 1) * TQ)
            outs.append((pv[j * HEAD_DIM:(j + 1) * HEAD_DIM, cols] * rden[:, cols]).astype(BF16))
            lse16 = jnp.where(head_row == B_HEADS * c + j, lse[:, cols], lse16)
        acc[t] = (outs, lse16)
        if c == B_STAGES - 1:
            pad = jnp.zeros((LANES - LSE_PARTS * N_HEADS, TQ), BF16)
            tile_t = jnp.concatenate(outs + _split_f32(lse16) + [pad], axis=0)
            rows, lo = pl.ds(tiles[t][3], TQ), 0
            tile = tile_t.T
            for ref in out_refs:
                ref[rows, :] = tile[:, lo:lo + ref.shape[1]]
                lo += ref.shape[1]

    _pipeline([(t, c) for t in range(len(tiles)) for c in range(B_STAGES)], start, finish)


def _attn_b_kernel(qkv_ref, o_ref, lse_ref, bias_ref, *stage, seq_len, tk, window, unit, offsets, dil):
    @pl.when(pl.program_id(0) == 0)
    def _():
        _fill_bias(bias_ref, B_HEADS, tk=tk, window=window, unit=unit, offsets=offsets)

    dq = N_HEADS * HEAD_DIM
    n_tiles = dil * seq_len // TQ
    out_refs = (o_ref, lse_ref) if dil == 1 else stage

    def tile_group(g, carry):
        tiles = []
        for u in range(TILES_PER_GROUP):
            t = g * TILES_PER_GROUP + u
            s0 = pl.multiple_of(t * TQ, TQ)
            if dil == 1:
                tiles.append(((), s0, s0, s0))
            else:
                q0 = pl.multiple_of((t // dil) * TQ, TQ)
                tiles.append(((t % dil,), q0, q0, s0))
        _attn_b_tiles(qkv_ref, tiles, bias_ref, out_refs, tk=tk, window=window, seq_len=seq_len)
        return carry

    lax.fori_loop(0, n_tiles // TILES_PER_GROUP, tile_group, 0)

    if dil > 1:
        (stage_ref,) = stage
        per_cls = REGROUP // dil
        perm = _grid_transpose_perm(REGROUP, per_cls, dil)
        for k in range(n_tiles * TQ // REGROUP):
            span, off = divmod(k * per_cls, TQ)
            rows = [stage_ref[(span * dil + r) * TQ + off:(span * dil + r) * TQ + off + per_cls, :]
                    for r in range(dil)]
            nat = _permute_rows(perm, jnp.concatenate(rows, axis=0))
            o_ref[k * REGROUP:(k + 1) * REGROUP, :] = nat[:, :dq]
            lse_ref[k * REGROUP:(k + 1) * REGROUP, :] = nat[:, dq:]


def _attention_b(qkv, *, dil, window):
    batch, _, seq_len, width = qkv.shape
    dq = N_HEADS * HEAD_DIM
    seq = seq_len * dil
    tk = min(TQ + 2 * window, seq_len)
    offsets = (0,) if tk == seq_len else (0, window, 2 * window)
    if dil == 1:
        in_spec = pl.BlockSpec((None, None, seq_len, width), lambda b: (b, 0, 0, 0))
    else:
        in_spec = pl.BlockSpec((None, dil, seq_len, width), lambda b: (b, 0, 0, 0))
    kern = functools.partial(_attn_b_kernel, seq_len=seq_len, tk=tk, window=window, unit=dil,
                             offsets=offsets, dil=dil)
    return pl.pallas_call(
        kern,
        grid=(batch,),
        in_specs=[in_spec],
        out_specs=[pl.BlockSpec((seq, dq), lambda b: (b, 0)), pl.BlockSpec((seq, LANES), lambda b: (b, 0))],
        out_shape=[jax.ShapeDtypeStruct((batch * seq, dq), BF16),
                   jax.ShapeDtypeStruct((batch * seq, LANES), BF16)],
        scratch_shapes=[pltpu.VMEM((len(offsets), B_STAGES, tk, B_HEADS * TQ), F32)]
        + ([pltpu.VMEM((seq, dq + LANES), BF16)] if dil > 1 else []),
        compiler_params=_params("arbitrary"),
    )(qkv)


def _attn_a_kernel(sink_ref, q_ref, k_ref, v_ref, o_ref, bias_ref, *, seq_len, tk, window, offsets):
    @pl.when(pl.program_id(0) == 0)
    def _():
        _fill_bias(bias_ref, HEADS_PER_TILE, tk=tk, window=window, unit=1, offsets=offsets)

    lax.fori_loop(0, seq_len // (TILES_PER_GROUP * TQ),
                  functools.partial(_attn_a_tile_group, sink_ref, q_ref, k_ref, v_ref, o_ref, bias_ref,
                                    seq_len=seq_len, tk=tk, window=window), 0)


def _attn_a_tile_group(sink_ref, q_ref, k_ref, v_ref, o_ref, bias_ref, g, carry, *, seq_len, tk, window):
    qblk = lax.broadcasted_iota(jnp.int32, (1, A_HEADS * TQ), 1) // TQ
    rows = [pl.multiple_of((g * TILES_PER_GROUP + t) * TQ, TQ) for t in range(TILES_PER_GROUP)]
    tiles = [_window_start(q0, window, tk, seq_len) for q0 in rows]
    shared = {}
    blocks = {}

    def start(stage):
        t, c, half = stage
        win0, var = tiles[t]
        if (c, half) == (0, 0):
            qs = [q_ref[pl.ds(rows[t], TQ), j * MXU_DIM:(j + 1) * MXU_DIM] for j in range(HEADS_PER_TILE)]
            shared[t] = (qs, k_ref[pl.ds(win0, tk), :], v_ref[pl.ds(win0, tk), :].T)
        qs, k4, vt = shared[t]
        vt_c = _with_ones(vt[c * HEAD_DIM:(c + 1) * HEAD_DIM])
        q_stack = _stack_heads(qs[half * A_HEADS:(half + 1) * A_HEADS], [c] * A_HEADS)
        bias = bias_ref[var, c, :, half * A_HEADS * TQ:(half + 1) * A_HEADS * TQ]
        return _scores(q_stack, k4, bias), vt_c

    def finish(stage, started):
        t, c, half = stage
        pv, m, den = _softmax_pv(*started)
        sink = jnp.zeros_like(m)
        for jj in range(A_HEADS):
            sink = jnp.where(qblk == jj, sink_ref[HEADS_PER_TILE * c + half * A_HEADS + jj] * LOG2E, sink)
        gate = 1.0 / (den + jnp.exp2(sink - m))
        for jj in range(A_HEADS):
            cols = slice(jj * TQ, (jj + 1) * TQ)
            blocks[t, half * A_HEADS + jj, c] = (pv[:, cols] * gate[:, cols]).astype(o_ref.dtype)
        if (c, half) == (KV_A - 1, HEADS_PER_TILE // A_HEADS - 1):
            tile_t = jnp.concatenate(
                [blocks.pop((t, j, cc)) for j in range(HEADS_PER_TILE) for cc in range(KV_A)], axis=0)
            o_ref[pl.ds(rows[t], TQ), :] = tile_t.T

    _pipeline([(t, c, half) for t in range(TILES_PER_GROUP) for c in range(KV_A)
               for half in range(HEADS_PER_TILE // A_HEADS)], start, finish)
    return carry


def _attention_a(qkv, sink):
    b, s, _ = qkv.shape
    dq = N_HEADS * HEAD_DIM
    dkv = KV_A * HEAD_DIM
    tk = TQ + 2 * WINDOW_A
    offsets = (0, WINDOW_A, 2 * WINDOW_A)
    kern = functools.partial(_attn_a_kernel, seq_len=s, tk=tk, window=WINDOW_A, offsets=offsets)
    return pl.pallas_call(
        kern,
        grid=(b,),
        in_specs=[pl.BlockSpec(memory_space=pltpu.SMEM),
                  pl.BlockSpec((None, s, dq), lambda bi: (bi, 0, 0)),
                  pl.BlockSpec((None, s, dkv), lambda bi: (bi, 0, dq // dkv)),
                  pl.BlockSpec((None, s, dkv), lambda bi: (bi, 0, dq // dkv + 1))],
        out_specs=pl.BlockSpec((None, s, dq), lambda bi: (bi, 0, 0)),
        out_shape=jax.ShapeDtypeStruct((b, s, dq), BF16),
        scratch_shapes=[pltpu.VMEM((len(offsets), KV_A, tk, HEADS_PER_TILE * TQ), F32)],
        compiler_params=_params("arbitrary"),
    )(sink, qkv, qkv, qkv)


def _merge_groups(o_refs, lse_refs):
    src = lax.broadcasted_iota(jnp.int32, (LANES, LANES), 0)
    dst = lax.broadcasted_iota(jnp.int32, (LANES, LANES), 1)
    fold = jnp.where((src % N_HEADS == dst % N_HEADS) & (src < LSE_PARTS * N_HEADS), 1.0, 0.0).astype(BF16)
    src = lax.broadcasted_iota(jnp.int32, (LANES, D_MODEL), 0)
    dst = lax.broadcasted_iota(jnp.int32, (LANES, D_MODEL), 1)
    expand = jnp.where((src % N_HEADS == dst // HEAD_DIM) & (src < LSE_PARTS * N_HEADS), 1.0, 0.0).astype(BF16)
    lses = [jnp.dot(r[...], fold, preferred_element_type=F32) for r in lse_refs]
    mx = functools.reduce(jnp.maximum, lses)
    es = [jnp.exp2(l - mx) for l in lses]
    den = functools.reduce(lambda a, b: a + b, es)
    piece_of_lane = lax.broadcasted_iota(jnp.int32, es[0].shape, 1) // N_HEADS
    o = None
    w_sum = None
    for e, o_ref in zip(es[:-1], o_refs[:-1]):
        packed = jnp.zeros(e.shape, BF16)
        for k, piece in enumerate(_split_f32(e / den)):
            packed = jnp.where(piece_of_lane == k, piece, packed)
        w = jnp.dot(packed, expand, preferred_element_type=F32)
        w_sum = w if w_sum is None else w_sum + w
        o = w * o_ref[...].astype(F32) if o is None else o + w * o_ref[...].astype(F32)
    o = o + (1.0 - w_sum) * o_refs[-1][...].astype(F32)
    return o.astype(BF16)


def _post_kernel(*refs, n_groups, final, ff_chunks, side_cast):
    refs = list(refs)
    x_ref = refs.pop(0)
    o_refs = [refs.pop(0) for _ in range(n_groups)]
    lse_refs = [refs.pop(0) for _ in range(n_groups)] if n_groups > 1 else []
    wo_ref, gf_ref, wg_ref, wu_ref, wd_ref = refs[:5]
    refs = refs[5:]
    gfin_ref = refs.pop(0) if final else None
    if side_cast:
        src_ref, scale_ref, out_ref, dst_ref = refs
        dst_ref[...] = (src_ref[...] * scale_ref[...]).astype(dst_ref.dtype)
    else:
        (out_ref,) = refs

    o = _merge_groups(o_refs, lse_refs) if n_groups > 1 else o_refs[0][...]
    x1 = x_ref[...] + jnp.dot(o, wo_ref[...], preferred_element_type=F32)
    h = _rms(x1, gf_ref[...]).astype(BF16)
    acc = x1
    lo = 0
    for fc in ff_chunks:
        gate = jnp.dot(h, wg_ref[:, lo:lo + fc], preferred_element_type=F32)
        up = jnp.dot(h, wu_ref[:, lo:lo + fc], preferred_element_type=F32)
        a = (gate / (1.0 + jnp.exp(-gate)) * up).astype(BF16)
        acc = acc + jnp.dot(a, wd_ref[lo:lo + fc, :], preferred_element_type=F32)
        lo += fc
    if final:
        acc = _rms(acc, gfin_ref[...])
    out_ref[...] = acc


def _ff_chunks(d_ff, n):
    tiles, rem = divmod(d_ff, MXU_DIM)
    assert rem == 0
    return tuple((tiles // n + (k < tiles % n)) * MXU_DIM for k in range(n))


def _layer_spec(shape, layer):
    return pl.BlockSpec((None,) + tuple(shape[1:]), lambda *_: (layer,) + (0,) * (len(shape) - 1),
                        pipeline_mode=pl.Buffered(1))


def _post(x2d, os, lses, wo, gf, wg, wu, wd, layer, gfin, tm, ff_chunks, side_cast=None):
    t, d = x2d.shape
    n_steps = t // tm
    ff_chunks = _ff_chunks(wg.shape[2], ff_chunks)
    final = gfin is not None
    row = lambda n: pl.BlockSpec((tm, n), lambda i: (i, 0))
    in_specs = [row(d)] * (1 + len(os)) + [row(LANES)] * len(lses)
    in_specs += [_const_spec((d, d))] + [_layer_spec(a.shape, layer) for a in (gf, wg, wu, wd)]
    args = [x2d, *os, *lses, wo, gf, wg, wu, wd]
    out_specs = [row(d)]
    out_shape = [jax.ShapeDtypeStruct((t, d), F32)]
    if final:
        in_specs.append(_const_spec((1, d)))
        args.append(gfin.reshape(1, d))
    if side_cast:
        w, scale = side_cast
        in_specs += [_row_chunk_spec(w, n_steps), _const_spec((1, w.shape[1]))]
        args += [w, scale.reshape(1, -1)]
        out_specs.append(_row_chunk_spec(w, n_steps))
        out_shape.append(jax.ShapeDtypeStruct(w.shape, BF16))
    return pl.pallas_call(
        functools.partial(_post_kernel, n_groups=len(os), final=final, ff_chunks=ff_chunks,
                          side_cast=bool(side_cast)),
        grid=(n_steps,),
        in_specs=in_specs,
        out_specs=out_specs,
        out_shape=out_shape,
        compiler_params=_params("parallel"),
    )(*args)


def _q_column_scale(n_cols, is_query_col):
    cols = np.arange(n_cols)
    return jnp.asarray(np.where(is_query_col(cols), HEAD_DIM ** -0.5 * LOG2E, 1.0), F32)


def kernel(x, norm_mix, norm_ffn, w_qkv_a, w_out_a, sink_a, w_qkv_b, w_out_b,
           w_gate, w_up, w_down, norm_final):
    b, s, d = x.shape
    t = b * s
    dq = N_HEADS * HEAD_DIM
    d_ff = w_gate.shape[2]

    wa = w_qkv_a[0] * _q_column_scale(w_qkv_a.shape[2], lambda c: c < dq)
    wq = wa[:, :dq].reshape(d, KV_A, HEADS_PER_TILE, HEAD_DIM).transpose(0, 2, 1, 3).reshape(d, dq)
    wa = jnp.concatenate([wq, wa[:, dq:]], axis=1).astype(BF16)
    wo_a = w_out_a[0].reshape(KV_A, HEADS_PER_TILE, HEAD_DIM, d).transpose(1, 0, 2, 3).reshape(dq, d)
    x2d = x.reshape(t, d)
    qkv, wg, wu, wd = _norm_matmul(
        x2d, norm_mix[0], wa, tm=2048,
        casts=(w_gate.reshape(-1, d_ff), w_up.reshape(-1, d_ff), w_down.reshape(-1, d)))
    ffn = (norm_ffn.reshape(-1, 1, d), wg.reshape(w_gate.shape), wu.reshape(w_up.shape), wd.reshape(w_down.shape))
    o_a = _attention_a(qkv.reshape(b, s, -1), sink_a[0])
    wb_scale = _q_column_scale(w_qkv_b.shape[2], lambda c: c % (3 * dq) < dq)
    x2d, wb = _post(x2d, [o_a.reshape(t, d)], [], wo_a.astype(BF16), *ffn, 0, None, tm=512, ff_chunks=2,
                    side_cast=(w_qkv_b[0], wb_scale))

    dils = tuple(dil for _, dil in DILATED_GROUPS)
    qkvs = _norm_matmul_dilated(x2d.reshape(b, s, d), norm_mix[1], wb, dils)
    os, lses = [], []
    for qkv_g, (win, dil) in zip(qkvs, DILATED_GROUPS):
        o_g, lse_g = _attention_b(qkv_g, dil=dil, window=win // (2 * dil))
        os.append(o_g)
        lses.append(lse_g)
    (out,) = _post(x2d, os, lses, w_out_b[0].astype(BF16), *ffn, 1, norm_final, tm=512, ff_chunks=2)
    return out.reshape(b, s, d)
```

```python
import functools

import jax
import jax.numpy as jnp
import numpy as np
from jax import lax
from jax.experimental import pallas as pl
from jax.experimental.pallas import tpu as pltpu

D_MODEL = 1024
HEAD_DIM = 64
N_HEADS = 16
KV_A = 4
WINDOW_A = 128
DILATED_GROUPS = ((128, 1), (512, 4), (2048, 16))
RMS_EPS = 1e-6
NEG = -1e30
LANES = 128
MXU_DIM = 256
HEADS_PER_TILE = MXU_DIM // HEAD_DIM
N_TILES = N_HEADS // HEADS_PER_TILE
TQ = 128
TILES_PER_GROUP = 8
PIPELINE_DEPTH = 4
A_HEADS = 4
B_HEADS = 2
B_COLS = B_HEADS * HEAD_DIM
B_STAGES = N_HEADS // B_HEADS
SPAN = 512
REGROUP = MXU_DIM
LSE_PARTS = 3
DEN_ROWS = 16
LOG2E = 1.4426950408889634
VMEM_LIMIT = 56 * 1024 * 1024

F32 = jnp.float32
BF16 = jnp.bfloat16


def _rms(x, g):
    ms = jnp.mean(x * x, axis=-1, keepdims=True)
    return x * lax.rsqrt(ms + RMS_EPS) * g


def _alibi_slopes(n):
    return [2.0 ** (-8.0 * (i + 1) / n) for i in range(n)]


def _const_spec(shape):
    return pl.BlockSpec(shape, lambda *_: (0,) * len(shape), pipeline_mode=pl.Buffered(1))


def _params(*sem):
    return pltpu.CompilerParams(dimension_semantics=sem, vmem_limit_bytes=VMEM_LIMIT)


def _grid_transpose_perm(n, n_major, n_minor):
    row = lax.broadcasted_iota(jnp.int32, (n, n), 0)
    col = lax.broadcasted_iota(jnp.int32, (n, n), 1)
    src = (row % n_minor) * n_major + row // n_minor
    return jnp.where(col == src, 1.0, 0.0).astype(BF16)


def _permute_rows(perm, x):
    return jnp.dot(perm, x, preferred_element_type=F32).astype(BF16)


def _row_chunk_spec(a2d, n_steps):
    rows, rem = divmod(a2d.shape[0], n_steps)
    assert rem == 0 and rows % 16 == 0
    return pl.BlockSpec((rows, a2d.shape[1]), lambda i: (i, 0))


def _norm_matmul_kernel(x_ref, g_ref, w_ref, *rest):
    n_casts = (len(rest) - 1) // 2
    o_ref = rest[n_casts]
    h = _rms(x_ref[...], g_ref[...]).astype(BF16)
    o_ref[...] = _pack_rows(jnp.dot(h, w_ref[...], preferred_element_type=F32).astype(BF16))
    for src_ref, dst_ref in zip(rest[:n_casts], rest[n_casts + 1:]):
        dst_ref[...] = src_ref[...].astype(dst_ref.dtype)


def _norm_matmul(x2d, g, w, tm, casts=()):
    t, d = x2d.shape
    n = w.shape[1]
    n_steps = t // tm
    cast_specs = [_row_chunk_spec(a, n_steps) for a in casts]
    return pl.pallas_call(
        _norm_matmul_kernel,
        grid=(n_steps,),
        in_specs=[pl.BlockSpec((tm, d), lambda i: (i, 0)), _const_spec((1, d)), _const_spec((d, n))] + cast_specs,
        out_specs=[pl.BlockSpec((tm // 2, n), lambda i: (i, 0))] + cast_specs,
        out_shape=[jax.ShapeDtypeStruct((t // 2, n), jnp.int32)]
        + [jax.ShapeDtypeStruct(a.shape, BF16) for a in casts],
        compiler_params=_params("parallel"),
    )(x2d, g.reshape(1, d), w, *casts)


def _pack_rows(x):
    return pltpu.bitcast(x, jnp.int32)


def _unpack_rows(x):
    return pltpu.bitcast(x, BF16)


def _norm_matmul_dilated_kernel(x_ref, g_ref, w_ref, *o_refs, dils):
    h = _rms(x_ref[...], g_ref[...]).astype(BF16)
    n = o_refs[0].shape[-1]
    for gi, (dil, o_ref) in enumerate(zip(dils, o_refs)):
        hg = h if dil == 1 else _to_classes(h, dil)
        y = jnp.dot(hg, w_ref[:, gi * n:(gi + 1) * n], preferred_element_type=F32).astype(BF16)
        o_ref[0] = _pack_rows(y).reshape(o_ref.shape[1:])


def _to_classes(h, dil):
    per = REGROUP // dil
    perm = _grid_transpose_perm(REGROUP, dil, per)
    parts = [_permute_rows(perm, h[b * REGROUP:(b + 1) * REGROUP]) for b in range(SPAN // REGROUP)]
    return jnp.concatenate([p[r * per:(r + 1) * per] for r in range(dil) for p in parts], axis=0)


def _norm_matmul_dilated(x, g, w, dils):
    b, s, d = x.shape
    n = w.shape[1] // len(dils)
    out_specs = [pl.BlockSpec((1, dil, SPAN // dil // 2, n), lambda bi, i: (bi, 0, i, 0)) for dil in dils]
    out_shape = [jax.ShapeDtypeStruct((b, dil, s // dil // 2, n), jnp.int32) for dil in dils]
    return pl.pallas_call(
        functools.partial(_norm_matmul_dilated_kernel, dils=dils),
        grid=(b, s // SPAN),
        in_specs=[pl.BlockSpec((None, SPAN, d), lambda bi, i: (bi, i, 0)),
                  _const_spec((1, d)), _const_spec(w.shape)],
        out_specs=out_specs,
        out_shape=out_shape,
        compiler_params=_params("parallel", "parallel"),
    )(x, g.reshape(1, d), w)


def _lane_group(shape):
    return lax.broadcasted_iota(jnp.int32, shape, 1) // HEAD_DIM


def _fill_bias(bias_ref, heads, *, tk, window, unit, offsets):
    slopes = _alibi_slopes(N_HEADS)
    key = lax.broadcasted_iota(jnp.int32, (tk, TQ), 0)
    qry = lax.broadcasted_iota(jnp.int32, (tk, TQ), 1)
    for v, off in enumerate(offsets):
        dist = jnp.abs(key - qry - off)
        negd = jnp.where(dist <= window, -(dist.astype(F32) * float(unit)), NEG)
        for c in range(N_HEADS // heads):
            for j in range(heads):
                bias_ref[v, c, :, j * TQ:(j + 1) * TQ] = (slopes[heads * c + j] * LOG2E) * negd


def _stack_heads(tiles, groups):
    lg = _lane_group(tiles[0].shape)
    return jnp.concatenate(
        [jnp.where(lg == g, t, jnp.zeros_like(t)) for t, g in zip(tiles, groups)], axis=0)


def _scores(q_stack, k, bias):
    return lax.dot_general(k, q_stack, (((1,), (1,)), ((), ())), preferred_element_type=F32) + bias


def _with_ones(vt):
    return jnp.concatenate([vt, jnp.ones((DEN_ROWS, vt.shape[1]), vt.dtype)], axis=0)


def _softmax_pv(sc, vt1):
    m = jnp.max(sc, axis=0, keepdims=True)
    p = jnp.exp2(sc - m)
    pv = jnp.dot(vt1, p.astype(BF16), preferred_element_type=F32)
    rows = vt1.shape[0] - DEN_ROWS
    return pv[:rows], m, pv[rows:rows + 1]


def _pipeline(stages, start_fn, finish_fn):
    started = [start_fn(st) for st in stages[:PIPELINE_DEPTH]]
    for k, st in enumerate(stages):
        cur = started.pop(0)
        if k + PIPELINE_DEPTH < len(stages):
            started.append(start_fn(stages[k + PIPELINE_DEPTH]))
        finish_fn(st, cur)


def _window_start(q0, window, tk, seq_len):
    start = jnp.clip(q0 - window, 0, seq_len - tk)
    return pl.multiple_of(start, min(window, TQ)), (q0 - start) // window


def _split_f32(x):
    parts = []
    for _ in range(LSE_PARTS):
        p = x.astype(BF16)
        parts.append(p)
        x = x - p.astype(F32)
    return parts


def _attn_b_tiles(qkv_ref, tiles, bias_ref, out_refs, *, tk, window, seq_len):
    dq = N_HEADS * HEAD_DIM
    head_row = lax.broadcasted_iota(jnp.int32, (N_HEADS, TQ), 0)
    windows = [_window_start(q0, window, tk, seq_len) for _, _, q0, _ in tiles]
    acc = {}

    def start(stage):
        t, c = stage
        idx, r0, _, _ = tiles[t]
        win0, var = windows[t]
        lo = c * B_COLS
        q_rows = pl.ds(pl.multiple_of(r0 // 2, TQ // 2), TQ // 2)
        kv_rows = pl.ds(pl.multiple_of(win0 // 2, min(window, TQ) // 2), tk // 2)
        q = _unpack_rows(qkv_ref[idx + (q_rows, slice(lo, lo + B_COLS))])
        k = _unpack_rows(qkv_ref[idx + (kv_rows, slice(dq + lo, dq + lo + B_COLS))])
        v = _unpack_rows(qkv_ref[idx + (kv_rows, slice(2 * dq + lo, 2 * dq + lo + B_COLS))])
        q_stack = _stack_heads([q] * B_HEADS, range(B_HEADS))
        return _scores(q_stack, k, bias_ref[var, c]), _with_ones(v.T)

    def finish(stage, started):
        t, c = stage
        pv, m, den = _softmax_pv(*started)
        rden = 1.0 / den
        lse = m + jnp.log2(den)
        outs, lse16 = acc.get(t, ([], jnp.zeros((N_HEADS, TQ), F32)))
        for j in range(B_HEADS):
            cols = slice(j * TQ, (j + 1) * TQ)
            outs.append((pv[j * HEAD_DIM:(j + 1) * HEAD_DIM, cols] * rden[:, cols]).astype(BF16))
            lse16 = jnp.where(head_row == B_HEADS * c + j, lse[:, cols], lse16)
        acc[t] = (outs, lse16)
        if c == B_STAGES - 1:
            pad = jnp.zeros((LANES - LSE_PARTS * N_HEADS, TQ), BF16)
            tile_t = jnp.concatenate(outs + _split_f32(lse16) + [pad], axis=0)
            rows, lo = pl.ds(tiles[t][3], TQ), 0
            tile = tile_t.T
            for ref in out_refs:
                ref[rows, :] = tile[:, lo:lo + ref.shape[1]]
                lo += ref.shape[1]

    _pipeline([(t, c) for t in range(len(tiles)) for c in range(B_STAGES)], start, finish)


def _attn_b_kernel(qkv_ref, o_ref, lse_ref, bias_ref, *stage, seq_len, tk, window, unit, offsets, dil):
    @pl.when(pl.program_id(0) == 0)
    def _():
        _fill_bias(bias_ref, B_HEADS, tk=tk, window=window, unit=unit, offsets=offsets)

    dq = N_HEADS * HEAD_DIM
    n_tiles = dil * seq_len // TQ
    out_refs = (o_ref, lse_ref) if dil == 1 else stage

    def tile_group(g, carry):
        tiles = []
        for u in range(TILES_PER_GROUP):
            t = g * TILES_PER_GROUP + u
            s0 = pl.multiple_of(t * TQ, TQ)
            if dil == 1:
                tiles.append(((), s0, s0, s0))
            else:
                q0 = pl.multiple_of((t // dil) * TQ, TQ)
                tiles.append(((t % dil,), q0, q0, s0))
        _attn_b_tiles(qkv_ref, tiles, bias_ref, out_refs, tk=tk, window=window, seq_len=seq_len)
        return carry

    lax.fori_loop(0, n_tiles // TILES_PER_GROUP, tile_group, 0)

    if dil > 1:
        (stage_ref,) = stage
        per_cls = REGROUP // dil
        perm = _grid_transpose_perm(REGROUP, per_cls, dil)
        for k in range(n_tiles * TQ // REGROUP):
            span, off = divmod(k * per_cls, TQ)
            rows = [stage_ref[(span * dil + r) * TQ + off:(span * dil + r) * TQ + off + per_cls, :]
                    for r in range(dil)]
            nat = _permute_rows(perm, jnp.concatenate(rows, axis=0))
            o_ref[k * REGROUP:(k + 1) * REGROUP, :] = nat[:, :dq]
            lse_ref[k * REGROUP:(k + 1) * REGROUP, :] = nat[:, dq:]


def _attention_b(qkv, *, dil, window):
    batch, _, packed_len, width = qkv.shape
    seq_len = 2 * packed_len
    dq = N_HEADS * HEAD_DIM
    seq = seq_len * dil
    tk = min(TQ + 2 * window, seq_len)
    offsets = (0,) if tk == seq_len else (0, window, 2 * window)
    if dil == 1:
        in_spec = pl.BlockSpec((None, None, packed_len, width), lambda b: (b, 0, 0, 0))
    else:
        in_spec = pl.BlockSpec((None, dil, packed_len, width), lambda b: (b, 0, 0, 0))
    kern = functools.partial(_attn_b_kernel, seq_len=seq_len, tk=tk, window=window, unit=dil,
                             offsets=offsets, dil=dil)
    return pl.pallas_call(
        kern,
        grid=(batch,),
        in_specs=[in_spec],
        out_specs=[pl.BlockSpec((seq, dq), lambda b: (b, 0)), pl.BlockSpec((seq, LANES), lambda b: (b, 0))],
        out_shape=[jax.ShapeDtypeStruct((batch * seq, dq), BF16),
                   jax.ShapeDtypeStruct((batch * seq, LANES), BF16)],
        scratch_shapes=[pltpu.VMEM((len(offsets), B_STAGES, tk, B_HEADS * TQ), F32)]
        + ([pltpu.VMEM((seq, dq + LANES), BF16)] if dil > 1 else []),
        compiler_params=_params("arbitrary"),
    )(qkv)


def _attn_a_kernel(sink_ref, q_ref, k_ref, v_ref, o_ref, bias_ref, *, seq_len, tk, window, offsets):
    @pl.when(pl.program_id(0) == 0)
    def _():
        _fill_bias(bias_ref, HEADS_PER_TILE, tk=tk, window=window, unit=1, offsets=offsets)

    lax.fori_loop(0, seq_len // (TILES_PER_GROUP * TQ),
                  functools.partial(_attn_a_tile_group, sink_ref, q_ref, k_ref, v_ref, o_ref, bias_ref,
                                    seq_len=seq_len, tk=tk, window=window), 0)


def _attn_a_tile_group(sink_ref, q_ref, k_ref, v_ref, o_ref, bias_ref, g, carry, *, seq_len, tk, window):
    qblk = lax.broadcasted_iota(jnp.int32, (1, A_HEADS * TQ), 1) // TQ
    rows = [pl.multiple_of((g * TILES_PER_GROUP + t) * TQ, TQ) for t in range(TILES_PER_GROUP)]
    tiles = [_window_start(q0, window, tk, seq_len) for q0 in rows]
    shared = {}
    blocks = {}

    def start(stage):
        t, c, half = stage
        win0, var = tiles[t]
        if (c, half) == (0, 0):
            q_rows = pl.ds(pl.multiple_of(rows[t] // 2, TQ // 2), TQ // 2)
            kv_rows = pl.ds(pl.multiple_of(win0 // 2, min(window, TQ) // 2), tk // 2)
            qs = [_unpack_rows(q_ref[q_rows, j * MXU_DIM:(j + 1) * MXU_DIM]) for j in range(HEADS_PER_TILE)]
            shared[t] = (qs, _unpack_rows(k_ref[kv_rows, :]), _unpack_rows(v_ref[kv_rows, :]).T)
        qs, k4, vt = shared[t]
        vt_c = _with_ones(vt[c * HEAD_DIM:(c + 1) * HEAD_DIM])
        q_stack = _stack_heads(qs[half * A_HEADS:(half + 1) * A_HEADS], [c] * A_HEADS)
        bias = bias_ref[var, c, :, half * A_HEADS * TQ:(half + 1) * A_HEADS * TQ]
        return _scores(q_stack, k4, bias), vt_c

    def finish(stage, started):
        t, c, half = stage
        pv, m, den = _softmax_pv(*started)
        sink = jnp.zeros_like(m)
        for jj in range(A_HEADS):
            sink = jnp.where(qblk == jj, sink_ref[HEADS_PER_TILE * c + half * A_HEADS + jj] * LOG2E, sink)
        gate = 1.0 / (den + jnp.exp2(sink - m))
        for jj in range(A_HEADS):
            cols = slice(jj * TQ, (jj + 1) * TQ)
            blocks[t, half * A_HEADS + jj, c] = (pv[:, cols] * gate[:, cols]).astype(o_ref.dtype)
        if (c, half) == (KV_A - 1, HEADS_PER_TILE // A_HEADS - 1):
            tile_t = jnp.concatenate(
                [blocks.pop((t, j, cc)) for j in range(HEADS_PER_TILE) for cc in range(KV_A)], axis=0)
            o_ref[pl.ds(rows[t], TQ), :] = tile_t.T

    _pipeline([(t, c, half) for t in range(TILES_PER_GROUP) for c in range(KV_A)
               for half in range(HEADS_PER_TILE // A_HEADS)], start, finish)
    return carry


def _attention_a(qkv, sink):
    b, packed_s, _ = qkv.shape
    s = 2 * packed_s
    dq = N_HEADS * HEAD_DIM
    dkv = KV_A * HEAD_DIM
    tk = TQ + 2 * WINDOW_A
    offsets = (0, WINDOW_A, 2 * WINDOW_A)
    kern = functools.partial(_attn_a_kernel, seq_len=s, tk=tk, window=WINDOW_A, offsets=offsets)
    return pl.pallas_call(
        kern,
        grid=(b,),
        in_specs=[pl.BlockSpec(memory_space=pltpu.SMEM),
                  pl.BlockSpec((None, packed_s, dq), lambda bi: (bi, 0, 0)),
                  pl.BlockSpec((None, packed_s, dkv), lambda bi: (bi, 0, dq // dkv)),
                  pl.BlockSpec((None, packed_s, dkv), lambda bi: (bi, 0, dq // dkv + 1))],
        out_specs=pl.BlockSpec((None, s, dq), lambda bi: (bi, 0, 0)),
        out_shape=jax.ShapeDtypeStruct((b, s, dq), BF16),
        scratch_shapes=[pltpu.VMEM((len(offsets), KV_A, tk, HEADS_PER_TILE * TQ), F32)],
        compiler_params=_params("arbitrary"),
    )(sink, qkv, qkv, qkv)


def _merge_groups(o_refs, lse_refs):
    src = lax.broadcasted_iota(jnp.int32, (LANES, LANES), 0)
    dst = lax.broadcasted_iota(jnp.int32, (LANES, LANES), 1)
    fold = jnp.where((src % N_HEADS == dst % N_HEADS) & (src < LSE_PARTS * N_HEADS), 1.0, 0.0).astype(BF16)
    src = lax.broadcasted_iota(jnp.int32, (LANES, D_MODEL), 0)
    dst = lax.broadcasted_iota(jnp.int32, (LANES, D_MODEL), 1)
    expand = jnp.where((src % N_HEADS == dst // HEAD_DIM) & (src < LSE_PARTS * N_HEADS), 1.0, 0.0).astype(BF16)
    lses = [jnp.dot(r[...], fold, preferred_element_type=F32) for r in lse_refs]
    mx = functools.reduce(jnp.maximum, lses)
    es = [jnp.exp2(l - mx) for l in lses]
    den = functools.reduce(lambda a, b: a + b, es)
    piece_of_lane = lax.broadcasted_iota(jnp.int32, es[0].shape, 1) // N_HEADS
    o = None
    w_sum = None
    for e, o_ref in zip(es[:-1], o_refs[:-1]):
        packed = jnp.zeros(e.shape, BF16)
        for k, piece in enumerate(_split_f32(e / den)):
            packed = jnp.where(piece_of_lane == k, piece, packed)
        w = jnp.dot(packed, expand, preferred_element_type=F32)
        w_sum = w if w_sum is None else w_sum + w
        o = w * o_ref[...].astype(F32) if o is None else o + w * o_ref[...].astype(F32)
    o = o + (1.0 - w_sum) * o_refs[-1][...].astype(F32)
    return o.astype(BF16)


def _post_kernel(*refs, n_groups, final, ff_chunks, side_cast):
    refs = list(refs)
    x_ref = refs.pop(0)
    o_refs = [refs.pop(0) for _ in range(n_groups)]
    lse_refs = [refs.pop(0) for _ in range(n_groups)] if n_groups > 1 else []
    wo_ref, gf_ref, wg_ref, wu_ref, wd_ref = refs[:5]
    refs = refs[5:]
    gfin_ref = refs.pop(0) if final else None
    if side_cast:
        src_ref, scale_ref, out_ref, dst_ref = refs
        dst_ref[...] = (src_ref[...] * scale_ref[...]).astype(dst_ref.dtype)
    else:
        (out_ref,) = refs

    o = _merge_groups(o_refs, lse_refs) if n_groups > 1 else o_refs[0][...]
    x1 = x_ref[...] + jnp.dot(o, wo_ref[...], preferred_element_type=F32)
    h = _rms(x1, gf_ref[...]).astype(BF16)
    acc = x1
    lo = 0
    for fc in ff_chunks:
        gate = jnp.dot(h, wg_ref[:, lo:lo + fc], preferred_element_type=F32)
        up = jnp.dot(h, wu_ref[:, lo:lo + fc], preferred_element_type=F32)
        a = (gate / (1.0 + jnp.exp(-gate)) * up).astype(BF16)
        acc = acc + jnp.dot(a, wd_ref[lo:lo + fc, :], preferred_element_type=F32)
        lo += fc
    if final:
        acc = _rms(acc, gfin_ref[...])
    out_ref[...] = acc


def _ff_chunks(d_ff, n):
    tiles, rem = divmod(d_ff, MXU_DIM)
    assert rem == 0
    return tuple((tiles // n + (k < tiles % n)) * MXU_DIM for k in range(n))


def _layer_spec(shape, layer):
    return pl.BlockSpec((None,) + tuple(shape[1:]), lambda *_: (layer,) + (0,) * (len(shape) - 1),
                        pipeline_mode=pl.Buffered(1))


def _post(x2d, os, lses, wo, gf, wg, wu, wd, layer, gfin, tm, ff_chunks, side_cast=None):
    t, d = x2d.shape
    n_steps = t // tm
    ff_chunks = _ff_chunks(wg.shape[2], ff_chunks)
    final = gfin is not None
    row = lambda n: pl.BlockSpec((tm, n), lambda i: (i, 0))
    in_specs = [row(d)] * (1 + len(os)) + [row(LANES)] * len(lses)
    in_specs += [_const_spec((d, d))] + [_layer_spec(a.shape, layer) for a in (gf, wg, wu, wd)]
    args = [x2d, *os, *lses, wo, gf, wg, wu, wd]
    out_specs = [row(d)]
    out_shape = [jax.ShapeDtypeStruct((t, d), F32)]
    if final:
        in_specs.append(_const_spec((1, d)))
        args.append(gfin.reshape(1, d))
    if side_cast:
        w, scale = side_cast
        in_specs += [_row_chunk_spec(w, n_steps), _const_spec((1, w.shape[1]))]
        args += [w, scale.reshape(1, -1)]
        out_specs.append(_row_chunk_spec(w, n_steps))
        out_shape.append(jax.ShapeDtypeStruct(w.shape, BF16))
    return pl.pallas_call(
        functools.partial(_post_kernel, n_groups=len(os), final=final, ff_chunks=ff_chunks,
                          side_cast=bool(side_cast)),
        grid=(n_steps,),
        in_specs=in_specs,
        out_specs=out_specs,
        out_shape=out_shape,
        compiler_params=_params("parallel"),
    )(*args)


def _q_column_scale(n_cols, is_query_col):
    cols = np.arange(n_cols)
    return jnp.asarray(np.where(is_query_col(cols), HEAD_DIM ** -0.5 * LOG2E, 1.0), F32)


def kernel(x, norm_mix, norm_ffn, w_qkv_a, w_out_a, sink_a, w_qkv_b, w_out_b,
           w_gate, w_up, w_down, norm_final):
    b, s, d = x.shape
    t = b * s
    dq = N_HEADS * HEAD_DIM
    d_ff = w_gate.shape[2]

    wa = w_qkv_a[0] * _q_column_scale(w_qkv_a.shape[2], lambda c: c < dq)
    wq = wa[:, :dq].reshape(d, KV_A, HEADS_PER_TILE, HEAD_DIM).transpose(0, 2, 1, 3).reshape(d, dq)
    wa = jnp.concatenate([wq, wa[:, dq:]], axis=1).astype(BF16)
    wo_a = w_out_a[0].reshape(KV_A, HEADS_PER_TILE, HEAD_DIM, d).transpose(1, 0, 2, 3).reshape(dq, d)
    x2d = x.reshape(t, d)
    qkv, wg, wu, wd = _norm_matmul(
        x2d, norm_mix[0], wa, tm=2048,
        casts=(w_gate.reshape(-1, d_ff), w_up.reshape(-1, d_ff), w_down.reshape(-1, d)))
    ffn = (norm_ffn.reshape(-1, 1, d), wg.reshape(w_gate.shape), wu.reshape(w_up.shape), wd.reshape(w_down.shape))
    o_a = _attention_a(qkv.reshape(b, s // 2, -1), sink_a[0])
    wb_scale = _q_column_scale(w_qkv_b.shape[2], lambda c: c % (3 * dq) < dq)
    x2d, wb = _post(x2d, [o_a.reshape(t, d)], [], wo_a.astype(BF16), *ffn, 0, None, tm=512, ff_chunks=2,
                    side_cast=(w_qkv_b[0], wb_scale))

    dils = tuple(dil for _, dil in DILATED_GROUPS)
    qkvs = _norm_matmul_dilated(x2d.reshape(b, s, d), norm_mix[1], wb, dils)
    os, lses = [], []
    for qkv_g, (win, dil) in zip(qkvs, DILATED_GROUPS):
        o_g, lse_g = _attention_b(qkv_g, dil=dil, window=win // (2 * dil))
        os.append(o_g)
        lses.append(lse_g)
    (out,) = _post(x2d, os, lses, w_out_b[0].astype(BF16), *ffn, 1, norm_final, tm=512, ff_chunks=2)
    return out.reshape(b, s, d)
```

```python
import functools

import jax
import jax.numpy as jnp
import numpy as np
from jax import lax
from jax.experimental import pallas as pl
from jax.experimental.pallas import tpu as pltpu

D_MODEL = 1024
HEAD_DIM = 64
N_HEADS = 16
KV_A = 4
WINDOW_A = 128
DILATED_GROUPS = ((128, 1), (512, 4), (2048, 16))
RMS_EPS = 1e-6
NEG = -1e30
LANES = 128
MXU_DIM = 256
HEADS_PER_TILE = MXU_DIM // HEAD_DIM
N_TILES = N_HEADS // HEADS_PER_TILE
TQ = 128
A_TILES = 8
B_TILES = 16
A_DEPTH = 4
B_DEPTH = 6
A_HEADS = 4
B_HEADS = 2
B_COLS = B_HEADS * HEAD_DIM
B_STAGES = N_HEADS // B_HEADS
SPAN = 512
REGROUP = MXU_DIM
LSE_PARTS = 3
DEN_ROWS = 16
LOG2E = 1.4426950408889634
VMEM_LIMIT = 56 * 1024 * 1024

F32 = jnp.float32
BF16 = jnp.bfloat16


def _rms(x, g):
    ms = jnp.mean(x * x, axis=-1, keepdims=True)
    return x * lax.rsqrt(ms + RMS_EPS) * g


def _alibi_slopes(n):
    return [2.0 ** (-8.0 * (i + 1) / n) for i in range(n)]


def _const_spec(shape):
    return pl.BlockSpec(shape, lambda *_: (0,) * len(shape), pipeline_mode=pl.Buffered(1))


def _params(*sem):
    return pltpu.CompilerParams(dimension_semantics=sem, vmem_limit_bytes=VMEM_LIMIT)


def _grid_transpose_perm(n, n_major, n_minor):
    row = lax.broadcasted_iota(jnp.int32, (n, n), 0)
    col = lax.broadcasted_iota(jnp.int32, (n, n), 1)
    src = (row % n_minor) * n_major + row // n_minor
    return jnp.where(col == src, 1.0, 0.0).astype(BF16)


def _permute_rows(perm, x):
    return jnp.dot(perm, x, preferred_element_type=F32).astype(BF16)


def _row_chunk_spec(a2d, n_steps):
    rows, rem = divmod(a2d.shape[0], n_steps)
    assert rem == 0 and rows % 16 == 0
    return pl.BlockSpec((rows, a2d.shape[1]), lambda i: (i, 0))


def _norm_matmul_kernel(x_ref, g_ref, w_ref, *rest):
    n_casts = (len(rest) - 1) // 2
    o_ref = rest[n_casts]
    h = _rms(x_ref[...], g_ref[...]).astype(BF16)
    o_ref[...] = _pack_rows(jnp.dot(h, w_ref[...], preferred_element_type=F32).astype(BF16))
    for src_ref, dst_ref in zip(rest[:n_casts], rest[n_casts + 1:]):
        dst_ref[...] = src_ref[...].astype(dst_ref.dtype)


def _norm_matmul(x2d, g, w, tm, casts=()):
    t, d = x2d.shape
    n = w.shape[1]
    n_steps = t // tm
    cast_specs = [_row_chunk_spec(a, n_steps) for a in casts]
    return pl.pallas_call(
        _norm_matmul_kernel,
        grid=(n_steps,),
        in_specs=[pl.BlockSpec((tm, d), lambda i: (i, 0)), _const_spec((1, d)), _const_spec((d, n))] + cast_specs,
        out_specs=[pl.BlockSpec((tm // 2, n), lambda i: (i, 0))] + cast_specs,
        out_shape=[jax.ShapeDtypeStruct((t // 2, n), jnp.int32)]
        + [jax.ShapeDtypeStruct(a.shape, BF16) for a in casts],
        compiler_params=_params("parallel"),
    )(x2d, g.reshape(1, d), w, *casts)


def _pack_rows(x):
    return pltpu.bitcast(x, jnp.int32)


def _unpack_rows(x):
    return pltpu.bitcast(x, BF16)


def _norm_matmul_dilated_kernel(x_ref, g_ref, w_ref, *o_refs, dils):
    h = _rms(x_ref[...], g_ref[...]).astype(BF16)
    n = o_refs[0].shape[-1]
    for gi, (dil, o_ref) in enumerate(zip(dils, o_refs)):
        hg = h if dil == 1 else _to_classes(h, dil)
        y = jnp.dot(hg, w_ref[:, gi * n:(gi + 1) * n], preferred_element_type=F32).astype(BF16)
        o_ref[0] = _pack_rows(y).reshape(o_ref.shape[1:])


def _to_classes(h, dil):
    per = REGROUP // dil
    perm = _grid_transpose_perm(REGROUP, dil, per)
    parts = [_permute_rows(perm, h[b * REGROUP:(b + 1) * REGROUP]) for b in range(SPAN // REGROUP)]
    return jnp.concatenate([p[r * per:(r + 1) * per] for r in range(dil) for p in parts], axis=0)


def _norm_matmul_dilated(x, g, w, dils):
    b, s, d = x.shape
    n = w.shape[1] // len(dils)
    out_specs = [pl.BlockSpec((1, dil, SPAN // dil // 2, n), lambda bi, i: (bi, 0, i, 0)) for dil in dils]
    out_shape = [jax.ShapeDtypeStruct((b, dil, s // dil // 2, n), jnp.int32) for dil in dils]
    return pl.pallas_call(
        functools.partial(_norm_matmul_dilated_kernel, dils=dils),
        grid=(b, s // SPAN),
        in_specs=[pl.BlockSpec((None, SPAN, d), lambda bi, i: (bi, i, 0)),
                  _const_spec((1, d)), _const_spec(w.shape)],
        out_specs=out_specs,
        out_shape=out_shape,
        compiler_params=_params("parallel", "parallel"),
    )(x, g.reshape(1, d), w)


def _lane_group(shape):
    return lax.broadcasted_iota(jnp.int32, shape, 1) // HEAD_DIM


def _fill_bias(bias_ref, heads, *, tk, window, unit, offsets):
    slopes = _alibi_slopes(N_HEADS)
    key = lax.broadcasted_iota(jnp.int32, (tk, TQ), 0)
    qry = lax.broadcasted_iota(jnp.int32, (tk, TQ), 1)
    for v, off in enumerate(offsets):
        dist = jnp.abs(key - qry - off)
        negd = jnp.where(dist <= window, -(dist.astype(F32) * float(unit)), NEG)
        for c in range(N_HEADS // heads):
            for j in range(heads):
                bias_ref[v, c, :, j * TQ:(j + 1) * TQ] = (slopes[heads * c + j] * LOG2E) * negd


def _stack_heads(tiles, groups):
    lg = _lane_group(tiles[0].shape)
    return jnp.concatenate(
        [jnp.where(lg == g, t, jnp.zeros_like(t)) for t, g in zip(tiles, groups)], axis=0)


def _scores(q_stack, k, bias):
    return lax.dot_general(k, q_stack, (((1,), (1,)), ((), ())), preferred_element_type=F32) + bias


def _with_ones(vt):
    return jnp.concatenate([vt, jnp.ones((DEN_ROWS, vt.shape[1]), vt.dtype)], axis=0)


def _softmax_pv(sc, vt1):
    m = jnp.max(sc, axis=0, keepdims=True)
    p = jnp.exp2(sc - m)
    pv = jnp.dot(vt1, p.astype(BF16), preferred_element_type=F32)
    rows = vt1.shape[0] - DEN_ROWS
    return pv[:rows], m, pv[rows:rows + 1]


def _pipeline(stages, start_fn, finish_fn, depth):
    started = [start_fn(st) for st in stages[:depth]]
    for k, st in enumerate(stages):
        cur = started.pop(0)
        if k + depth < len(stages):
            started.append(start_fn(stages[k + depth]))
        finish_fn(st, cur)


def _window_start(q0, window, tk, seq_len):
    start = jnp.clip(q0 - window, 0, seq_len - tk)
    return pl.multiple_of(start, min(window, TQ)), (q0 - start) // window


def _split_f32(x):
    parts = []
    for _ in range(LSE_PARTS):
        p = x.astype(BF16)
        parts.append(p)
        x = x - p.astype(F32)
    return parts


def _attn_b_tiles(qkv_ref, tiles, bias_ref, out_refs, *, tk, window, seq_len):
    dq = N_HEADS * HEAD_DIM
    head_row = lax.broadcasted_iota(jnp.int32, (N_HEADS, TQ), 0)
    windows = [_window_start(q0, window, tk, seq_len) for _, _, q0, _ in tiles]
    acc = {}

    def start(stage):
        t, c = stage
        idx, r0, _, _ = tiles[t]
        win0, var = windows[t]
        lo = c * B_COLS
        q_rows = pl.ds(pl.multiple_of(r0 // 2, TQ // 2), TQ // 2)
        kv_rows = pl.ds(pl.multiple_of(win0 // 2, min(window, TQ) // 2), tk // 2)
        q = _unpack_rows(qkv_ref[idx + (q_rows, slice(lo, lo + B_COLS))])
        k = _unpack_rows(qkv_ref[idx + (kv_rows, slice(dq + lo, dq + lo + B_COLS))])
        v = _unpack_rows(qkv_ref[idx + (kv_rows, slice(2 * dq + lo, 2 * dq + lo + B_COLS))])
        q_stack = _stack_heads([q] * B_HEADS, range(B_HEADS))
        return _scores(q_stack, k, bias_ref[var, c]), _with_ones(v.T)

    def finish(stage, started):
        t, c = stage
        pv, m, den = _softmax_pv(*started)
        rden = 1.0 / den
        lse = m + jnp.log2(den)
        outs, lse16 = acc.get(t, ([], jnp.zeros((N_HEADS, TQ), F32)))
        for j in range(B_HEADS):
            cols = slice(j * TQ, (j + 1) * TQ)
            outs.append((pv[j * HEAD_DIM:(j + 1) * HEAD_DIM, cols] * rden[:, cols]).astype(BF16))
            lse16 = jnp.where(head_row == B_HEADS * c + j, lse[:, cols], lse16)
        acc[t] = (outs, lse16)
        if c == B_STAGES - 1:
            pad = jnp.zeros((LANES - LSE_PARTS * N_HEADS, TQ), BF16)
            tile_t = jnp.concatenate(outs + _split_f32(lse16) + [pad], axis=0)
            rows, lo = pl.ds(tiles[t][3], TQ), 0
            tile = tile_t.T
            for ref in out_refs:
                ref[rows, :] = tile[:, lo:lo + ref.shape[1]]
                lo += ref.shape[1]

    _pipeline([(t, c) for t in range(len(tiles)) for c in range(B_STAGES)], start, finish, B_DEPTH)


def _attn_b_kernel(qkv_ref, o_ref, lse_ref, bias_ref, *stage, seq_len, tk, window, unit, offsets, dil):
    @pl.when(pl.program_id(0) == 0)
    def _():
        _fill_bias(bias_ref, B_HEADS, tk=tk, window=window, unit=unit, offsets=offsets)

    dq = N_HEADS * HEAD_DIM
    n_tiles = dil * seq_len // TQ
    out_refs = (o_ref, lse_ref) if dil == 1 else stage

    def tile_group(g, carry):
        tiles = []
        for u in range(B_TILES):
            t = g * B_TILES + u
            s0 = pl.multiple_of(t * TQ, TQ)
            if dil == 1:
                tiles.append(((), s0, s0, s0))
            else:
                q0 = pl.multiple_of((t // dil) * TQ, TQ)
                tiles.append(((t % dil,), q0, q0, s0))
        _attn_b_tiles(qkv_ref, tiles, bias_ref, out_refs, tk=tk, window=window, seq_len=seq_len)
        return carry

    lax.fori_loop(0, n_tiles // B_TILES, tile_group, 0)

    if dil > 1:
        (stage_ref,) = stage
        per_cls = REGROUP // dil
        perm = _grid_transpose_perm(REGROUP, per_cls, dil)
        for k in range(n_tiles * TQ // REGROUP):
            span, off = divmod(k * per_cls, TQ)
            rows = [stage_ref[(span * dil + r) * TQ + off:(span * dil + r) * TQ + off + per_cls, :]
                    for r in range(dil)]
            nat = _permute_rows(perm, jnp.concatenate(rows, axis=0))
            o_ref[k * REGROUP:(k + 1) * REGROUP, :] = nat[:, :dq]
            lse_ref[k * REGROUP:(k + 1) * REGROUP, :] = nat[:, dq:]


def _attention_b(qkv, *, dil, window):
    batch, _, packed_len, width = qkv.shape
    seq_len = 2 * packed_len
    dq = N_HEADS * HEAD_DIM
    seq = seq_len * dil
    tk = min(TQ + 2 * window, seq_len)
    offsets = (0,) if tk == seq_len else (0, window, 2 * window)
    if dil == 1:
        in_spec = pl.BlockSpec((None, None, packed_len, width), lambda b: (b, 0, 0, 0))
    else:
        in_spec = pl.BlockSpec((None, dil, packed_len, width), lambda b: (b, 0, 0, 0))
    kern = functools.partial(_attn_b_kernel, seq_len=seq_len, tk=tk, window=window, unit=dil,
                             offsets=offsets, dil=dil)
    return pl.pallas_call(
        kern,
        grid=(batch,),
        in_specs=[in_spec],
        out_specs=[pl.BlockSpec((seq, dq), lambda b: (b, 0)), pl.BlockSpec((seq, LANES), lambda b: (b, 0))],
        out_shape=[jax.ShapeDtypeStruct((batch * seq, dq), BF16),
                   jax.ShapeDtypeStruct((batch * seq, LANES), BF16)],
        scratch_shapes=[pltpu.VMEM((len(offsets), B_STAGES, tk, B_HEADS * TQ), F32)]
        + ([pltpu.VMEM((seq, dq + LANES), BF16)] if dil > 1 else []),
        compiler_params=_params("arbitrary"),
    )(qkv)


def _attn_a_kernel(sink_ref, q_ref, k_ref, v_ref, o_ref, bias_ref, *, seq_len, tk, window, offsets):
    @pl.when(pl.program_id(0) == 0)
    def _():
        _fill_bias(bias_ref, HEADS_PER_TILE, tk=tk, window=window, unit=1, offsets=offsets)

    lax.fori_loop(0, seq_len // (A_TILES * TQ),
                  functools.partial(_attn_a_tile_group, sink_ref, q_ref, k_ref, v_ref, o_ref, bias_ref,
                                    seq_len=seq_len, tk=tk, window=window), 0)


def _attn_a_tile_group(sink_ref, q_ref, k_ref, v_ref, o_ref, bias_ref, g, carry, *, seq_len, tk, window):
    qblk = lax.broadcasted_iota(jnp.int32, (1, A_HEADS * TQ), 1) // TQ
    rows = [pl.multiple_of((g * A_TILES + t) * TQ, TQ) for t in range(A_TILES)]
    tiles = [_window_start(q0, window, tk, seq_len) for q0 in rows]
    shared = {}
    blocks = {}

    def start(stage):
        t, c, half = stage
        win0, var = tiles[t]
        if (c, half) == (0, 0):
            q_rows = pl.ds(pl.multiple_of(rows[t] // 2, TQ // 2), TQ // 2)
            kv_rows = pl.ds(pl.multiple_of(win0 // 2, min(window, TQ) // 2), tk // 2)
            qs = [_unpack_rows(q_ref[q_rows, j * MXU_DIM:(j + 1) * MXU_DIM]) for j in range(HEADS_PER_TILE)]
            shared[t] = (qs, _unpack_rows(k_ref[kv_rows, :]), _unpack_rows(v_ref[kv_rows, :]).T)
        qs, k4, vt = shared[t]
        vt_c = _with_ones(vt[c * HEAD_DIM:(c + 1) * HEAD_DIM])
        q_stack = _stack_heads(qs[half * A_HEADS:(half + 1) * A_HEADS], [c] * A_HEADS)
        bias = bias_ref[var, c, :, half * A_HEADS * TQ:(half + 1) * A_HEADS * TQ]
        return _scores(q_stack, k4, bias), vt_c

    def finish(stage, started):
        t, c, half = stage
        pv, m, den = _softmax_pv(*started)
        sink = jnp.zeros_like(m)
        for jj in range(A_HEADS):
            sink = jnp.where(qblk == jj, sink_ref[HEADS_PER_TILE * c + half * A_HEADS + jj] * LOG2E, sink)
        gate = 1.0 / (den + jnp.exp2(sink - m))
        for jj in range(A_HEADS):
            cols = slice(jj * TQ, (jj + 1) * TQ)
            blocks[t, half * A_HEADS + jj, c] = (pv[:, cols] * gate[:, cols]).astype(o_ref.dtype)
        if (c, half) == (KV_A - 1, HEADS_PER_TILE // A_HEADS - 1):
            tile_t = jnp.concatenate(
                [blocks.pop((t, j, cc)) for j in range(HEADS_PER_TILE) for cc in range(KV_A)], axis=0)
            o_ref[pl.ds(rows[t], TQ), :] = tile_t.T

    _pipeline([(t, c, half) for t in range(A_TILES) for c in range(KV_A)
               for half in range(HEADS_PER_TILE // A_HEADS)], start, finish, A_DEPTH)
    return carry


def _attention_a(qkv, sink):
    b, packed_s, _ = qkv.shape
    s = 2 * packed_s
    dq = N_HEADS * HEAD_DIM
    dkv = KV_A * HEAD_DIM
    tk = TQ + 2 * WINDOW_A
    offsets = (0, WINDOW_A, 2 * WINDOW_A)
    kern = functools.partial(_attn_a_kernel, seq_len=s, tk=tk, window=WINDOW_A, offsets=offsets)
    return pl.pallas_call(
        kern,
        grid=(b,),
        in_specs=[pl.BlockSpec(memory_space=pltpu.SMEM),
                  pl.BlockSpec((None, packed_s, dq), lambda bi: (bi, 0, 0)),
                  pl.BlockSpec((None, packed_s, dkv), lambda bi: (bi, 0, dq // dkv)),
                  pl.BlockSpec((None, packed_s, dkv), lambda bi: (bi, 0, dq // dkv + 1))],
        out_specs=pl.BlockSpec((None, s, dq), lambda bi: (bi, 0, 0)),
        out_shape=jax.ShapeDtypeStruct((b, s, dq), BF16),
        scratch_shapes=[pltpu.VMEM((len(offsets), KV_A, tk, HEADS_PER_TILE * TQ), F32)],
        compiler_params=_params("arbitrary"),
    )(sink, qkv, qkv, qkv)


def _merge_groups(o_refs, lse_refs):
    src = lax.broadcasted_iota(jnp.int32, (LANES, LANES), 0)
    dst = lax.broadcasted_iota(jnp.int32, (LANES, LANES), 1)
    fold = jnp.where((src % N_HEADS == dst % N_HEADS) & (src < LSE_PARTS * N_HEADS), 1.0, 0.0).astype(BF16)
    src = lax.broadcasted_iota(jnp.int32, (LANES, D_MODEL), 0)
    dst = lax.broadcasted_iota(jnp.int32, (LANES, D_MODEL), 1)
    expand = jnp.where((src % N_HEADS == dst // HEAD_DIM) & (src < LSE_PARTS * N_HEADS), 1.0, 0.0).astype(BF16)
    lses = [jnp.dot(r[...], fold, preferred_element_type=F32) for r in lse_refs]
    mx = functools.reduce(jnp.maximum, lses)
    es = [jnp.exp2(l - mx) for l in lses]
    den = functools.reduce(lambda a, b: a + b, es)
    piece_of_lane = lax.broadcasted_iota(jnp.int32, es[0].shape, 1) // N_HEADS
    o = None
    w_sum = None
    for e, o_ref in zip(es[:-1], o_refs[:-1]):
        packed = jnp.zeros(e.shape, BF16)
        for k, piece in enumerate(_split_f32(e / den)):
            packed = jnp.where(piece_of_lane == k, piece, packed)
        w = jnp.dot(packed, expand, preferred_element_type=F32)
        w_sum = w if w_sum is None else w_sum + w
        o = w * o_ref[...].astype(F32) if o is None else o + w * o_ref[...].astype(F32)
    o = o + (1.0 - w_sum) * o_refs[-1][...].astype(F32)
    return o.astype(BF16)


def _post_kernel(*refs, n_groups, final, ff_chunks, side_cast):
    refs = list(refs)
    x_ref = refs.pop(0)
    o_refs = [refs.pop(0) for _ in range(n_groups)]
    lse_refs = [refs.pop(0) for _ in range(n_groups)] if n_groups > 1 else []
    wo_ref, gf_ref, wg_ref, wu_ref, wd_ref = refs[:5]
    refs = refs[5:]
    gfin_ref = refs.pop(0) if final else None
    if side_cast:
        src_ref, scale_ref, out_ref, dst_ref = refs
        dst_ref[...] = (src_ref[...] * scale_ref[...]).astype(dst_ref.dtype)
    else:
        (out_ref,) = refs

    o = _merge_groups(o_refs, lse_refs) if n_groups > 1 else o_refs[0][...]
    x1 = x_ref[...] + jnp.dot(o, wo_ref[...], preferred_element_type=F32)
    h = _rms(x1, gf_ref[...]).astype(BF16)
    acc = x1
    lo = 0
    for fc in ff_chunks:
        gate = jnp.dot(h, wg_ref[:, lo:lo + fc], preferred_element_type=F32)
        up = jnp.dot(h, wu_ref[:, lo:lo + fc], preferred_element_type=F32)
        a = (gate / (1.0 + jnp.exp(-gate)) * up).astype(BF16)
        acc = acc + jnp.dot(a, wd_ref[lo:lo + fc, :], preferred_element_type=F32)
        lo += fc
    if final:
        acc = _rms(acc, gfin_ref[...])
    out_ref[...] = acc


def _ff_chunks(d_ff, n):
    tiles, rem = divmod(d_ff, MXU_DIM)
    assert rem == 0
    return tuple((tiles // n + (k < tiles % n)) * MXU_DIM for k in range(n))


def _layer_spec(shape, layer):
    return pl.BlockSpec((None,) + tuple(shape[1:]), lambda *_: (layer,) + (0,) * (len(shape) - 1),
                        pipeline_mode=pl.Buffered(1))


def _post(x2d, os, lses, wo, gf, wg, wu, wd, layer, gfin, tm, ff_chunks, side_cast=None):
    t, d = x2d.shape
    n_steps = t // tm
    ff_chunks = _ff_chunks(wg.shape[2], ff_chunks)
    final = gfin is not None
    row = lambda n: pl.BlockSpec((tm, n), lambda i: (i, 0))
    in_specs = [row(d)] * (1 + len(os)) + [row(LANES)] * len(lses)
    in_specs += [_const_spec((d, d))] + [_layer_spec(a.shape, layer) for a in (gf, wg, wu, wd)]
    args = [x2d, *os, *lses, wo, gf, wg, wu, wd]
    out_specs = [row(d)]
    out_shape = [jax.ShapeDtypeStruct((t, d), F32)]
    if final:
        in_specs.append(_const_spec((1, d)))
        args.append(gfin.reshape(1, d))
    if side_cast:
        w, scale = side_cast
        in_specs += [_row_chunk_spec(w, n_steps), _const_spec((1, w.shape[1]))]
        args += [w, scale.reshape(1, -1)]
        out_specs.append(_row_chunk_spec(w, n_steps))
        out_shape.append(jax.ShapeDtypeStruct(w.shape, BF16))
    return pl.pallas_call(
        functools.partial(_post_kernel, n_groups=len(os), final=final, ff_chunks=ff_chunks,
                          side_cast=bool(side_cast)),
        grid=(n_steps,),
        in_specs=in_specs,
        out_specs=out_specs,
        out_shape=out_shape,
        compiler_params=_params("parallel"),
    )(*args)


def _q_column_scale(n_cols, is_query_col):
    cols = np.arange(n_cols)
    return jnp.asarray(np.where(is_query_col(cols), HEAD_DIM ** -0.5 * LOG2E, 1.0), F32)


def kernel(x, norm_mix, norm_ffn, w_qkv_a, w_out_a, sink_a, w_qkv_b, w_out_b,
           w_gate, w_up, w_down, norm_final):
    b, s, d = x.shape
    t = b * s
    dq = N_HEADS * HEAD_DIM
    d_ff = w_gate.shape[2]

    wa = w_qkv_a[0] * _q_column_scale(w_qkv_a.shape[2], lambda c: c < dq)
    wq = wa[:, :dq].reshape(d, KV_A, HEADS_PER_TILE, HEAD_DIM).transpose(0, 2, 1, 3).reshape(d, dq)
    wa = jnp.concatenate([wq, wa[:, dq:]], axis=1).astype(BF16)
    wo_a = w_out_a[0].reshape(KV_A, HEADS_PER_TILE, HEAD_DIM, d).transpose(1, 0, 2, 3).reshape(dq, d)
    x2d = x.reshape(t, d)
    qkv, wg, wu, wd = _norm_matmul(
        x2d, norm_mix[0], wa, tm=2048,
        casts=(w_gate.reshape(-1, d_ff), w_up.reshape(-1, d_ff), w_down.reshape(-1, d)))
    ffn = (norm_ffn.reshape(-1, 1, d), wg.reshape(w_gate.shape), wu.reshape(w_up.shape), wd.reshape(w_down.shape))
    o_a = _attention_a(qkv.reshape(b, s // 2, -1), sink_a[0])
    wb_scale = _q_column_scale(w_qkv_b.shape[2], lambda c: c % (3 * dq) < dq)
    x2d, wb = _post(x2d, [o_a.reshape(t, d)], [], wo_a.astype(BF16), *ffn, 0, None, tm=512, ff_chunks=2,
                    side_cast=(w_qkv_b[0], wb_scale))

    dils = tuple(dil for _, dil in DILATED_GROUPS)
    qkvs = _norm_matmul_dilated(x2d.reshape(b, s, d), norm_mix[1], wb, dils)
    os, lses = [], []
    for qkv_g, (win, dil) in zip(qkvs, DILATED_GROUPS):
        o_g, lse_g = _attention_b(qkv_g, dil=dil, window=win // (2 * dil))
        os.append(o_g)
        lses.append(lse_g)
    (out,) = _post(x2d, os, lses, w_out_b[0].astype(BF16), *ffn, 1, norm_final, tm=512, ff_chunks=2)
    return out.reshape(b, s, d)
```

```python
import functools

import jax
import jax.numpy as jnp
import numpy as np
from jax import lax
from jax.experimental import pallas as pl
from jax.experimental.pallas import tpu as pltpu

D_MODEL = 1024
HEAD_DIM = 64
N_HEADS = 16
KV_A = 4
WINDOW_A = 128
DILATED_GROUPS = ((128, 1), (512, 4), (2048, 16))
RMS_EPS = 1e-6
NEG = -1e30
LANES = 128
MXU_DIM = 256
HEADS_PER_TILE = MXU_DIM // HEAD_DIM
N_TILES = N_HEADS // HEADS_PER_TILE
TQ = 128
A_TILES = 8
B_TILES = 16
A_DEPTH = 4
B_DEPTH = 6
A_HEADS = 4
B_HEADS = 2
B_COLS = B_HEADS * HEAD_DIM
B_STAGES = N_HEADS // B_HEADS
SPAN = 512
REGROUP = MXU_DIM
LSE_PARTS = 3
DEN_ROWS = 16
LOG2E = 1.4426950408889634
VMEM_LIMIT = 56 * 1024 * 1024

F32 = jnp.float32
BF16 = jnp.bfloat16


def _rms(x, g):
    ms = jnp.mean(x * x, axis=-1, keepdims=True)
    return x * lax.rsqrt(ms + RMS_EPS) * g


def _alibi_slopes(n):
    return [2.0 ** (-8.0 * (i + 1) / n) for i in range(n)]


def _const_spec(shape):
    return pl.BlockSpec(shape, lambda *_: (0,) * len(shape), pipeline_mode=pl.Buffered(1))


def _params(*sem):
    return pltpu.CompilerParams(dimension_semantics=sem, vmem_limit_bytes=VMEM_LIMIT)


def _grid_transpose_perm(n, n_major, n_minor):
    row = lax.broadcasted_iota(jnp.int32, (n, n), 0)
    col = lax.broadcasted_iota(jnp.int32, (n, n), 1)
    src = (row % n_minor) * n_major + row // n_minor
    return jnp.where(col == src, 1.0, 0.0).astype(BF16)


def _permute_rows(perm, x):
    return jnp.dot(perm, x, preferred_element_type=F32).astype(BF16)


def _row_chunk_spec(a2d, n_steps):
    rows, rem = divmod(a2d.shape[0], n_steps)
    assert rem == 0 and rows % 16 == 0
    return pl.BlockSpec((rows, a2d.shape[1]), lambda i: (i, 0))


def _norm_matmul_kernel(x_ref, g_ref, w_ref, *rest):
    n_casts = (len(rest) - 1) // 2
    o_ref = rest[n_casts]
    h = _rms(x_ref[...], g_ref[...]).astype(BF16)
    o_ref[...] = _pack_rows(jnp.dot(h, w_ref[...], preferred_element_type=F32).astype(BF16))
    for src_ref, dst_ref in zip(rest[:n_casts], rest[n_casts + 1:]):
        dst_ref[...] = src_ref[...].astype(dst_ref.dtype)


def _norm_matmul(x2d, g, w, tm, casts=()):
    t, d = x2d.shape
    n = w.shape[1]
    n_steps = t // tm
    cast_specs = [_row_chunk_spec(a, n_steps) for a in casts]
    return pl.pallas_call(
        _norm_matmul_kernel,
        grid=(n_steps,),
        in_specs=[pl.BlockSpec((tm, d), lambda i: (i, 0)), _const_spec((1, d)), _const_spec((d, n))] + cast_specs,
        out_specs=[pl.BlockSpec((tm // 2, n), lambda i: (i, 0))] + cast_specs,
        out_shape=[jax.ShapeDtypeStruct((t // 2, n), jnp.int32)]
        + [jax.ShapeDtypeStruct(a.shape, BF16) for a in casts],
        compiler_params=_params("parallel"),
    )(x2d, g.reshape(1, d), w, *casts)


def _pack_rows(x):
    return pltpu.bitcast(x, jnp.int32)


def _unpack_rows(x):
    return pltpu.bitcast(x, BF16)


def _norm_matmul_dilated_kernel(x_ref, g_ref, w_ref, *o_refs, dils):
    h = _rms(x_ref[...], g_ref[...]).astype(BF16)
    n = o_refs[0].shape[-1]
    for gi, (dil, o_ref) in enumerate(zip(dils, o_refs)):
        hg = h if dil == 1 else _to_classes(h, dil)
        y = jnp.dot(hg, w_ref[:, gi * n:(gi + 1) * n], preferred_element_type=F32).astype(BF16)
        o_ref[0] = _pack_rows(y).reshape(o_ref.shape[1:])


def _to_classes(h, dil):
    per = REGROUP // dil
    perm = _grid_transpose_perm(REGROUP, dil, per)
    parts = [_permute_rows(perm, h[b * REGROUP:(b + 1) * REGROUP]) for b in range(SPAN // REGROUP)]
    return jnp.concatenate([p[r * per:(r + 1) * per] for r in range(dil) for p in parts], axis=0)


def _norm_matmul_dilated(x, g, w, dils):
    b, s, d = x.shape
    n = w.shape[1] // len(dils)
    out_specs = [pl.BlockSpec((1, dil, SPAN // dil // 2, n), lambda bi, i: (bi, 0, i, 0)) for dil in dils]
    out_shape = [jax.ShapeDtypeStruct((b, dil, s // dil // 2, n), jnp.int32) for dil in dils]
    return pl.pallas_call(
        functools.partial(_norm_matmul_dilated_kernel, dils=dils),
        grid=(b, s // SPAN),
        in_specs=[pl.BlockSpec((None, SPAN, d), lambda bi, i: (bi, i, 0)),
                  _const_spec((1, d)), _const_spec(w.shape)],
        out_specs=out_specs,
        out_shape=out_shape,
        compiler_params=_params("parallel", "parallel"),
    )(x, g.reshape(1, d), w)


def _lane_group(shape):
    return lax.broadcasted_iota(jnp.int32, shape, 1) // HEAD_DIM


def _fill_bias(bias_ref, heads, *, tk, window, unit, offsets):
    slopes = _alibi_slopes(N_HEADS)
    key = lax.broadcasted_iota(jnp.int32, (tk, TQ), 0)
    qry = lax.broadcasted_iota(jnp.int32, (tk, TQ), 1)
    for v, off in enumerate(offsets):
        dist = jnp.abs(key - qry - off)
        negd = jnp.where(dist <= window, -(dist.astype(F32) * float(unit)), NEG)
        for c in range(N_HEADS // heads):
            for j in range(heads):
                bias_ref[v, c, :, j * TQ:(j + 1) * TQ] = (slopes[heads * c + j] * LOG2E) * negd


def _stack_heads(tiles, groups):
    lg = _lane_group(tiles[0].shape)
    return jnp.concatenate(
        [jnp.where(lg == g, t, jnp.zeros_like(t)) for t, g in zip(tiles, groups)], axis=0)


def _scores(q_stack, k, bias):
    return lax.dot_general(k, q_stack, (((1,), (1,)), ((), ())), preferred_element_type=F32) + bias


def _with_ones(vt):
    return jnp.concatenate([vt, jnp.ones((DEN_ROWS, vt.shape[1]), vt.dtype)], axis=0)


def _softmax_pv(sc, vt1):
    m = jnp.max(sc, axis=0, keepdims=True)
    p = jnp.exp2(sc - m)
    pv = jnp.dot(vt1, p.astype(BF16), preferred_element_type=F32)
    rows = vt1.shape[0] - DEN_ROWS
    return pv[:rows], m, pv[rows:rows + 1]


def _pipeline(stages, start_fn, finish_fn, depth):
    started = [start_fn(st) for st in stages[:depth]]
    for k, st in enumerate(stages):
        cur = started.pop(0)
        if k + depth < len(stages):
            started.append(start_fn(stages[k + depth]))
        finish_fn(st, cur)


def _window_start(q0, window, tk, seq_len):
    start = jnp.clip(q0 - window, 0, seq_len - tk)
    return pl.multiple_of(start, min(window, TQ)), (q0 - start) // window


def _split_f32(x):
    parts = []
    for _ in range(LSE_PARTS):
        p = x.astype(BF16)
        parts.append(p)
        x = x - p.astype(F32)
    return parts


def _attn_b_tiles(qkv_ref, tiles, bias_ref, out_refs, *, tk, window, seq_len):
    dq = N_HEADS * HEAD_DIM
    head_row = lax.broadcasted_iota(jnp.int32, (N_HEADS, TQ), 0)
    windows = [_window_start(q0, window, tk, seq_len) for _, _, q0, _ in tiles]
    acc = {}

    def start(stage):
        t, c = stage
        idx, r0, _, _ = tiles[t]
        win0, var = windows[t]
        lo = c * B_COLS
        q_rows = pl.ds(pl.multiple_of(r0 // 2, TQ // 2), TQ // 2)
        kv_rows = pl.ds(pl.multiple_of(win0 // 2, min(window, TQ) // 2), tk // 2)
        q = _unpack_rows(qkv_ref[idx + (q_rows, slice(lo, lo + B_COLS))])
        k = _unpack_rows(qkv_ref[idx + (kv_rows, slice(dq + lo, dq + lo + B_COLS))])
        v = _unpack_rows(qkv_ref[idx + (kv_rows, slice(2 * dq + lo, 2 * dq + lo + B_COLS))])
        q_stack = _stack_heads([q] * B_HEADS, range(B_HEADS))
        return _scores(q_stack, k, bias_ref[var, c]), _with_ones(v.T)

    def finish(stage, started):
        t, c = stage
        pv, m, den = _softmax_pv(*started)
        rden = 1.0 / den
        lse = m + jnp.log2(den)
        outs, lse16 = acc.get(t, ([], jnp.zeros((N_HEADS, TQ), F32)))
        for j in range(B_HEADS):
            cols = slice(j * TQ, (j + 1) * TQ)
            outs.append((pv[j * HEAD_DIM:(j + 1) * HEAD_DIM, cols] * rden[:, cols]).astype(BF16))
            lse16 = jnp.where(head_row == B_HEADS * c + j, lse[:, cols], lse16)
        acc[t] = (outs, lse16)
        if c == B_STAGES - 1:
            pad = jnp.zeros((LANES - LSE_PARTS * N_HEADS, TQ), BF16)
            tile_t = jnp.concatenate(outs + _split_f32(lse16) + [pad], axis=0)
            rows, lo = pl.ds(tiles[t][3], TQ), 0
            tile = tile_t.T
            for ref in out_refs:
                ref[rows, :] = tile[:, lo:lo + ref.shape[1]]
                lo += ref.shape[1]

    _pipeline([(t, c) for t in range(len(tiles)) for c in range(B_STAGES)], start, finish, B_DEPTH)


def _attn_b_kernel(qkv_ref, o_ref, lse_ref, bias_ref, *stage, seq_len, tk, window, unit, offsets, dil):
    @pl.when(pl.program_id(0) == 0)
    def _():
        _fill_bias(bias_ref, B_HEADS, tk=tk, window=window, unit=unit, offsets=offsets)

    dq = N_HEADS * HEAD_DIM
    n_tiles = dil * seq_len // TQ
    out_refs = (o_ref, lse_ref) if dil == 1 else stage

    def tile_group(g, carry):
        tiles = []
        for u in range(B_TILES):
            t = g * B_TILES + u
            s0 = pl.multiple_of(t * TQ, TQ)
            if dil == 1:
                tiles.append(((), s0, s0, s0))
            else:
                q0 = pl.multiple_of((t // dil) * TQ, TQ)
                tiles.append(((t % dil,), q0, q0, s0))
        _attn_b_tiles(qkv_ref, tiles, bias_ref, out_refs, tk=tk, window=window, seq_len=seq_len)
        return carry

    lax.fori_loop(0, n_tiles // B_TILES, tile_group, 0)

    if dil > 1:
        (stage_ref,) = stage
        per_cls = REGROUP // dil
        perm = _grid_transpose_perm(REGROUP, per_cls, dil)
        for k in range(n_tiles * TQ // REGROUP):
            span, off = divmod(k * per_cls, TQ)
            rows = [stage_ref[(span * dil + r) * TQ + off:(span * dil + r) * TQ + off + per_cls, :]
                    for r in range(dil)]
            nat = _permute_rows(perm, jnp.concatenate(rows, axis=0))
            o_ref[k * REGROUP:(k + 1) * REGROUP, :] = nat[:, :dq]
            lse_ref[k * REGROUP:(k + 1) * REGROUP, :] = nat[:, dq:]


def _attention_b(qkv, *, dil, window):
    batch, _, packed_len, width = qkv.shape
    seq_len = 2 * packed_len
    dq = N_HEADS * HEAD_DIM
    seq = seq_len * dil
    tk = min(TQ + 2 * window, seq_len)
    offsets = (0,) if tk == seq_len else (0, window, 2 * window)
    if dil == 1:
        in_spec = pl.BlockSpec((None, None, packed_len, width), lambda b: (b, 0, 0, 0))
    else:
        in_spec = pl.BlockSpec((None, dil, packed_len, width), lambda b: (b, 0, 0, 0))
    kern = functools.partial(_attn_b_kernel, seq_len=seq_len, tk=tk, window=window, unit=dil,
                             offsets=offsets, dil=dil)
    return pl.pallas_call(
        kern,
        grid=(batch,),
        in_specs=[in_spec],
        out_specs=[pl.BlockSpec((seq, dq), lambda b: (b, 0)), pl.BlockSpec((seq, LANES), lambda b: (b, 0))],
        out_shape=[jax.ShapeDtypeStruct((batch * seq, dq), BF16),
                   jax.ShapeDtypeStruct((batch * seq, LANES), BF16)],
        scratch_shapes=[pltpu.VMEM((len(offsets), B_STAGES, tk, B_HEADS * TQ), F32)]
        + ([pltpu.VMEM((seq, dq + LANES), BF16)] if dil > 1 else []),
        compiler_params=_params("arbitrary"),
    )(qkv)


def _attn_a_kernel(sink_ref, q_ref, k_ref, v_ref, o_ref, bias_ref, *, seq_len, tk, window, offsets):
    @pl.when(pl.program_id(0) == 0)
    def _():
        _fill_bias(bias_ref, HEADS_PER_TILE, tk=tk, window=window, unit=1, offsets=offsets)

    lax.fori_loop(0, seq_len // (A_TILES * TQ),
                  functools.partial(_attn_a_tile_group, sink_ref, q_ref, k_ref, v_ref, o_ref, bias_ref,
                                    seq_len=seq_len, tk=tk, window=window), 0)


def _attn_a_tile_group(sink_ref, q_ref, k_ref, v_ref, o_ref, bias_ref, g, carry, *, seq_len, tk, window):
    qblk = lax.broadcasted_iota(jnp.int32, (1, A_HEADS * TQ), 1) // TQ
    rows = [pl.multiple_of((g * A_TILES + t) * TQ, TQ) for t in range(A_TILES)]
    tiles = [_window_start(q0, window, tk, seq_len) for q0 in rows]
    shared = {}
    blocks = {}

    def start(stage):
        t, c, half = stage
        win0, var = tiles[t]
        if (c, half) == (0, 0):
            q_rows = pl.ds(pl.multiple_of(rows[t] // 2, TQ // 2), TQ // 2)
            kv_rows = pl.ds(pl.multiple_of(win0 // 2, min(window, TQ) // 2), tk // 2)
            qs = [_unpack_rows(q_ref[q_rows, j * MXU_DIM:(j + 1) * MXU_DIM]) for j in range(HEADS_PER_TILE)]
            shared[t] = (qs, _unpack_rows(k_ref[kv_rows, :]), _unpack_rows(v_ref[kv_rows, :]).T)
        qs, k4, vt = shared[t]
        vt_c = _with_ones(vt[c * HEAD_DIM:(c + 1) * HEAD_DIM])
        q_stack = _stack_heads(qs[half * A_HEADS:(half + 1) * A_HEADS], [c] * A_HEADS)
        bias = bias_ref[var, c, :, half * A_HEADS * TQ:(half + 1) * A_HEADS * TQ]
        return _scores(q_stack, k4, bias), vt_c

    def finish(stage, started):
        t, c, half = stage
        pv, m, den = _softmax_pv(*started)
        sink = jnp.zeros_like(m)
        for jj in range(A_HEADS):
            sink = jnp.where(qblk == jj, sink_ref[HEADS_PER_TILE * c + half * A_HEADS + jj] * LOG2E, sink)
        gate = 1.0 / (den + jnp.exp2(sink - m))
        for jj in range(A_HEADS):
            cols = slice(jj * TQ, (jj + 1) * TQ)
            blocks[t, half * A_HEADS + jj, c] = (pv[:, cols] * gate[:, cols]).astype(o_ref.dtype)
        if (c, half) == (KV_A - 1, HEADS_PER_TILE // A_HEADS - 1):
            tile_t = jnp.concatenate(
                [blocks.pop((t, j, cc)) for j in range(HEADS_PER_TILE) for cc in range(KV_A)], axis=0)
            o_ref[pl.ds(rows[t], TQ), :] = tile_t.T

    _pipeline([(t, c, half) for t in range(A_TILES) for c in range(KV_A)
               for half in range(HEADS_PER_TILE // A_HEADS)], start, finish, A_DEPTH)
    return carry


def _attention_a(qkv, sink):
    b, packed_s, _ = qkv.shape
    s = 2 * packed_s
    dq = N_HEADS * HEAD_DIM
    dkv = KV_A * HEAD_DIM
    tk = TQ + 2 * WINDOW_A
    offsets = (0, WINDOW_A, 2 * WINDOW_A)
    kern = functools.partial(_attn_a_kernel, seq_len=s, tk=tk, window=WINDOW_A, offsets=offsets)
    return pl.pallas_call(
        kern,
        grid=(b,),
        in_specs=[pl.BlockSpec(memory_space=pltpu.SMEM),
                  pl.BlockSpec((None, packed_s, dq), lambda bi: (bi, 0, 0)),
                  pl.BlockSpec((None, packed_s, dkv), lambda bi: (bi, 0, dq // dkv)),
                  pl.BlockSpec((None, packed_s, dkv), lambda bi: (bi, 0, dq // dkv + 1))],
        out_specs=pl.BlockSpec((None, s, dq), lambda bi: (bi, 0, 0)),
        out_shape=jax.ShapeDtypeStruct((b, s, dq), BF16),
        scratch_shapes=[pltpu.VMEM((len(offsets), KV_A, tk, HEADS_PER_TILE * TQ), F32)],
        compiler_params=_params("arbitrary"),
    )(sink, qkv, qkv, qkv)


def _merge_groups(o_refs, lse_refs):
    src = lax.broadcasted_iota(jnp.int32, (LANES, LANES), 0)
    dst = lax.broadcasted_iota(jnp.int32, (LANES, LANES), 1)
    fold = jnp.where((src % N_HEADS == dst % N_HEADS) & (src < LSE_PARTS * N_HEADS), 1.0, 0.0).astype(BF16)
    src = lax.broadcasted_iota(jnp.int32, (LANES, D_MODEL), 0)
    dst = lax.broadcasted_iota(jnp.int32, (LANES, D_MODEL), 1)
    expand = jnp.where((src % N_HEADS == dst // HEAD_DIM) & (src < LSE_PARTS * N_HEADS), 1.0, 0.0).astype(BF16)
    lses = [jnp.dot(r[...], fold, preferred_element_type=F32) for r in lse_refs]
    mx = functools.reduce(jnp.maximum, lses)
    es = [jnp.exp2(l - mx) for l in lses]
    den = functools.reduce(lambda a, b: a + b, es)
    piece_of_lane = lax.broadcasted_iota(jnp.int32, es[0].shape, 1) // N_HEADS
    o = None
    w_sum = None
    for e, o_ref in zip(es[:-1], o_refs[:-1]):
        packed = jnp.zeros(e.shape, BF16)
        for k, piece in enumerate(_split_f32(e / den)):
            packed = jnp.where(piece_of_lane == k, piece, packed)
        w = jnp.dot(packed, expand, preferred_element_type=F32)
        w_sum = w if w_sum is None else w_sum + w
        o = w * o_ref[...].astype(F32) if o is None else o + w * o_ref[...].astype(F32)
    o = o + (1.0 - w_sum) * o_refs[-1][...].astype(F32)
    return o.astype(BF16)


def _post_kernel(*refs, n_groups, final, ff_chunks, side_cast):
    refs = list(refs)
    x_ref = refs.pop(0)
    o_refs = [refs.pop(0) for _ in range(n_groups)]
    lse_refs = [refs.pop(0) for _ in range(n_groups)] if n_groups > 1 else []
    wo_ref, gf_ref, wg_ref, wu_ref, wd_ref = refs[:5]
    refs = refs[5:]
    gfin_ref = refs.pop(0) if final else None
    if side_cast:
        src_ref, scale_ref, out_ref, dst_ref = refs
        dst_ref[...] = (src_ref[...] * scale_ref[...]).astype(dst_ref.dtype)
    else:
        (out_ref,) = refs

    o = _merge_groups(o_refs, lse_refs) if n_groups > 1 else o_refs[0][...]
    x1 = x_ref[...] + jnp.dot(o, wo_ref[...], preferred_element_type=F32)
    h = (x1 * gf_ref[...]).astype(BF16)
    r = lax.rsqrt(jnp.mean(x1 * x1, axis=-1, keepdims=True) + RMS_EPS)
    acc = x1
    lo = 0
    for fc in ff_chunks:
        gate = r * jnp.dot(h, wg_ref[:, lo:lo + fc], preferred_element_type=F32)
        up = r * jnp.dot(h, wu_ref[:, lo:lo + fc], preferred_element_type=F32)
        a = (gate / (1.0 + jnp.exp(-gate)) * up).astype(BF16)
        acc = acc + jnp.dot(a, wd_ref[lo:lo + fc, :], preferred_element_type=F32)
        lo += fc
    if final:
        acc = _rms(acc, gfin_ref[...])
    out_ref[...] = acc


def _ff_chunks(d_ff, n):
    tiles, rem = divmod(d_ff, MXU_DIM)
    assert rem == 0
    return tuple((tiles // n + (k < tiles % n)) * MXU_DIM for k in range(n))


def _layer_spec(shape, layer):
    return pl.BlockSpec((None,) + tuple(shape[1:]), lambda *_: (layer,) + (0,) * (len(shape) - 1),
                        pipeline_mode=pl.Buffered(1))


def _post(x2d, os, lses, wo, gf, wg, wu, wd, layer, gfin, tm, ff_chunks, side_cast=None):
    t, d = x2d.shape
    n_steps = t // tm
    ff_chunks = _ff_chunks(wg.shape[2], ff_chunks)
    final = gfin is not None
    row = lambda n: pl.BlockSpec((tm, n), lambda i: (i, 0))
    in_specs = [row(d)] * (1 + len(os)) + [row(LANES)] * len(lses)
    in_specs += [_const_spec((d, d))] + [_layer_spec(a.shape, layer) for a in (gf, wg, wu, wd)]
    args = [x2d, *os, *lses, wo, gf, wg, wu, wd]
    out_specs = [row(d)]
    out_shape = [jax.ShapeDtypeStruct((t, d), F32)]
    if final:
        in_specs.append(_const_spec((1, d)))
        args.append(gfin.reshape(1, d))
    if side_cast:
        w, scale = side_cast
        in_specs += [_row_chunk_spec(w, n_steps), _const_spec((1, w.shape[1]))]
        args += [w, scale.reshape(1, -1)]
        out_specs.append(_row_chunk_spec(w, n_steps))
        out_shape.append(jax.ShapeDtypeStruct(w.shape, BF16))
    return pl.pallas_call(
        functools.partial(_post_kernel, n_groups=len(os), final=final, ff_chunks=ff_chunks,
                          side_cast=bool(side_cast)),
        grid=(n_steps,),
        in_specs=in_specs,
        out_specs=out_specs,
        out_shape=out_shape,
        compiler_params=_params("parallel"),
    )(*args)


def _q_column_scale(n_cols, is_query_col):
    cols = np.arange(n_cols)
    return jnp.asarray(np.where(is_query_col(cols), HEAD_DIM ** -0.5 * LOG2E, 1.0), F32)


def kernel(x, norm_mix, norm_ffn, w_qkv_a, w_out_a, sink_a, w_qkv_b, w_out_b,
           w_gate, w_up, w_down, norm_final):
    b, s, d = x.shape
    t = b * s
    dq = N_HEADS * HEAD_DIM
    d_ff = w_gate.shape[2]

    wa = w_qkv_a[0] * _q_column_scale(w_qkv_a.shape[2], lambda c: c < dq)
    wq = wa[:, :dq].reshape(d, KV_A, HEADS_PER_TILE, HEAD_DIM).transpose(0, 2, 1, 3).reshape(d, dq)
    wa = jnp.concatenate([wq, wa[:, dq:]], axis=1).astype(BF16)
    wo_a = w_out_a[0].reshape(KV_A, HEADS_PER_TILE, HEAD_DIM, d).transpose(1, 0, 2, 3).reshape(dq, d)
    x2d = x.reshape(t, d)
    qkv, wg, wu, wd = _norm_matmul(
        x2d, norm_mix[0], wa, tm=2048,
        casts=(w_gate.reshape(-1, d_ff), w_up.reshape(-1, d_ff), w_down.reshape(-1, d)))
    ffn = (norm_ffn.reshape(-1, 1, d), wg.reshape(w_gate.shape), wu.reshape(w_up.shape), wd.reshape(w_down.shape))
    o_a = _attention_a(qkv.reshape(b, s // 2, -1), sink_a[0])
    wb_scale = _q_column_scale(w_qkv_b.shape[2], lambda c: c % (3 * dq) < dq)
    x2d, wb = _post(x2d, [o_a.reshape(t, d)], [], wo_a.astype(BF16), *ffn, 0, None, tm=512, ff_chunks=2,
                    side_cast=(w_qkv_b[0], wb_scale))

    dils = tuple(dil for _, dil in DILATED_GROUPS)
    qkvs = _norm_matmul_dilated(x2d.reshape(b, s, d), norm_mix[1], wb, dils)
    os, lses = [], []
    for qkv_g, (win, dil) in zip(qkvs, DILATED_GROUPS):
        o_g, lse_g = _attention_b(qkv_g, dil=dil, window=win // (2 * dil))
        os.append(o_g)
        lses.append(lse_g)
    (out,) = _post(x2d, os, lses, w_out_b[0].astype(BF16), *ffn, 1, norm_final, tm=512, ff_chunks=2)
    return out.reshape(b, s, d)
```

```python
import functools

import jax
import jax.numpy as jnp
import numpy as np
from jax import lax
from jax.experimental import pallas as pl
from jax.experimental.pallas import tpu as pltpu

D_MODEL = 1024
HEAD_DIM = 64
N_HEADS = 16
KV_A = 4
WINDOW_A = 128
DILATED_GROUPS = ((128, 1), (512, 4), (2048, 16))
RMS_EPS = 1e-6
NEG = -1e30
LANES = 128
MXU_DIM = 256
HEADS_PER_TILE = MXU_DIM // HEAD_DIM
TQ = 128
A_TILES = 8
B_TILES = 16
A_DEPTH = 4
B_DEPTH = 6
A_HEADS = 4
B_HEADS = 2
B_COLS = B_HEADS * HEAD_DIM
B_STAGES = N_HEADS // B_HEADS
SPAN = 512
REGROUP = MXU_DIM
LSE_PARTS = 3
BF16_TILE_ROWS = 16
DEN_ROWS = BF16_TILE_ROWS
LOG2E = 1.4426950408889634
VMEM_LIMIT = 56 * 1024 * 1024

F32 = jnp.float32
BF16 = jnp.bfloat16


def _rms(x, g):
    ms = jnp.mean(x * x, axis=-1, keepdims=True)
    return x * lax.rsqrt(ms + RMS_EPS) * g


def _alibi_slopes(n):
    return [2.0 ** (-8.0 * (i + 1) / n) for i in range(n)]


def _const_spec(shape):
    return pl.BlockSpec(shape, lambda *_: (0,) * len(shape), pipeline_mode=pl.Buffered(1))


def _params(*sem):
    return pltpu.CompilerParams(dimension_semantics=sem, vmem_limit_bytes=VMEM_LIMIT)


def _grid_transpose_perm(n, n_major, n_minor):
    row = lax.broadcasted_iota(jnp.int32, (n, n), 0)
    col = lax.broadcasted_iota(jnp.int32, (n, n), 1)
    src = (row % n_minor) * n_major + row // n_minor
    return jnp.where(col == src, 1.0, 0.0).astype(BF16)


def _permute_rows(perm, x):
    return jnp.dot(perm, x, preferred_element_type=F32).astype(BF16)


def _pack_rows(x):
    return pltpu.bitcast(x, jnp.int32)


def _unpack_rows(x):
    return pltpu.bitcast(x, BF16)


def _row_chunk_spec(a2d, n_steps):
    rows, rem = divmod(a2d.shape[0], n_steps)
    assert rem == 0 and rows % BF16_TILE_ROWS == 0
    return pl.BlockSpec((rows, a2d.shape[1]), lambda i: (i, 0))


def _norm_matmul_kernel(x_ref, g_ref, w_ref, *rest):
    n_casts = (len(rest) - 1) // 2
    o_ref = rest[n_casts]
    h = _rms(x_ref[...], g_ref[...]).astype(BF16)
    o_ref[...] = _pack_rows(jnp.dot(h, w_ref[...], preferred_element_type=F32).astype(BF16))
    for src_ref, dst_ref in zip(rest[:n_casts], rest[n_casts + 1:]):
        dst_ref[...] = src_ref[...].astype(dst_ref.dtype)


def _norm_matmul(x2d, g, w, tm, casts=()):
    t, d = x2d.shape
    n = w.shape[1]
    n_steps = t // tm
    cast_specs = [_row_chunk_spec(a, n_steps) for a in casts]
    return pl.pallas_call(
        _norm_matmul_kernel,
        grid=(n_steps,),
        in_specs=[pl.BlockSpec((tm, d), lambda i: (i, 0)), _const_spec((1, d)), _const_spec((d, n))] + cast_specs,
        out_specs=[pl.BlockSpec((tm // 2, n), lambda i: (i, 0))] + cast_specs,
        out_shape=[jax.ShapeDtypeStruct((t // 2, n), jnp.int32)]
        + [jax.ShapeDtypeStruct(a.shape, BF16) for a in casts],
        compiler_params=_params("parallel"),
    )(x2d, g.reshape(1, d), w, *casts)


def _norm_matmul_dilated_kernel(x_ref, g_ref, w_ref, *o_refs, dils):
    h = _rms(x_ref[...], g_ref[...]).astype(BF16)
    n = o_refs[0].shape[-1]
    for gi, (dil, o_ref) in enumerate(zip(dils, o_refs)):
        hg = h if dil == 1 else _to_classes(h, dil)
        y = jnp.dot(hg, w_ref[:, gi * n:(gi + 1) * n], preferred_element_type=F32).astype(BF16)
        o_ref[0] = _pack_rows(y).reshape(o_ref.shape[1:])


def _to_classes(h, dil):
    per = REGROUP // dil
    perm = _grid_transpose_perm(REGROUP, dil, per)
    parts = [_permute_rows(perm, h[b * REGROUP:(b + 1) * REGROUP]) for b in range(SPAN // REGROUP)]
    return jnp.concatenate([p[r * per:(r + 1) * per] for r in range(dil) for p in parts], axis=0)


def _norm_matmul_dilated(x, g, w, dils):
    b, s, d = x.shape
    n = w.shape[1] // len(dils)
    out_specs = [pl.BlockSpec((1, dil, SPAN // dil // 2, n), lambda bi, i: (bi, 0, i, 0)) for dil in dils]
    out_shape = [jax.ShapeDtypeStruct((b, dil, s // dil // 2, n), jnp.int32) for dil in dils]
    return pl.pallas_call(
        functools.partial(_norm_matmul_dilated_kernel, dils=dils),
        grid=(b, s // SPAN),
        in_specs=[pl.BlockSpec((None, SPAN, d), lambda bi, i: (bi, i, 0)),
                  _const_spec((1, d)), _const_spec(w.shape)],
        out_specs=out_specs,
        out_shape=out_shape,
        compiler_params=_params("parallel", "parallel"),
    )(x, g.reshape(1, d), w)


def _lane_group(shape):
    return lax.broadcasted_iota(jnp.int32, shape, 1) // HEAD_DIM


def _fill_bias(bias_ref, heads, *, tk, window, unit, offsets):
    slopes = _alibi_slopes(N_HEADS)
    key = lax.broadcasted_iota(jnp.int32, (tk, TQ), 0)
    qry = lax.broadcasted_iota(jnp.int32, (tk, TQ), 1)
    for v, off in enumerate(offsets):
        dist = jnp.abs(key - qry - off)
        negd = jnp.where(dist <= window, -(dist.astype(F32) * float(unit)), NEG)
        for c in range(N_HEADS // heads):
            for j in range(heads):
                bias_ref[v, c, :, j * TQ:(j + 1) * TQ] = (slopes[heads * c + j] * LOG2E) * negd


def _stack_heads(tiles, groups):
    lg = _lane_group(tiles[0].shape)
    return jnp.concatenate(
        [jnp.where(lg == g, t, jnp.zeros_like(t)) for t, g in zip(tiles, groups)], axis=0)


def _scores(q_stack, k, bias):
    return lax.dot_general(k, q_stack, (((1,), (1,)), ((), ())), preferred_element_type=F32) + bias


def _with_ones(vt):
    return jnp.concatenate([vt, jnp.ones((DEN_ROWS, vt.shape[1]), vt.dtype)], axis=0)


def _softmax_pv(sc, vt1):
    m = jnp.max(sc, axis=0, keepdims=True)
    p = jnp.exp2(sc - m)
    pv = jnp.dot(vt1, p.astype(BF16), preferred_element_type=F32)
    rows = vt1.shape[0] - DEN_ROWS
    return pv[:rows], m, pv[rows:rows + 1]


def _pipeline(stages, start_fn, finish_fn, depth):
    started = [start_fn(st) for st in stages[:depth]]
    for k, st in enumerate(stages):
        cur = started.pop(0)
        if k + depth < len(stages):
            started.append(start_fn(stages[k + depth]))
        finish_fn(st, cur)


def _window_start(q0, window, tk, seq_len):
    start = jnp.clip(q0 - window, 0, seq_len - tk)
    return pl.multiple_of(start, min(window, TQ)), (q0 - start) // window


def _split_f32(x):
    parts = []
    for _ in range(LSE_PARTS):
        p = x.astype(BF16)
        parts.append(p)
        x = x - p.astype(F32)
    return parts


def _attn_b_tiles(qkv_ref, tiles, bias_ref, out_refs, *, tk, window, seq_len):
    dq = N_HEADS * HEAD_DIM
    head_row = lax.broadcasted_iota(jnp.int32, (N_HEADS, TQ), 0)
    windows = [_window_start(q0, window, tk, seq_len) for _, _, q0, _ in tiles]
    acc = {}

    def start(stage):
        t, c = stage
        idx, r0, _, _ = tiles[t]
        win0, var = windows[t]
        lo = c * B_COLS
        q_rows = pl.ds(pl.multiple_of(r0 // 2, TQ // 2), TQ // 2)
        kv_rows = pl.ds(pl.multiple_of(win0 // 2, min(window, TQ) // 2), tk // 2)
        q = _unpack_rows(qkv_ref[idx + (q_rows, slice(lo, lo + B_COLS))])
        k = _unpack_rows(qkv_ref[idx + (kv_rows, slice(dq + lo, dq + lo + B_COLS))])
        v = _unpack_rows(qkv_ref[idx + (kv_rows, slice(2 * dq + lo, 2 * dq + lo + B_COLS))])
        q_stack = _stack_heads([q] * B_HEADS, range(B_HEADS))
        return _scores(q_stack, k, bias_ref[var, c]), _with_ones(v.T)

    def finish(stage, started):
        t, c = stage
        pv, m, den = _softmax_pv(*started)
        rden = 1.0 / den
        lse = m + jnp.log2(den)
        outs, lse16 = acc.get(t, ([], jnp.zeros((N_HEADS, TQ), F32)))
        for j in range(B_HEADS):
            cols = slice(j * TQ, (j + 1) * TQ)
            outs.append((pv[j * HEAD_DIM:(j + 1) * HEAD_DIM, cols] * rden[:, cols]).astype(BF16))
            lse16 = jnp.where(head_row == B_HEADS * c + j, lse[:, cols], lse16)
        acc[t] = (outs, lse16)
        if c == B_STAGES - 1:
            pad = jnp.zeros((LANES - LSE_PARTS * N_HEADS, TQ), BF16)
            tile_t = jnp.concatenate(outs + _split_f32(lse16) + [pad], axis=0)
            rows, lo = pl.ds(tiles[t][3], TQ), 0
            tile = tile_t.T
            for ref in out_refs:
                ref[rows, :] = tile[:, lo:lo + ref.shape[1]]
                lo += ref.shape[1]

    _pipeline([(t, c) for t in range(len(tiles)) for c in range(B_STAGES)], start, finish, B_DEPTH)


def _attn_b_kernel(qkv_ref, o_ref, lse_ref, bias_ref, *stage, seq_len, tk, window, unit, offsets, dil):
    @pl.when(pl.program_id(0) == 0)
    def _():
        _fill_bias(bias_ref, B_HEADS, tk=tk, window=window, unit=unit, offsets=offsets)

    dq = N_HEADS * HEAD_DIM
    n_tiles = dil * seq_len // TQ
    out_refs = (o_ref, lse_ref) if dil == 1 else stage

    def tile_group(g, carry):
        tiles = []
        for u in range(B_TILES):
            t = g * B_TILES + u
            s0 = pl.multiple_of(t * TQ, TQ)
            if dil == 1:
                tiles.append(((), s0, s0, s0))
            else:
                q0 = pl.multiple_of((t // dil) * TQ, TQ)
                tiles.append(((t % dil,), q0, q0, s0))
        _attn_b_tiles(qkv_ref, tiles, bias_ref, out_refs, tk=tk, window=window, seq_len=seq_len)
        return carry

    lax.fori_loop(0, n_tiles // B_TILES, tile_group, 0)

    if dil > 1:
        (stage_ref,) = stage
        per_cls = REGROUP // dil
        perm = _grid_transpose_perm(REGROUP, per_cls, dil)
        for k in range(n_tiles * TQ // REGROUP):
            span, off = divmod(k * per_cls, TQ)
            rows = [stage_ref[(span * dil + r) * TQ + off:(span * dil + r) * TQ + off + per_cls, :]
                    for r in range(dil)]
            nat = _permute_rows(perm, jnp.concatenate(rows, axis=0))
            o_ref[k * REGROUP:(k + 1) * REGROUP, :] = nat[:, :dq]
            lse_ref[k * REGROUP:(k + 1) * REGROUP, :] = nat[:, dq:]


def _attention_b(qkv, *, dil, window):
    batch, _, packed_len, width = qkv.shape
    seq_len = 2 * packed_len
    dq = N_HEADS * HEAD_DIM
    seq = seq_len * dil
    tk = min(TQ + 2 * window, seq_len)
    offsets = (0,) if tk == seq_len else (0, window, 2 * window)
    if dil == 1:
        in_spec = pl.BlockSpec((None, None, packed_len, width), lambda b: (b, 0, 0, 0))
    else:
        in_spec = pl.BlockSpec((None, dil, packed_len, width), lambda b: (b, 0, 0, 0))
    kern = functools.partial(_attn_b_kernel, seq_len=seq_len, tk=tk, window=window, unit=dil,
                             offsets=offsets, dil=dil)
    return pl.pallas_call(
        kern,
        grid=(batch,),
        in_specs=[in_spec],
        out_specs=[pl.BlockSpec((seq, dq), lambda b: (b, 0)), pl.BlockSpec((seq, LANES), lambda b: (b, 0))],
        out_shape=[jax.ShapeDtypeStruct((batch * seq, dq), BF16),
                   jax.ShapeDtypeStruct((batch * seq, LANES), BF16)],
        scratch_shapes=[pltpu.VMEM((len(offsets), B_STAGES, tk, B_HEADS * TQ), F32)]
        + ([pltpu.VMEM((seq, dq + LANES), BF16)] if dil > 1 else []),
        compiler_params=_params("arbitrary"),
    )(qkv)


def _attn_a_kernel(sink_ref, q_ref, k_ref, v_ref, o_ref, bias_ref, *, seq_len, tk, window, offsets):
    @pl.when(pl.program_id(0) == 0)
    def _():
        _fill_bias(bias_ref, HEADS_PER_TILE, tk=tk, window=window, unit=1, offsets=offsets)

    lax.fori_loop(0, seq_len // (A_TILES * TQ),
                  functools.partial(_attn_a_tile_group, sink_ref, q_ref, k_ref, v_ref, o_ref, bias_ref,
                                    seq_len=seq_len, tk=tk, window=window), 0)


def _attn_a_tile_group(sink_ref, q_ref, k_ref, v_ref, o_ref, bias_ref, g, carry, *, seq_len, tk, window):
    qblk = lax.broadcasted_iota(jnp.int32, (1, A_HEADS * TQ), 1) // TQ
    rows = [pl.multiple_of((g * A_TILES + t) * TQ, TQ) for t in range(A_TILES)]
    tiles = [_window_start(q0, window, tk, seq_len) for q0 in rows]
    shared = {}
    blocks = {}

    def start(stage):
        t, c, half = stage
        win0, var = tiles[t]
        if (c, half) == (0, 0):
            q_rows = pl.ds(pl.multiple_of(rows[t] // 2, TQ // 2), TQ // 2)
            kv_rows = pl.ds(pl.multiple_of(win0 // 2, min(window, TQ) // 2), tk // 2)
            qs = [_unpack_rows(q_ref[q_rows, j * MXU_DIM:(j + 1) * MXU_DIM]) for j in range(HEADS_PER_TILE)]
            shared[t] = (qs, _unpack_rows(k_ref[kv_rows, :]), _unpack_rows(v_ref[kv_rows, :]).T)
        qs, k4, vt = shared[t]
        vt_c = _with_ones(vt[c * HEAD_DIM:(c + 1) * HEAD_DIM])
        q_stack = _stack_heads(qs[half * A_HEADS:(half + 1) * A_HEADS], [c] * A_HEADS)
        bias = bias_ref[var, c, :, half * A_HEADS * TQ:(half + 1) * A_HEADS * TQ]
        return _scores(q_stack, k4, bias), vt_c

    def finish(stage, started):
        t, c, half = stage
        pv, m, den = _softmax_pv(*started)
        sink = jnp.zeros_like(m)
        for jj in range(A_HEADS):
            sink = jnp.where(qblk == jj, sink_ref[HEADS_PER_TILE * c + half * A_HEADS + jj] * LOG2E, sink)
        gate = 1.0 / (den + jnp.exp2(sink - m))
        for jj in range(A_HEADS):
            cols = slice(jj * TQ, (jj + 1) * TQ)
            blocks[t, half * A_HEADS + jj, c] = (pv[:, cols] * gate[:, cols]).astype(o_ref.dtype)
        if (c, half) == (KV_A - 1, HEADS_PER_TILE // A_HEADS - 1):
            tile_t = jnp.concatenate(
                [blocks.pop((t, j, cc)) for j in range(HEADS_PER_TILE) for cc in range(KV_A)], axis=0)
            o_ref[pl.ds(rows[t], TQ), :] = tile_t.T

    _pipeline([(t, c, half) for t in range(A_TILES) for c in range(KV_A)
               for half in range(HEADS_PER_TILE // A_HEADS)], start, finish, A_DEPTH)
    return carry


def _attention_a(qkv, sink):
    b, packed_s, _ = qkv.shape
    s = 2 * packed_s
    dq = N_HEADS * HEAD_DIM
    dkv = KV_A * HEAD_DIM
    tk = TQ + 2 * WINDOW_A
    offsets = (0, WINDOW_A, 2 * WINDOW_A)
    kern = functools.partial(_attn_a_kernel, seq_len=s, tk=tk, window=WINDOW_A, offsets=offsets)
    return pl.pallas_call(
        kern,
        grid=(b,),
        in_specs=[pl.BlockSpec(memory_space=pltpu.SMEM),
                  pl.BlockSpec((None, packed_s, dq), lambda bi: (bi, 0, 0)),
                  pl.BlockSpec((None, packed_s, dkv), lambda bi: (bi, 0, dq // dkv)),
                  pl.BlockSpec((None, packed_s, dkv), lambda bi: (bi, 0, dq // dkv + 1))],
        out_specs=pl.BlockSpec((None, s, dq), lambda bi: (bi, 0, 0)),
        out_shape=jax.ShapeDtypeStruct((b, s, dq), BF16),
        scratch_shapes=[pltpu.VMEM((len(offsets), KV_A, tk, HEADS_PER_TILE * TQ), F32)],
        compiler_params=_params("arbitrary"),
    )(sink, qkv, qkv, qkv)


def _merge_groups(o_refs, lse_refs):
    src = lax.broadcasted_iota(jnp.int32, (LANES, LANES), 0)
    dst = lax.broadcasted_iota(jnp.int32, (LANES, LANES), 1)
    fold = jnp.where((src % N_HEADS == dst % N_HEADS) & (src < LSE_PARTS * N_HEADS), 1.0, 0.0).astype(BF16)
    src = lax.broadcasted_iota(jnp.int32, (LANES, D_MODEL), 0)
    dst = lax.broadcasted_iota(jnp.int32, (LANES, D_MODEL), 1)
    expand = jnp.where((src % N_HEADS == dst // HEAD_DIM) & (src < LSE_PARTS * N_HEADS), 1.0, 0.0).astype(BF16)
    lses = [jnp.dot(r[...], fold, preferred_element_type=F32) for r in lse_refs]
    mx = functools.reduce(jnp.maximum, lses)
    es = [jnp.exp2(l - mx) for l in lses]
    den = functools.reduce(lambda a, b: a + b, es)
    piece_of_lane = lax.broadcasted_iota(jnp.int32, es[0].shape, 1) // N_HEADS
    o = None
    w_sum = None
    for e, o_ref in zip(es[:-1], o_refs[:-1]):
        packed = jnp.zeros(e.shape, BF16)
        for k, piece in enumerate(_split_f32(e / den)):
            packed = jnp.where(piece_of_lane == k, piece, packed)
        w = jnp.dot(packed, expand, preferred_element_type=F32)
        w_sum = w if w_sum is None else w_sum + w
        o = w * o_ref[...].astype(F32) if o is None else o + w * o_ref[...].astype(F32)
    o = o + (1.0 - w_sum) * o_refs[-1][...].astype(F32)
    return o.astype(BF16)


def _post_kernel(*refs, n_groups, final, ff_chunks, side_cast):
    refs = list(refs)
    x_ref = refs.pop(0)
    o_refs = [refs.pop(0) for _ in range(n_groups)]
    lse_refs = [refs.pop(0) for _ in range(n_groups)] if n_groups > 1 else []
    wo_ref, gf_ref, wg_ref, wu_ref, wd_ref = refs[:5]
    refs = refs[5:]
    gfin_ref = refs.pop(0) if final else None
    if side_cast:
        src_ref, scale_ref, out_ref, dst_ref = refs
        dst_ref[...] = (src_ref[...] * scale_ref[...]).astype(dst_ref.dtype)
    else:
        (out_ref,) = refs

    o = _merge_groups(o_refs, lse_refs) if n_groups > 1 else o_refs[0][...]
    x1 = x_ref[...] + jnp.dot(o, wo_ref[...], preferred_element_type=F32)
    h = (x1 * gf_ref[...]).astype(BF16)
    r = lax.rsqrt(jnp.mean(x1 * x1, axis=-1, keepdims=True) + RMS_EPS)
    acc = x1
    lo = 0
    for fc in ff_chunks:
        gate = r * jnp.dot(h, wg_ref[:, lo:lo + fc], preferred_element_type=F32)
        up = r * jnp.dot(h, wu_ref[:, lo:lo + fc], preferred_element_type=F32)
        a = (gate / (1.0 + jnp.exp(-gate)) * up).astype(BF16)
        acc = acc + jnp.dot(a, wd_ref[lo:lo + fc, :], preferred_element_type=F32)
        lo += fc
    if final:
        acc = _rms(acc, gfin_ref[...])
    out_ref[...] = acc


def _ff_chunks(d_ff, n):
    tiles, rem = divmod(d_ff, MXU_DIM)
    assert rem == 0
    return tuple((tiles // n + (k < tiles % n)) * MXU_DIM for k in range(n))


def _layer_spec(shape, layer):
    return pl.BlockSpec((None,) + tuple(shape[1:]), lambda *_: (layer,) + (0,) * (len(shape) - 1),
                        pipeline_mode=pl.Buffered(1))


def _post(x2d, os, lses, wo, gf, wg, wu, wd, layer, gfin, tm, ff_chunks, side_cast=None):
    t, d = x2d.shape
    n_steps = t // tm
    ff_chunks = _ff_chunks(wg.shape[2], ff_chunks)
    final = gfin is not None
    row = lambda n: pl.BlockSpec((tm, n), lambda i: (i, 0))
    in_specs = [row(d)] * (1 + len(os)) + [row(LANES)] * len(lses)
    in_specs += [_const_spec((d, d))] + [_layer_spec(a.shape, layer) for a in (gf, wg, wu, wd)]
    args = [x2d, *os, *lses, wo, gf, wg, wu, wd]
    out_specs = [row(d)]
    out_shape = [jax.ShapeDtypeStruct((t, d), F32)]
    if final:
        in_specs.append(_const_spec((1, d)))
        args.append(gfin.reshape(1, d))
    if side_cast:
        w, scale = side_cast
        in_specs += [_row_chunk_spec(w, n_steps), _const_spec((1, w.shape[1]))]
        args += [w, scale.reshape(1, -1)]
        out_specs.append(_row_chunk_spec(w, n_steps))
        out_shape.append(jax.ShapeDtypeStruct(w.shape, BF16))
    return pl.pallas_call(
        functools.partial(_post_kernel, n_groups=len(os), final=final, ff_chunks=ff_chunks,
                          side_cast=bool(side_cast)),
        grid=(n_steps,),
        in_specs=in_specs,
        out_specs=out_specs,
        out_shape=out_shape,
        compiler_params=_params("parallel"),
    )(*args)


def _q_column_scale(n_cols, is_query_col):
    cols = np.arange(n_cols)
    return jnp.asarray(np.where(is_query_col(cols), HEAD_DIM ** -0.5 * LOG2E, 1.0), F32)


def kernel(x, norm_mix, norm_ffn, w_qkv_a, w_out_a, sink_a, w_qkv_b, w_out_b,
           w_gate, w_up, w_down, norm_final):
    b, s, d = x.shape
    t = b * s
    dq = N_HEADS * HEAD_DIM
    d_ff = w_gate.shape[2]

    wa = w_qkv_a[0] * _q_column_scale(w_qkv_a.shape[2], lambda c: c < dq)
    wq = wa[:, :dq].reshape(d, KV_A, HEADS_PER_TILE, HEAD_DIM).transpose(0, 2, 1, 3).reshape(d, dq)
    wa = jnp.concatenate([wq, wa[:, dq:]], axis=1).astype(BF16)
    wo_a = w_out_a[0].reshape(KV_A, HEADS_PER_TILE, HEAD_DIM, d).transpose(1, 0, 2, 3).reshape(dq, d)
    x2d = x.reshape(t, d)
    qkv, wg, wu, wd = _norm_matmul(
        x2d, norm_mix[0], wa, tm=2048,
        casts=(w_gate.reshape(-1, d_ff), w_up.reshape(-1, d_ff), w_down.reshape(-1, d)))
    ffn = (norm_ffn.reshape(-1, 1, d), wg.reshape(w_gate.shape), wu.reshape(w_up.shape), wd.reshape(w_down.shape))
    o_a = _attention_a(qkv.reshape(b, s // 2, -1), sink_a[0])
    wb_scale = _q_column_scale(w_qkv_b.shape[2], lambda c: c % (3 * dq) < dq)
    x2d, wb = _post(x2d, [o_a.reshape(t, d)], [], wo_a.astype(BF16), *ffn, 0, None, tm=512, ff_chunks=2,
                    side_cast=(w_qkv_b[0], wb_scale))

    dils = tuple(dil for _, dil in DILATED_GROUPS)
    qkvs = _norm_matmul_dilated(x2d.reshape(b, s, d), norm_mix[1], wb, dils)
    os, lses = [], []
    for qkv_g, (win, dil) in zip(qkvs, DILATED_GROUPS):
        o_g, lse_g = _attention_b(qkv_g, dil=dil, window=win // (2 * dil))
        os.append(o_g)
        lses.append(lse_g)
    (out,) = _post(x2d, os, lses, w_out_b[0].astype(BF16), *ffn, 1, norm_final, tm=512, ff_chunks=2)
    return out.reshape(b, s, d)
```

```python
import functools

import jax
import jax.numpy as jnp
import numpy as np
from jax import lax
from jax.experimental import pallas as pl
from jax.experimental.pallas import tpu as pltpu

D_MODEL = 1024
HEAD_DIM = 64
N_HEADS = 16
KV_A = 4
WINDOW_A = 128
DILATED_GROUPS = ((128, 1), (512, 4), (2048, 16))
RMS_EPS = 1e-6
NEG = -1e30
LANES = 128
MXU_DIM = 256
HEADS_PER_TILE = MXU_DIM // HEAD_DIM
TQ = 128
A_TILES = 8
B_TILES = 16
A_DEPTH = 4
B_DEPTH = 6
A_HEADS = 4
B_HEADS = 2
B_COLS = B_HEADS * HEAD_DIM
B_STAGES = N_HEADS // B_HEADS
SPAN = 512
REGROUP = MXU_DIM
LSE_PARTS = 3
BF16_TILE_ROWS = 16
DEN_ROWS = BF16_TILE_ROWS
LOG2E = 1.4426950408889634
VMEM_LIMIT = 56 * 1024 * 1024

F32 = jnp.float32
BF16 = jnp.bfloat16


def _rms(x, g):
    ms = jnp.mean(x * x, axis=-1, keepdims=True)
    return x * lax.rsqrt(ms + RMS_EPS) * g


def _alibi_slopes(n):
    return [2.0 ** (-8.0 * (i + 1) / n) for i in range(n)]


def _const_spec(shape):
    return pl.BlockSpec(shape, lambda *_: (0,) * len(shape), pipeline_mode=pl.Buffered(1))


def _params(*sem):
    return pltpu.CompilerParams(dimension_semantics=sem, vmem_limit_bytes=VMEM_LIMIT)


def _grid_transpose_perm(n, n_major, n_minor):
    row = lax.broadcasted_iota(jnp.int32, (n, n), 0)
    col = lax.broadcasted_iota(jnp.int32, (n, n), 1)
    src = (row % n_minor) * n_major + row // n_minor
    return jnp.where(col == src, 1.0, 0.0).astype(BF16)


def _permute_rows(perm, x):
    return jnp.dot(perm, x, preferred_element_type=F32).astype(BF16)


def _pack_rows(x):
    return pltpu.bitcast(x, jnp.int32)


def _unpack_rows(x):
    return pltpu.bitcast(x, BF16)


def _row_chunk_spec(a2d, n_steps):
    rows, rem = divmod(a2d.shape[0], n_steps)
    assert rem == 0 and rows % BF16_TILE_ROWS == 0
    return pl.BlockSpec((rows, a2d.shape[1]), lambda i: (i, 0))


def _norm_matmul_kernel(x_ref, g_ref, w_ref, *rest):
    n_casts = (len(rest) - 1) // 2
    o_ref = rest[n_casts]
    h = _rms(x_ref[...], g_ref[...]).astype(BF16)
    o_ref[...] = _pack_rows(jnp.dot(h, w_ref[...], preferred_element_type=F32).astype(BF16))
    for src_ref, dst_ref in zip(rest[:n_casts], rest[n_casts + 1:]):
        dst_ref[...] = src_ref[...].astype(dst_ref.dtype)


def _norm_matmul(x2d, g, w, tm, casts=()):
    t, d = x2d.shape
    n = w.shape[1]
    n_steps = t // tm
    cast_specs = [_row_chunk_spec(a, n_steps) for a in casts]
    return pl.pallas_call(
        _norm_matmul_kernel,
        grid=(n_steps,),
        in_specs=[pl.BlockSpec((tm, d), lambda i: (i, 0)), _const_spec((1, d)), _const_spec((d, n))] + cast_specs,
        out_specs=[pl.BlockSpec((tm // 2, n), lambda i: (i, 0))] + cast_specs,
        out_shape=[jax.ShapeDtypeStruct((t // 2, n), jnp.int32)]
        + [jax.ShapeDtypeStruct(a.shape, BF16) for a in casts],
        compiler_params=_params("parallel"),
    )(x2d, g.reshape(1, d), w, *casts)


def _norm_matmul_dilated_kernel(x_ref, g_ref, w_ref, *o_refs, dils):
    h = _rms(x_ref[...], g_ref[...]).astype(BF16)
    n = o_refs[0].shape[-1]
    for gi, (dil, o_ref) in enumerate(zip(dils, o_refs)):
        hg = h if dil == 1 else _to_classes(h, dil)
        y = jnp.dot(hg, w_ref[:, gi * n:(gi + 1) * n], preferred_element_type=F32).astype(BF16)
        o_ref[0] = _pack_rows(y).reshape(o_ref.shape[1:])


def _to_classes(h, dil):
    per = REGROUP // dil
    perm = _grid_transpose_perm(REGROUP, dil, per)
    parts = [_permute_rows(perm, h[b * REGROUP:(b + 1) * REGROUP]) for b in range(SPAN // REGROUP)]
    return jnp.concatenate([p[r * per:(r + 1) * per] for r in range(dil) for p in parts], axis=0)


def _norm_matmul_dilated(x, g, w, dils):
    b, s, d = x.shape
    n = w.shape[1] // len(dils)
    out_specs = [pl.BlockSpec((1, dil, SPAN // dil // 2, n), lambda bi, i: (bi, 0, i, 0)) for dil in dils]
    out_shape = [jax.ShapeDtypeStruct((b, dil, s // dil // 2, n), jnp.int32) for dil in dils]
    return pl.pallas_call(
        functools.partial(_norm_matmul_dilated_kernel, dils=dils),
        grid=(b, s // SPAN),
        in_specs=[pl.BlockSpec((None, SPAN, d), lambda bi, i: (bi, i, 0)),
                  _const_spec((1, d)), _const_spec(w.shape)],
        out_specs=out_specs,
        out_shape=out_shape,
        compiler_params=_params("parallel", "parallel"),
    )(x, g.reshape(1, d), w)


def _lane_group(shape):
    return lax.broadcasted_iota(jnp.int32, shape, 1) // HEAD_DIM


def _fill_bias(bias_ref, heads, *, tk, window, unit, offsets):
    slopes = _alibi_slopes(N_HEADS)
    key = lax.broadcasted_iota(jnp.int32, (tk, TQ), 0)
    qry = lax.broadcasted_iota(jnp.int32, (tk, TQ), 1)
    for v, off in enumerate(offsets):
        dist = jnp.abs(key - qry - off)
        negd = jnp.where(dist <= window, -(dist.astype(F32) * float(unit)), NEG)
        for c in range(N_HEADS // heads):
            for j in range(heads):
                bias_ref[v, c, :, j * TQ:(j + 1) * TQ] = (slopes[heads * c + j] * LOG2E) * negd


def _stack_heads(tiles, groups):
    lg = _lane_group(tiles[0].shape)
    return jnp.concatenate(
        [jnp.where(lg == g, t, jnp.zeros_like(t)) for t, g in zip(tiles, groups)], axis=0)


def _scores(q_stack, k, bias):
    return lax.dot_general(k, q_stack, (((1,), (1,)), ((), ())), preferred_element_type=F32) + bias


def _with_ones(vt):
    return jnp.concatenate([vt, jnp.ones((DEN_ROWS, vt.shape[1]), vt.dtype)], axis=0)


def _softmax_pv(sc, vt1):
    m = jnp.max(sc, axis=0, keepdims=True)
    p = jnp.exp2(sc - m)
    pv = jnp.dot(vt1, p.astype(BF16), preferred_element_type=F32)
    rows = vt1.shape[0] - DEN_ROWS
    return pv[:rows], m, pv[rows:rows + 1]


def _pipeline(stages, start_fn, finish_fn, depth):
    started = [start_fn(st) for st in stages[:depth]]
    for k, st in enumerate(stages):
        cur = started.pop(0)
        if k + depth < len(stages):
            started.append(start_fn(stages[k + depth]))
        finish_fn(st, cur)


def _window_start(q0, window, tk, seq_len):
    start = jnp.clip(q0 - window, 0, seq_len - tk)
    return pl.multiple_of(start, min(window, TQ)), (q0 - start) // window


def _split_f32(x):
    parts = []
    for _ in range(LSE_PARTS):
        p = x.astype(BF16)
        parts.append(p)
        x = x - p.astype(F32)
    return parts


def _attn_b_tiles(qkv_ref, tiles, bias_ref, out_refs, *, tk, window, seq_len):
    dq = N_HEADS * HEAD_DIM
    head_row = lax.broadcasted_iota(jnp.int32, (N_HEADS, TQ), 0)
    windows = [_window_start(q0, window, tk, seq_len) for _, _, q0, _ in tiles]
    acc = {}

    def start(stage):
        t, c = stage
        idx, r0, _, _ = tiles[t]
        win0, var = windows[t]
        lo = c * B_COLS
        q_rows = pl.ds(pl.multiple_of(r0 // 2, TQ // 2), TQ // 2)
        kv_rows = pl.ds(pl.multiple_of(win0 // 2, min(window, TQ) // 2), tk // 2)
        q = _unpack_rows(qkv_ref[idx + (q_rows, slice(lo, lo + B_COLS))])
        k = _unpack_rows(qkv_ref[idx + (kv_rows, slice(dq + lo, dq + lo + B_COLS))])
        v = _unpack_rows(qkv_ref[idx + (kv_rows, slice(2 * dq + lo, 2 * dq + lo + B_COLS))])
        q_stack = _stack_heads([q] * B_HEADS, range(B_HEADS))
        return _scores(q_stack, k, bias_ref[var, c]), _with_ones(v.T)

    def finish(stage, started):
        t, c = stage
        pv, m, den = _softmax_pv(*started)
        rden = 1.0 / den
        lse = m + jnp.log2(den)
        outs, lse16 = acc.get(t, ([], jnp.zeros((N_HEADS, TQ), F32)))
        for j in range(B_HEADS):
            cols = slice(j * TQ, (j + 1) * TQ)
            outs.append((pv[j * HEAD_DIM:(j + 1) * HEAD_DIM, cols] * rden[:, cols]).astype(BF16))
            lse16 = jnp.where(head_row == B_HEADS * c + j, lse[:, cols], lse16)
        acc[t] = (outs, lse16)
        if c == B_STAGES - 1:
            pad = jnp.zeros((LANES - LSE_PARTS * N_HEADS, TQ), BF16)
            tile_t = jnp.concatenate(outs + _split_f32(lse16) + [pad], axis=0)
            rows, lo = pl.ds(tiles[t][3], TQ), 0
            tile = tile_t.T
            for ref in out_refs:
                ref[rows, :] = tile[:, lo:lo + ref.shape[1]]
                lo += ref.shape[1]

    _pipeline([(t, c) for t in range(len(tiles)) for c in range(B_STAGES)], start, finish, B_DEPTH)


def _attn_b_kernel(qkv_ref, o_ref, lse_ref, bias_ref, *stage, seq_len, tk, window, unit, offsets, dil):
    @pl.when(pl.program_id(0) == 0)
    def _():
        _fill_bias(bias_ref, B_HEADS, tk=tk, window=window, unit=unit, offsets=offsets)

    dq = N_HEADS * HEAD_DIM
    n_tiles = dil * seq_len // TQ
    out_refs = (o_ref, lse_ref) if dil == 1 else stage

    def tile_group(g, carry):
        tiles = []
        for u in range(B_TILES):
            t = g * B_TILES + u
            s0 = pl.multiple_of(t * TQ, TQ)
            if dil == 1:
                tiles.append(((), s0, s0, s0))
            else:
                q0 = pl.multiple_of((t // dil) * TQ, TQ)
                tiles.append(((t % dil,), q0, q0, s0))
        _attn_b_tiles(qkv_ref, tiles, bias_ref, out_refs, tk=tk, window=window, seq_len=seq_len)
        return carry

    lax.fori_loop(0, n_tiles // B_TILES, tile_group, 0)

    if dil > 1:
        (stage_ref,) = stage
        per_cls = REGROUP // dil
        perm = _grid_transpose_perm(REGROUP, per_cls, dil)
        for k in range(n_tiles * TQ // REGROUP):
            span, off = divmod(k * per_cls, TQ)
            rows = [stage_ref[(span * dil + r) * TQ + off:(span * dil + r) * TQ + off + per_cls, :]
                    for r in range(dil)]
            nat = _permute_rows(perm, jnp.concatenate(rows, axis=0))
            o_ref[k * REGROUP:(k + 1) * REGROUP, :] = nat[:, :dq]
            lse_ref[k * REGROUP:(k + 1) * REGROUP, :] = nat[:, dq:]


def _attention_b(qkv, *, dil, window):
    batch, _, packed_len, width = qkv.shape
    seq_len = 2 * packed_len
    dq = N_HEADS * HEAD_DIM
    seq = seq_len * dil
    tk = min(TQ + 2 * window, seq_len)
    offsets = (0,) if tk == seq_len else (0, window, 2 * window)
    if dil == 1:
        in_spec = pl.BlockSpec((None, None, packed_len, width), lambda b: (b, 0, 0, 0))
    else:
        in_spec = pl.BlockSpec((None, dil, packed_len, width), lambda b: (b, 0, 0, 0))
    kern = functools.partial(_attn_b_kernel, seq_len=seq_len, tk=tk, window=window, unit=dil,
                             offsets=offsets, dil=dil)
    return pl.pallas_call(
        kern,
        grid=(batch,),
        in_specs=[in_spec],
        out_specs=[pl.BlockSpec((seq, dq), lambda b: (b, 0)), pl.BlockSpec((seq, LANES), lambda b: (b, 0))],
        out_shape=[jax.ShapeDtypeStruct((batch * seq, dq), BF16),
                   jax.ShapeDtypeStruct((batch * seq, LANES), BF16)],
        scratch_shapes=[pltpu.VMEM((len(offsets), B_STAGES, tk, B_HEADS * TQ), F32)]
        + ([pltpu.VMEM((seq, dq + LANES), BF16)] if dil > 1 else []),
        compiler_params=_params("arbitrary"),
    )(qkv)


def _attn_a_kernel(sink_ref, q_ref, k_ref, v_ref, o_ref, bias_ref, *, seq_len, tk, window, offsets):
    @pl.when(pl.program_id(0) == 0)
    def _():
        _fill_bias(bias_ref, HEADS_PER_TILE, tk=tk, window=window, unit=1, offsets=offsets)

    lax.fori_loop(0, seq_len // (A_TILES * TQ),
                  functools.partial(_attn_a_tile_group, sink_ref, q_ref, k_ref, v_ref, o_ref, bias_ref,
                                    seq_len=seq_len, tk=tk, window=window), 0)


def _attn_a_tile_group(sink_ref, q_ref, k_ref, v_ref, o_ref, bias_ref, g, carry, *, seq_len, tk, window):
    qblk = lax.broadcasted_iota(jnp.int32, (1, A_HEADS * TQ), 1) // TQ
    rows = [pl.multiple_of((g * A_TILES + t) * TQ, TQ) for t in range(A_TILES)]
    tiles = [_window_start(q0, window, tk, seq_len) for q0 in rows]
    shared = {}
    blocks = {}

    def start(stage):
        t, c, half = stage
        win0, var = tiles[t]
        if (c, half) == (0, 0):
            q_rows = pl.ds(pl.multiple_of(rows[t] // 2, TQ // 2), TQ // 2)
            kv_rows = pl.ds(pl.multiple_of(win0 // 2, min(window, TQ) // 2), tk // 2)
            qs = [_unpack_rows(q_ref[q_rows, j * MXU_DIM:(j + 1) * MXU_DIM]) for j in range(HEADS_PER_TILE)]
            shared[t] = (qs, _unpack_rows(k_ref[kv_rows, :]), _unpack_rows(v_ref[kv_rows, :]).T)
        qs, k4, vt = shared[t]
        vt_c = _with_ones(vt[c * HEAD_DIM:(c + 1) * HEAD_DIM])
        q_stack = _stack_heads(qs[half * A_HEADS:(half + 1) * A_HEADS], [c] * A_HEADS)
        bias = bias_ref[var, c, :, half * A_HEADS * TQ:(half + 1) * A_HEADS * TQ]
        return _scores(q_stack, k4, bias), vt_c

    def finish(stage, started):
        t, c, half = stage
        pv, m, den = _softmax_pv(*started)
        sink = jnp.zeros_like(m)
        for jj in range(A_HEADS):
            sink = jnp.where(qblk == jj, sink_ref[HEADS_PER_TILE * c + half * A_HEADS + jj] * LOG2E, sink)
        gate = 1.0 / (den + jnp.exp2(sink - m))
        for jj in range(A_HEADS):
            cols = slice(jj * TQ, (jj + 1) * TQ)
            blocks[t, half * A_HEADS + jj, c] = (pv[:, cols] * gate[:, cols]).astype(o_ref.dtype)
        if (c, half) == (KV_A - 1, HEADS_PER_TILE // A_HEADS - 1):
            tile_t = jnp.concatenate(
                [blocks.pop((t, j, cc)) for j in range(HEADS_PER_TILE) for cc in range(KV_A)], axis=0)
            o_ref[pl.ds(rows[t], TQ), :] = tile_t.T

    _pipeline([(t, c, half) for t in range(A_TILES) for c in range(KV_A)
               for half in range(HEADS_PER_TILE // A_HEADS)], start, finish, A_DEPTH)
    return carry


def _attention_a(qkv, sink):
    b, packed_s, _ = qkv.shape
    s = 2 * packed_s
    dq = N_HEADS * HEAD_DIM
    dkv = KV_A * HEAD_DIM
    tk = TQ + 2 * WINDOW_A
    offsets = (0, WINDOW_A, 2 * WINDOW_A)
    kern = functools.partial(_attn_a_kernel, seq_len=s, tk=tk, window=WINDOW_A, offsets=offsets)
    return pl.pallas_call(
        kern,
        grid=(b,),
        in_specs=[pl.BlockSpec(memory_space=pltpu.SMEM),
                  pl.BlockSpec((None, packed_s, dq), lambda bi: (bi, 0, 0)),
                  pl.BlockSpec((None, packed_s, dkv), lambda bi: (bi, 0, dq // dkv)),
                  pl.BlockSpec((None, packed_s, dkv), lambda bi: (bi, 0, dq // dkv + 1))],
        out_specs=pl.BlockSpec((None, s, dq), lambda bi: (bi, 0, 0)),
        out_shape=jax.ShapeDtypeStruct((b, s, dq), BF16),
        scratch_shapes=[pltpu.VMEM((len(offsets), KV_A, tk, HEADS_PER_TILE * TQ), F32)],
        compiler_params=_params("arbitrary"),
    )(sink, qkv, qkv, qkv)


def _merge_groups(o_refs, lse_refs):
    src = lax.broadcasted_iota(jnp.int32, (LANES, LANES), 0)
    dst = lax.broadcasted_iota(jnp.int32, (LANES, LANES), 1)
    fold = jnp.where((src % N_HEADS == dst % N_HEADS) & (src < LSE_PARTS * N_HEADS), 1.0, 0.0).astype(BF16)
    src = lax.broadcasted_iota(jnp.int32, (LANES, D_MODEL), 0)
    dst = lax.broadcasted_iota(jnp.int32, (LANES, D_MODEL), 1)
    expand = jnp.where((src % N_HEADS == dst // HEAD_DIM) & (src < LSE_PARTS * N_HEADS), 1.0, 0.0).astype(BF16)
    lses = [jnp.dot(r[...], fold, preferred_element_type=F32) for r in lse_refs]
    mx = functools.reduce(jnp.maximum, lses)
    es = [jnp.exp2(l - mx) for l in lses]
    den = functools.reduce(lambda a, b: a + b, es)
    piece_of_lane = lax.broadcasted_iota(jnp.int32, es[0].shape, 1) // N_HEADS
    o = None
    w_sum = None
    for e, o_ref in zip(es[:-1], o_refs[:-1]):
        packed = jnp.zeros(e.shape, BF16)
        for k, piece in enumerate(_split_f32(e / den)):
            packed = jnp.where(piece_of_lane == k, piece, packed)
        w = jnp.dot(packed, expand, preferred_element_type=F32)
        w_sum = w if w_sum is None else w_sum + w
        o = w * o_ref[...].astype(F32) if o is None else o + w * o_ref[...].astype(F32)
    o = o + (1.0 - w_sum) * o_refs[-1][...].astype(F32)
    return o.astype(BF16)


def _post_kernel(*refs, n_groups, final, ff_chunks, side_cast):
    refs = list(refs)
    x_ref = refs.pop(0)
    o_refs = [refs.pop(0) for _ in range(n_groups)]
    lse_refs = [refs.pop(0) for _ in range(n_groups)] if n_groups > 1 else []
    wo_ref, gf_ref, wg_ref, wu_ref, wd_ref = refs[:5]
    refs = refs[5:]
    gfin_ref = refs.pop(0) if final else None
    if side_cast:
        src_ref, scale_ref, out_ref, dst_ref = refs
        dst_ref[...] = (src_ref[...] * scale_ref[...]).astype(dst_ref.dtype)
    else:
        (out_ref,) = refs

    o = _merge_groups(o_refs, lse_refs) if n_groups > 1 else o_refs[0][...]
    x1 = x_ref[...] + jnp.dot(o, wo_ref[...], preferred_element_type=F32)
    h = (x1 * gf_ref[...]).astype(BF16)
    r = lax.rsqrt(jnp.mean(x1 * x1, axis=-1, keepdims=True) + RMS_EPS)
    acc = x1
    lo = 0
    for fc in ff_chunks:
        gate = r * jnp.dot(h, wg_ref[:, lo:lo + fc], preferred_element_type=F32)
        up = r * jnp.dot(h, wu_ref[:, lo:lo + fc], preferred_element_type=F32)
        a = (gate / (1.0 + jnp.exp(-gate)) * up).astype(BF16)
        acc = acc + jnp.dot(a, wd_ref[lo:lo + fc, :], preferred_element_type=F32)
        lo += fc
    if final:
        acc = _rms(acc, gfin_ref[...])
    out_ref[...] = acc


def _ff_chunks(d_ff, n):
    tiles, rem = divmod(d_ff, MXU_DIM)
    assert rem == 0
    return tuple((tiles // n + (k < tiles % n)) * MXU_DIM for k in range(n))


def _layer_spec(shape, layer):
    return pl.BlockSpec((None,) + tuple(shape[1:]), lambda *_: (layer,) + (0,) * (len(shape) - 1),
                        pipeline_mode=pl.Buffered(1))


def _post(x2d, os, lses, wo, gf, wg, wu, wd, layer, gfin, tm, ff_chunks, side_cast=None):
    t, d = x2d.shape
    n_steps = t // tm
    ff_chunks = _ff_chunks(wg.shape[2], ff_chunks)
    final = gfin is not None
    row = lambda n: pl.BlockSpec((tm, n), lambda i: (i, 0))
    in_specs = [row(d)] * (1 + len(os)) + [row(LANES)] * len(lses)
    in_specs += [_const_spec((d, d))] + [_layer_spec(a.shape, layer) for a in (gf, wg, wu, wd)]
    args = [x2d, *os, *lses, wo, gf, wg, wu, wd]
    out_specs = [row(d)]
    out_shape = [jax.ShapeDtypeStruct((t, d), F32)]
    if final:
        in_specs.append(_const_spec((1, d)))
        args.append(gfin.reshape(1, d))
    if side_cast:
        w, scale = side_cast
        in_specs += [_row_chunk_spec(w, n_steps), _const_spec((1, w.shape[1]))]
        args += [w, scale.reshape(1, -1)]
        out_specs.append(_row_chunk_spec(w, n_steps))
        out_shape.append(jax.ShapeDtypeStruct(w.shape, BF16))
    return pl.pallas_call(
        functools.partial(_post_kernel, n_groups=len(os), final=final, ff_chunks=ff_chunks,
                          side_cast=bool(side_cast)),
        grid=(n_steps,),
        in_specs=in_specs,
        out_specs=out_specs,
        out_shape=out_shape,
        compiler_params=_params("parallel"),
    )(*args)


def _q_column_scale(n_cols, is_query_col):
    cols = np.arange(n_cols)
    return jnp.asarray(np.where(is_query_col(cols), HEAD_DIM ** -0.5 * LOG2E, 1.0), F32)


def kernel(x, norm_mix, norm_ffn, w_qkv_a, w_out_a, sink_a, w_qkv_b, w_out_b,
           w_gate, w_up, w_down, norm_final):
    b, s, d = x.shape
    t = b * s
    dq = N_HEADS * HEAD_DIM
    d_ff = w_gate.shape[2]

    wa = w_qkv_a[0] * _q_column_scale(w_qkv_a.shape[2], lambda c: c < dq)
    wq = wa[:, :dq].reshape(d, KV_A, HEADS_PER_TILE, HEAD_DIM).transpose(0, 2, 1, 3).reshape(d, dq)
    wa = jnp.concatenate([wq, wa[:, dq:]], axis=1).astype(BF16)
    wo_a = w_out_a[0].reshape(KV_A, HEADS_PER_TILE, HEAD_DIM, d).transpose(1, 0, 2, 3).reshape(dq, d)
    x2d = x.reshape(t, d)
    qkv, wg, wu, wd = _norm_matmul(
        x2d, norm_mix[0], wa, tm=2048,
        casts=(w_gate.reshape(-1, d_ff), w_up.reshape(-1, d_ff), w_down.reshape(-1, d)))
    ffn = (norm_ffn.reshape(-1, 1, d), wg.reshape(w_gate.shape), wu.reshape(w_up.shape), wd.reshape(w_down.shape))
    o_a = _attention_a(qkv.reshape(b, s // 2, -1), sink_a[0])
    wb_scale = _q_column_scale(w_qkv_b.shape[2], lambda c: c % (3 * dq) < dq)
    x2d, wb = _post(x2d, [o_a.reshape(t, d)], [], wo_a.astype(BF16), *ffn, 0, None, tm=1024, ff_chunks=4,
                    side_cast=(w_qkv_b[0], wb_scale))

    dils = tuple(dil for _, dil in DILATED_GROUPS)
    qkvs = _norm_matmul_dilated(x2d.reshape(b, s, d), norm_mix[1], wb, dils)
    os, lses = [], []
    for qkv_g, (win, dil) in zip(qkvs, DILATED_GROUPS):
        o_g, lse_g = _attention_b(qkv_g, dil=dil, window=win // (2 * dil))
        os.append(o_g)
        lses.append(lse_g)
    (out,) = _post(x2d, os, lses, w_out_b[0].astype(BF16), *ffn, 1, norm_final, tm=512, ff_chunks=2)
    return out.reshape(b, s, d)
```

```python
import functools

import jax
import jax.numpy as jnp
import numpy as np
from jax import lax
from jax.experimental import pallas as pl
from jax.experimental.pallas import tpu as pltpu

D_MODEL = 1024
HEAD_DIM = 64
N_HEADS = 16
KV_A = 4
WINDOW_A = 128
DILATED_GROUPS = ((128, 1), (512, 4), (2048, 16))
RMS_EPS = 1e-6
NEG = -1e30
LANES = 128
MXU_DIM = 256
HEADS_PER_TILE = MXU_DIM // HEAD_DIM
TQ = 128
A_TILES = 8
B_TILES = 16
A_DEPTH = 4
B_DEPTH = 6
A_HEADS = 4
B_HEADS = 2
B_COLS = B_HEADS * HEAD_DIM
B_STAGES = N_HEADS // B_HEADS
SPAN = 512
REGROUP = MXU_DIM
LSE_PARTS = 3
BF16_TILE_ROWS = 16
DEN_ROWS = BF16_TILE_ROWS
LOG2E = 1.4426950408889634
VMEM_LIMIT = 56 * 1024 * 1024

F32 = jnp.float32
BF16 = jnp.bfloat16


def _rms(x, g):
    ms = jnp.mean(x * x, axis=-1, keepdims=True)
    return x * lax.rsqrt(ms + RMS_EPS) * g


def _alibi_slopes(n):
    return [2.0 ** (-8.0 * (i + 1) / n) for i in range(n)]


def _const_spec(shape):
    return pl.BlockSpec(shape, lambda *_: (0,) * len(shape), pipeline_mode=pl.Buffered(1))


def _params(*sem):
    return pltpu.CompilerParams(dimension_semantics=sem, vmem_limit_bytes=VMEM_LIMIT)


def _grid_transpose_perm(n, n_major, n_minor):
    row = lax.broadcasted_iota(jnp.int32, (n, n), 0)
    col = lax.broadcasted_iota(jnp.int32, (n, n), 1)
    src = (row % n_minor) * n_major + row // n_minor
    return jnp.where(col == src, 1.0, 0.0).astype(BF16)


def _permute_rows(perm, x):
    return jnp.dot(perm, x, preferred_element_type=F32).astype(BF16)


def _pack_rows(x):
    return pltpu.bitcast(x, jnp.int32)


def _unpack_rows(x):
    return pltpu.bitcast(x, BF16)


def _row_chunk_spec(a2d, n_steps):
    rows, rem = divmod(a2d.shape[0], n_steps)
    assert rem == 0 and rows % BF16_TILE_ROWS == 0
    return pl.BlockSpec((rows, a2d.shape[1]), lambda i: (i, 0))


def _norm_matmul_kernel(x_ref, g_ref, w_ref, *rest):
    n_casts = (len(rest) - 1) // 2
    o_ref = rest[n_casts]
    h = _rms(x_ref[...], g_ref[...]).astype(BF16)
    o_ref[...] = _pack_rows(jnp.dot(h, w_ref[...], preferred_element_type=F32).astype(BF16))
    for src_ref, dst_ref in zip(rest[:n_casts], rest[n_casts + 1:]):
        dst_ref[...] = src_ref[...].astype(dst_ref.dtype)


def _norm_matmul(x2d, g, w, tm, casts=()):
    t, d = x2d.shape
    n = w.shape[1]
    n_steps = t // tm
    cast_specs = [_row_chunk_spec(a, n_steps) for a in casts]
    return pl.pallas_call(
        _norm_matmul_kernel,
        grid=(n_steps,),
        in_specs=[pl.BlockSpec((tm, d), lambda i: (i, 0)), _const_spec((1, d)), _const_spec((d, n))] + cast_specs,
        out_specs=[pl.BlockSpec((tm // 2, n), lambda i: (i, 0))] + cast_specs,
        out_shape=[jax.ShapeDtypeStruct((t // 2, n), jnp.int32)]
        + [jax.ShapeDtypeStruct(a.shape, BF16) for a in casts],
        compiler_params=_params("parallel"),
    )(x2d, g.reshape(1, d), w, *casts)


def _norm_matmul_dilated_kernel(x_ref, g_ref, w_ref, *o_refs, dils):
    h = _rms(x_ref[...], g_ref[...]).astype(BF16)
    n = o_refs[0].shape[-1]
    for gi, (dil, o_ref) in enumerate(zip(dils, o_refs)):
        hg = h if dil == 1 else _to_classes(h, dil)
        y = jnp.dot(hg, w_ref[:, gi * n:(gi + 1) * n], preferred_element_type=F32).astype(BF16)
        o_ref[0] = _pack_rows(y).reshape(o_ref.shape[1:])


def _to_classes(h, dil):
    per = REGROUP // dil
    perm = _grid_transpose_perm(REGROUP, dil, per)
    parts = [_permute_rows(perm, h[b * REGROUP:(b + 1) * REGROUP]) for b in range(SPAN // REGROUP)]
    return jnp.concatenate([p[r * per:(r + 1) * per] for r in range(dil) for p in parts], axis=0)


def _norm_matmul_dilated(x, g, w, dils):
    b, s, d = x.shape
    n = w.shape[1] // len(dils)
    out_specs = [pl.BlockSpec((1, dil, SPAN // dil // 2, n), lambda bi, i: (bi, 0, i, 0)) for dil in dils]
    out_shape = [jax.ShapeDtypeStruct((b, dil, s // dil // 2, n), jnp.int32) for dil in dils]
    return pl.pallas_call(
        functools.partial(_norm_matmul_dilated_kernel, dils=dils),
        grid=(b, s // SPAN),
        in_specs=[pl.BlockSpec((None, SPAN, d), lambda bi, i: (bi, i, 0)),
                  _const_spec((1, d)), _const_spec(w.shape)],
        out_specs=out_specs,
        out_shape=out_shape,
        compiler_params=_params("parallel", "parallel"),
    )(x, g.reshape(1, d), w)


def _lane_group(shape):
    return lax.broadcasted_iota(jnp.int32, shape, 1) // HEAD_DIM


def _fill_bias(bias_ref, heads, *, tk, window, unit, offsets):
    slopes = _alibi_slopes(N_HEADS)
    key = lax.broadcasted_iota(jnp.int32, (tk, TQ), 0)
    qry = lax.broadcasted_iota(jnp.int32, (tk, TQ), 1)
    for v, off in enumerate(offsets):
        dist = jnp.abs(key - qry - off)
        negd = jnp.where(dist <= window, -(dist.astype(F32) * float(unit)), NEG)
        for c in range(N_HEADS // heads):
            for j in range(heads):
                bias_ref[v, c, :, j * TQ:(j + 1) * TQ] = (slopes[heads * c + j] * LOG2E) * negd


def _stack_heads(tiles, groups):
    lg = _lane_group(tiles[0].shape)
    return jnp.concatenate(
        [jnp.where(lg == g, t, jnp.zeros_like(t)) for t, g in zip(tiles, groups)], axis=0)


def _scores(q_stack, k, bias):
    return lax.dot_general(k, q_stack, (((1,), (1,)), ((), ())), preferred_element_type=F32) + bias


def _with_ones(vt):
    return jnp.concatenate([vt, jnp.ones((DEN_ROWS, vt.shape[1]), vt.dtype)], axis=0)


def _softmax_pv(sc, vt1):
    m = jnp.max(sc, axis=0, keepdims=True)
    p = jnp.exp2(sc - m)
    pv = jnp.dot(vt1, p.astype(BF16), preferred_element_type=F32)
    rows = vt1.shape[0] - DEN_ROWS
    return pv[:rows], m, pv[rows:rows + 1]


def _pipeline(stages, start_fn, finish_fn, depth):
    started = [start_fn(st) for st in stages[:depth]]
    for k, st in enumerate(stages):
        cur = started.pop(0)
        if k + depth < len(stages):
            started.append(start_fn(stages[k + depth]))
        finish_fn(st, cur)


def _window_start(q0, window, tk, seq_len):
    start = jnp.clip(q0 - window, 0, seq_len - tk)
    return pl.multiple_of(start, min(window, TQ)), (q0 - start) // window


def _split_f32(x):
    parts = []
    for _ in range(LSE_PARTS):
        p = x.astype(BF16)
        parts.append(p)
        x = x - p.astype(F32)
    return parts


def _attn_b_tiles(qkv_ref, tiles, bias_ref, out_refs, *, tk, window, seq_len):
    dq = N_HEADS * HEAD_DIM
    head_row = lax.broadcasted_iota(jnp.int32, (N_HEADS, TQ), 0)
    windows = [_window_start(q0, window, tk, seq_len) for _, _, q0, _ in tiles]
    acc = {}

    def start(stage):
        t, c = stage
        idx, r0, _, _ = tiles[t]
        win0, var = windows[t]
        lo = c * B_COLS
        q_rows = pl.ds(pl.multiple_of(r0 // 2, TQ // 2), TQ // 2)
        kv_rows = pl.ds(pl.multiple_of(win0 // 2, min(window, TQ) // 2), tk // 2)
        q = _unpack_rows(qkv_ref[idx + (q_rows, slice(lo, lo + B_COLS))])
        k = _unpack_rows(qkv_ref[idx + (kv_rows, slice(dq + lo, dq + lo + B_COLS))])
        v = _unpack_rows(qkv_ref[idx + (kv_rows, slice(2 * dq + lo, 2 * dq + lo + B_COLS))])
        q_stack = _stack_heads([q] * B_HEADS, range(B_HEADS))
        return _scores(q_stack, k, bias_ref[var, c]), _with_ones(v.T)

    def finish(stage, started):
        t, c = stage
        pv, m, den = _softmax_pv(*started)
        rden = 1.0 / den
        lse = m + jnp.log2(den)
        outs, lse16 = acc.get(t, ([], jnp.zeros((N_HEADS, TQ), F32)))
        for j in range(B_HEADS):
            cols = slice(j * TQ, (j + 1) * TQ)
            outs.append((pv[j * HEAD_DIM:(j + 1) * HEAD_DIM, cols] * rden[:, cols]).astype(BF16))
            lse16 = jnp.where(head_row == B_HEADS * c + j, lse[:, cols], lse16)
        acc[t] = (outs, lse16)
        if c == B_STAGES - 1:
            pad = jnp.zeros((LANES - LSE_PARTS * N_HEADS, TQ), BF16)
            tile_t = jnp.concatenate(outs + _split_f32(lse16) + [pad], axis=0)
            s0, lo = tiles[t][3], 0
            tile = tile_t.T
            for ref in out_refs:
                piece = tile[:, lo:lo + ref.shape[1]]
                if ref.dtype == jnp.int32:
                    ref[pl.ds(pl.multiple_of(s0 // 2, TQ // 2), TQ // 2), :] = _pack_rows(piece)
                else:
                    ref[pl.ds(s0, TQ), :] = piece
                lo += ref.shape[1]

    _pipeline([(t, c) for t in range(len(tiles)) for c in range(B_STAGES)], start, finish, B_DEPTH)


def _attn_b_kernel(qkv_ref, o_ref, lse_ref, bias_ref, *stage, seq_len, tk, window, unit, offsets, dil):
    @pl.when(pl.program_id(0) == 0)
    def _():
        _fill_bias(bias_ref, B_HEADS, tk=tk, window=window, unit=unit, offsets=offsets)

    dq = N_HEADS * HEAD_DIM
    n_tiles = dil * seq_len // TQ
    out_refs = (o_ref, lse_ref) if dil == 1 else stage

    def tile_group(g, carry):
        tiles = []
        for u in range(B_TILES):
            t = g * B_TILES + u
            s0 = pl.multiple_of(t * TQ, TQ)
            if dil == 1:
                tiles.append(((), s0, s0, s0))
            else:
                q0 = pl.multiple_of((t // dil) * TQ, TQ)
                tiles.append(((t % dil,), q0, q0, s0))
        _attn_b_tiles(qkv_ref, tiles, bias_ref, out_refs, tk=tk, window=window, seq_len=seq_len)
        return carry

    lax.fori_loop(0, n_tiles // B_TILES, tile_group, 0)

    if dil > 1:
        (stage_ref,) = stage
        per_cls = REGROUP // dil
        perm = _grid_transpose_perm(REGROUP, per_cls, dil)
        for k in range(n_tiles * TQ // REGROUP):
            span, off = divmod(k * per_cls, TQ)
            rows = [stage_ref[(span * dil + r) * TQ + off:(span * dil + r) * TQ + off + per_cls, :]
                    for r in range(dil)]
            nat = _permute_rows(perm, jnp.concatenate(rows, axis=0))
            o_ref[k * REGROUP // 2:(k + 1) * REGROUP // 2, :] = _pack_rows(nat[:, :dq])
            lse_ref[k * REGROUP // 2:(k + 1) * REGROUP // 2, :] = _pack_rows(nat[:, dq:])


def _attention_b(qkv, *, dil, window):
    batch, _, packed_len, width = qkv.shape
    seq_len = 2 * packed_len
    dq = N_HEADS * HEAD_DIM
    seq = seq_len * dil
    tk = min(TQ + 2 * window, seq_len)
    offsets = (0,) if tk == seq_len else (0, window, 2 * window)
    if dil == 1:
        in_spec = pl.BlockSpec((None, None, packed_len, width), lambda b: (b, 0, 0, 0))
    else:
        in_spec = pl.BlockSpec((None, dil, packed_len, width), lambda b: (b, 0, 0, 0))
    kern = functools.partial(_attn_b_kernel, seq_len=seq_len, tk=tk, window=window, unit=dil,
                             offsets=offsets, dil=dil)
    return pl.pallas_call(
        kern,
        grid=(batch,),
        in_specs=[in_spec],
        out_specs=[pl.BlockSpec((seq // 2, dq), lambda b: (b, 0)), pl.BlockSpec((seq // 2, LANES), lambda b: (b, 0))],
        out_shape=[jax.ShapeDtypeStruct((batch * seq // 2, dq), jnp.int32),
                   jax.ShapeDtypeStruct((batch * seq // 2, LANES), jnp.int32)],
        scratch_shapes=[pltpu.VMEM((len(offsets), B_STAGES, tk, B_HEADS * TQ), F32)]
        + ([pltpu.VMEM((seq, dq + LANES), BF16)] if dil > 1 else []),
        compiler_params=_params("arbitrary"),
    )(qkv)


def _attn_a_kernel(sink_ref, q_ref, k_ref, v_ref, o_ref, bias_ref, *, seq_len, tk, window, offsets):
    @pl.when(pl.program_id(0) == 0)
    def _():
        _fill_bias(bias_ref, HEADS_PER_TILE, tk=tk, window=window, unit=1, offsets=offsets)

    lax.fori_loop(0, seq_len // (A_TILES * TQ),
                  functools.partial(_attn_a_tile_group, sink_ref, q_ref, k_ref, v_ref, o_ref, bias_ref,
                                    seq_len=seq_len, tk=tk, window=window), 0)


def _attn_a_tile_group(sink_ref, q_ref, k_ref, v_ref, o_ref, bias_ref, g, carry, *, seq_len, tk, window):
    qblk = lax.broadcasted_iota(jnp.int32, (1, A_HEADS * TQ), 1) // TQ
    rows = [pl.multiple_of((g * A_TILES + t) * TQ, TQ) for t in range(A_TILES)]
    tiles = [_window_start(q0, window, tk, seq_len) for q0 in rows]
    shared = {}
    blocks = {}

    def start(stage):
        t, c, half = stage
        win0, var = tiles[t]
        if (c, half) == (0, 0):
            q_rows = pl.ds(pl.multiple_of(rows[t] // 2, TQ // 2), TQ // 2)
            kv_rows = pl.ds(pl.multiple_of(win0 // 2, min(window, TQ) // 2), tk // 2)
            qs = [_unpack_rows(q_ref[q_rows, j * MXU_DIM:(j + 1) * MXU_DIM]) for j in range(HEADS_PER_TILE)]
            shared[t] = (qs, _unpack_rows(k_ref[kv_rows, :]), _unpack_rows(v_ref[kv_rows, :]).T)
        qs, k4, vt = shared[t]
        vt_c = _with_ones(vt[c * HEAD_DIM:(c + 1) * HEAD_DIM])
        q_stack = _stack_heads(qs[half * A_HEADS:(half + 1) * A_HEADS], [c] * A_HEADS)
        bias = bias_ref[var, c, :, half * A_HEADS * TQ:(half + 1) * A_HEADS * TQ]
        return _scores(q_stack, k4, bias), vt_c

    def finish(stage, started):
        t, c, half = stage
        pv, m, den = _softmax_pv(*started)
        sink = jnp.zeros_like(m)
        for jj in range(A_HEADS):
            sink = jnp.where(qblk == jj, sink_ref[HEADS_PER_TILE * c + half * A_HEADS + jj] * LOG2E, sink)
        gate = 1.0 / (den + jnp.exp2(sink - m))
        for jj in range(A_HEADS):
            cols = slice(jj * TQ, (jj + 1) * TQ)
            blocks[t, half * A_HEADS + jj, c] = (pv[:, cols] * gate[:, cols]).astype(o_ref.dtype)
        if (c, half) == (KV_A - 1, HEADS_PER_TILE // A_HEADS - 1):
            tile_t = jnp.concatenate(
                [blocks.pop((t, j, cc)) for j in range(HEADS_PER_TILE) for cc in range(KV_A)], axis=0)
            o_ref[pl.ds(rows[t], TQ), :] = tile_t.T

    _pipeline([(t, c, half) for t in range(A_TILES) for c in range(KV_A)
               for half in range(HEADS_PER_TILE // A_HEADS)], start, finish, A_DEPTH)
    return carry


def _attention_a(qkv, sink):
    b, packed_s, _ = qkv.shape
    s = 2 * packed_s
    dq = N_HEADS * HEAD_DIM
    dkv = KV_A * HEAD_DIM
    tk = TQ + 2 * WINDOW_A
    offsets = (0, WINDOW_A, 2 * WINDOW_A)
    kern = functools.partial(_attn_a_kernel, seq_len=s, tk=tk, window=WINDOW_A, offsets=offsets)
    return pl.pallas_call(
        kern,
        grid=(b,),
        in_specs=[pl.BlockSpec(memory_space=pltpu.SMEM),
                  pl.BlockSpec((None, packed_s, dq), lambda bi: (bi, 0, 0)),
                  pl.BlockSpec((None, packed_s, dkv), lambda bi: (bi, 0, dq // dkv)),
                  pl.BlockSpec((None, packed_s, dkv), lambda bi: (bi, 0, dq // dkv + 1))],
        out_specs=pl.BlockSpec((None, s, dq), lambda bi: (bi, 0, 0)),
        out_shape=jax.ShapeDtypeStruct((b, s, dq), BF16),
        scratch_shapes=[pltpu.VMEM((len(offsets), KV_A, tk, HEADS_PER_TILE * TQ), F32)],
        compiler_params=_params("arbitrary"),
    )(sink, qkv, qkv, qkv)


def _merge_groups(o_refs, lse_refs):
    src = lax.broadcasted_iota(jnp.int32, (LANES, LANES), 0)
    dst = lax.broadcasted_iota(jnp.int32, (LANES, LANES), 1)
    fold = jnp.where((src % N_HEADS == dst % N_HEADS) & (src < LSE_PARTS * N_HEADS), 1.0, 0.0).astype(BF16)
    src = lax.broadcasted_iota(jnp.int32, (LANES, D_MODEL), 0)
    dst = lax.broadcasted_iota(jnp.int32, (LANES, D_MODEL), 1)
    expand = jnp.where((src % N_HEADS == dst // HEAD_DIM) & (src < LSE_PARTS * N_HEADS), 1.0, 0.0).astype(BF16)
    lses = [jnp.dot(_unpack_rows(r[...]), fold, preferred_element_type=F32) for r in lse_refs]
    mx = functools.reduce(jnp.maximum, lses)
    es = [jnp.exp2(l - mx) for l in lses]
    den = functools.reduce(lambda a, b: a + b, es)
    piece_of_lane = lax.broadcasted_iota(jnp.int32, es[0].shape, 1) // N_HEADS
    o = None
    w_sum = None
    for e, o_ref in zip(es[:-1], o_refs[:-1]):
        packed = jnp.zeros(e.shape, BF16)
        for k, piece in enumerate(_split_f32(e / den)):
            packed = jnp.where(piece_of_lane == k, piece, packed)
        w = jnp.dot(packed, expand, preferred_element_type=F32)
        w_sum = w if w_sum is None else w_sum + w
        o_g = _unpack_rows(o_ref[...]).astype(F32)
        o = w * o_g if o is None else o + w * o_g
    o = o + (1.0 - w_sum) * _unpack_rows(o_refs[-1][...]).astype(F32)
    return o.astype(BF16)


def _post_kernel(*refs, n_groups, final, ff_chunks, side_cast):
    refs = list(refs)
    x_ref = refs.pop(0)
    o_refs = [refs.pop(0) for _ in range(n_groups)]
    lse_refs = [refs.pop(0) for _ in range(n_groups)] if n_groups > 1 else []
    wo_ref, gf_ref, wg_ref, wu_ref, wd_ref = refs[:5]
    refs = refs[5:]
    gfin_ref = refs.pop(0) if final else None
    if side_cast:
        src_ref, scale_ref, out_ref, dst_ref = refs
        dst_ref[...] = (src_ref[...] * scale_ref[...]).astype(dst_ref.dtype)
    else:
        (out_ref,) = refs

    o = _merge_groups(o_refs, lse_refs) if n_groups > 1 else o_refs[0][...]
    x1 = x_ref[...] + jnp.dot(o, wo_ref[...], preferred_element_type=F32)
    h = (x1 * gf_ref[...]).astype(BF16)
    r = lax.rsqrt(jnp.mean(x1 * x1, axis=-1, keepdims=True) + RMS_EPS)
    acc = x1
    lo = 0
    for fc in ff_chunks:
        gate = r * jnp.dot(h, wg_ref[:, lo:lo + fc], preferred_element_type=F32)
        up = r * jnp.dot(h, wu_ref[:, lo:lo + fc], preferred_element_type=F32)
        a = (gate / (1.0 + jnp.exp(-gate)) * up).astype(BF16)
        acc = acc + jnp.dot(a, wd_ref[lo:lo + fc, :], preferred_element_type=F32)
        lo += fc
    if final:
        acc = _rms(acc, gfin_ref[...])
    out_ref[...] = acc


def _ff_chunks(d_ff, n):
    tiles, rem = divmod(d_ff, MXU_DIM)
    assert rem == 0
    return tuple((tiles // n + (k < tiles % n)) * MXU_DIM for k in range(n))


def _layer_spec(shape, layer):
    return pl.BlockSpec((None,) + tuple(shape[1:]), lambda *_: (layer,) + (0,) * (len(shape) - 1),
                        pipeline_mode=pl.Buffered(1))


def _post(x2d, os, lses, wo, gf, wg, wu, wd, layer, gfin, tm, ff_chunks, side_cast=None):
    t, d = x2d.shape
    n_steps = t // tm
    ff_chunks = _ff_chunks(wg.shape[2], ff_chunks)
    final = gfin is not None
    row = lambda n: pl.BlockSpec((tm, n), lambda i: (i, 0))
    packed = lambda n: pl.BlockSpec((tm // 2, n), lambda i: (i, 0))
    in_specs = [row(d)] + ([packed(d)] * len(os) + [packed(LANES)] * len(lses) if lses else [row(d)])
    in_specs += [_const_spec((d, d))] + [_layer_spec(a.shape, layer) for a in (gf, wg, wu, wd)]
    args = [x2d, *os, *lses, wo, gf, wg, wu, wd]
    out_specs = [row(d)]
    out_shape = [jax.ShapeDtypeStruct((t, d), F32)]
    if final:
        in_specs.append(_const_spec((1, d)))
        args.append(gfin.reshape(1, d))
    if side_cast:
        w, scale = side_cast
        in_specs += [_row_chunk_spec(w, n_steps), _const_spec((1, w.shape[1]))]
        args += [w, scale.reshape(1, -1)]
        out_specs.append(_row_chunk_spec(w, n_steps))
        out_shape.append(jax.ShapeDtypeStruct(w.shape, BF16))
    return pl.pallas_call(
        functools.partial(_post_kernel, n_groups=len(os), final=final, ff_chunks=ff_chunks,
                          side_cast=bool(side_cast)),
        grid=(n_steps,),
        in_specs=in_specs,
        out_specs=out_specs,
        out_shape=out_shape,
        compiler_params=_params("parallel"),
    )(*args)


def _q_column_scale(n_cols, is_query_col):
    cols = np.arange(n_cols)
    return jnp.asarray(np.where(is_query_col(cols), HEAD_DIM ** -0.5 * LOG2E, 1.0), F32)


def kernel(x, norm_mix, norm_ffn, w_qkv_a, w_out_a, sink_a, w_qkv_b, w_out_b,
           w_gate, w_up, w_down, norm_final):
    b, s, d = x.shape
    t = b * s
    dq = N_HEADS * HEAD_DIM
    d_ff = w_gate.shape[2]

    wa = w_qkv_a[0] * _q_column_scale(w_qkv_a.shape[2], lambda c: c < dq)
    wq = wa[:, :dq].reshape(d, KV_A, HEADS_PER_TILE, HEAD_DIM).transpose(0, 2, 1, 3).reshape(d, dq)
    wa = jnp.concatenate([wq, wa[:, dq:]], axis=1).astype(BF16)
    wo_a = w_out_a[0].reshape(KV_A, HEADS_PER_TILE, HEAD_DIM, d).transpose(1, 0, 2, 3).reshape(dq, d)
    x2d = x.reshape(t, d)
    qkv, wg, wu, wd = _norm_matmul(
        x2d, norm_mix[0], wa, tm=2048,
        casts=(w_gate.reshape(-1, d_ff), w_up.reshape(-1, d_ff), w_down.reshape(-1, d)))
    ffn = (norm_ffn.reshape(-1, 1, d), wg.reshape(w_gate.shape), wu.reshape(w_up.shape), wd.reshape(w_down.shape))
    o_a = _attention_a(qkv.reshape(b, s // 2, -1), sink_a[0])
    wb_scale = _q_column_scale(w_qkv_b.shape[2], lambda c: c % (3 * dq) < dq)
    x2d, wb = _post(x2d, [o_a.reshape(t, d)], [], wo_a.astype(BF16), *ffn, 0, None, tm=1024, ff_chunks=4,
                    side_cast=(w_qkv_b[0], wb_scale))

    dils = tuple(dil for _, dil in DILATED_GROUPS)
    qkvs = _norm_matmul_dilated(x2d.reshape(b, s, d), norm_mix[1], wb, dils)
    os, lses = [], []
    for qkv_g, (win, dil) in zip(qkvs, DILATED_GROUPS):
        o_g, lse_g = _attention_b(qkv_g, dil=dil, window=win // (2 * dil))
        os.append(o_g)
        lses.append(lse_g)
    (out,) = _post(x2d, os, lses, w_out_b[0].astype(BF16), *ffn, 1, norm_final, tm=512, ff_chunks=2)
    return out.reshape(b, s, d)
```

```python
import functools

import jax
import jax.numpy as jnp
import numpy as np
from jax import lax
from jax.experimental import pallas as pl
from jax.experimental.pallas import tpu as pltpu

D_MODEL = 1024
HEAD_DIM = 64
N_HEADS = 16
KV_A = 4
WINDOW_A = 128
DILATED_GROUPS = ((128, 1), (512, 4), (2048, 16))
RMS_EPS = 1e-6
NEG = -1e30
LANES = 128
MXU_DIM = 256
HEADS_PER_TILE = MXU_DIM // HEAD_DIM
TQ = 128
A_TILES = 16
B_TILES = 16
A_DEPTH = 4
B_DEPTH = 6
A_HEADS = 4
B_HEADS = 2
B_COLS = B_HEADS * HEAD_DIM
B_STAGES = N_HEADS // B_HEADS
SPAN = 512
REGROUP = MXU_DIM
LSE_PARTS = 3
BF16_TILE_ROWS = 16
DEN_ROWS = BF16_TILE_ROWS
LOG2E = 1.4426950408889634
VMEM_LIMIT = 56 * 1024 * 1024

F32 = jnp.float32
BF16 = jnp.bfloat16


def _rms(x, g):
    ms = jnp.mean(x * x, axis=-1, keepdims=True)
    return x * lax.rsqrt(ms + RMS_EPS) * g


def _alibi_slopes(n):
    return [2.0 ** (-8.0 * (i + 1) / n) for i in range(n)]


def _const_spec(shape):
    return pl.BlockSpec(shape, lambda *_: (0,) * len(shape), pipeline_mode=pl.Buffered(1))


def _params(*sem):
    return pltpu.CompilerParams(dimension_semantics=sem, vmem_limit_bytes=VMEM_LIMIT)


def _grid_transpose_perm(n, n_major, n_minor):
    row = lax.broadcasted_iota(jnp.int32, (n, n), 0)
    col = lax.broadcasted_iota(jnp.int32, (n, n), 1)
    src = (row % n_minor) * n_major + row // n_minor
    return jnp.where(col == src, 1.0, 0.0).astype(BF16)


def _permute_rows(perm, x):
    return jnp.dot(perm, x, preferred_element_type=F32).astype(BF16)


def _pack_rows(x):
    return pltpu.bitcast(x, jnp.int32)


def _unpack_rows(x):
    return pltpu.bitcast(x, BF16)


def _row_chunk_spec(a2d, n_steps):
    rows, rem = divmod(a2d.shape[0], n_steps)
    assert rem == 0 and rows % BF16_TILE_ROWS == 0
    return pl.BlockSpec((rows, a2d.shape[1]), lambda i: (i, 0))


def _norm_matmul_kernel(x_ref, g_ref, w_ref, *rest):
    n_casts = (len(rest) - 1) // 2
    o_ref = rest[n_casts]
    h = _rms(x_ref[...], g_ref[...]).astype(BF16)
    o_ref[...] = _pack_rows(jnp.dot(h, w_ref[...], preferred_element_type=F32).astype(BF16))
    for src_ref, dst_ref in zip(rest[:n_casts], rest[n_casts + 1:]):
        dst_ref[...] = src_ref[...].astype(dst_ref.dtype)


def _norm_matmul(x2d, g, w, tm, casts=()):
    t, d = x2d.shape
    n = w.shape[1]
    n_steps = t // tm
    cast_specs = [_row_chunk_spec(a, n_steps) for a in casts]
    return pl.pallas_call(
        _norm_matmul_kernel,
        grid=(n_steps,),
        in_specs=[pl.BlockSpec((tm, d), lambda i: (i, 0)), _const_spec((1, d)), _const_spec((d, n))] + cast_specs,
        out_specs=[pl.BlockSpec((tm // 2, n), lambda i: (i, 0))] + cast_specs,
        out_shape=[jax.ShapeDtypeStruct((t // 2, n), jnp.int32)]
        + [jax.ShapeDtypeStruct(a.shape, BF16) for a in casts],
        compiler_params=_params("parallel"),
    )(x2d, g.reshape(1, d), w, *casts)


def _norm_matmul_dilated_kernel(x_ref, g_ref, w_ref, *o_refs, dils):
    h = _rms(x_ref[...], g_ref[...]).astype(BF16)
    n = o_refs[0].shape[-1]
    for gi, (dil, o_ref) in enumerate(zip(dils, o_refs)):
        hg = h if dil == 1 else _to_classes(h, dil)
        y = jnp.dot(hg, w_ref[:, gi * n:(gi + 1) * n], preferred_element_type=F32).astype(BF16)
        o_ref[0] = _pack_rows(y).reshape(o_ref.shape[1:])


def _to_classes(h, dil):
    per = REGROUP // dil
    perm = _grid_transpose_perm(REGROUP, dil, per)
    parts = [_permute_rows(perm, h[b * REGROUP:(b + 1) * REGROUP]) for b in range(SPAN // REGROUP)]
    return jnp.concatenate([p[r * per:(r + 1) * per] for r in range(dil) for p in parts], axis=0)


def _norm_matmul_dilated(x, g, w, dils):
    b, s, d = x.shape
    n = w.shape[1] // len(dils)
    out_specs = [pl.BlockSpec((1, dil, SPAN // dil // 2, n), lambda bi, i: (bi, 0, i, 0)) for dil in dils]
    out_shape = [jax.ShapeDtypeStruct((b, dil, s // dil // 2, n), jnp.int32) for dil in dils]
    return pl.pallas_call(
        functools.partial(_norm_matmul_dilated_kernel, dils=dils),
        grid=(b, s // SPAN),
        in_specs=[pl.BlockSpec((None, SPAN, d), lambda bi, i: (bi, i, 0)),
                  _const_spec((1, d)), _const_spec(w.shape)],
        out_specs=out_specs,
        out_shape=out_shape,
        compiler_params=_params("parallel", "parallel"),
    )(x, g.reshape(1, d), w)


def _lane_group(shape):
    return lax.broadcasted_iota(jnp.int32, shape, 1) // HEAD_DIM


def _fill_bias(bias_ref, heads, *, tk, window, unit, offsets):
    slopes = _alibi_slopes(N_HEADS)
    key = lax.broadcasted_iota(jnp.int32, (tk, TQ), 0)
    qry = lax.broadcasted_iota(jnp.int32, (tk, TQ), 1)
    for v, off in enumerate(offsets):
        dist = jnp.abs(key - qry - off)
        negd = jnp.where(dist <= window, -(dist.astype(F32) * float(unit)), NEG)
        for c in range(N_HEADS // heads):
            for j in range(heads):
                bias_ref[v, c, :, j * TQ:(j + 1) * TQ] = (slopes[heads * c + j] * LOG2E) * negd


def _stack_heads(tiles, groups):
    lg = _lane_group(tiles[0].shape)
    return jnp.concatenate(
        [jnp.where(lg == g, t, jnp.zeros_like(t)) for t, g in zip(tiles, groups)], axis=0)


def _scores(q_stack, k, bias):
    return lax.dot_general(k, q_stack, (((1,), (1,)), ((), ())), preferred_element_type=F32) + bias


def _with_ones(vt):
    return jnp.concatenate([vt, jnp.ones((DEN_ROWS, vt.shape[1]), vt.dtype)], axis=0)


def _softmax_pv(sc, vt1):
    m = jnp.max(sc, axis=0, keepdims=True)
    p = jnp.exp2(sc - m)
    pv = jnp.dot(vt1, p.astype(BF16), preferred_element_type=F32)
    rows = vt1.shape[0] - DEN_ROWS
    return pv[:rows], m, pv[rows:rows + 1]


def _pipeline(stages, start_fn, finish_fn, depth):
    started = [start_fn(st) for st in stages[:depth]]
    for k, st in enumerate(stages):
        cur = started.pop(0)
        if k + depth < len(stages):
            started.append(start_fn(stages[k + depth]))
        finish_fn(st, cur)


def _window_start(q0, window, tk, seq_len):
    start = jnp.clip(q0 - window, 0, seq_len - tk)
    return pl.multiple_of(start, min(window, TQ)), (q0 - start) // window


def _split_f32(x):
    parts = []
    for _ in range(LSE_PARTS):
        p = x.astype(BF16)
        parts.append(p)
        x = x - p.astype(F32)
    return parts


def _attn_b_tiles(qkv_ref, tiles, bias_ref, out_refs, *, tk, window, seq_len):
    dq = N_HEADS * HEAD_DIM
    head_row = lax.broadcasted_iota(jnp.int32, (N_HEADS, TQ), 0)
    windows = [_window_start(q0, window, tk, seq_len) for _, _, q0, _ in tiles]
    acc = {}

    def start(stage):
        t, c = stage
        idx, r0, _, _ = tiles[t]
        win0, var = windows[t]
        lo = c * B_COLS
        q_rows = pl.ds(pl.multiple_of(r0 // 2, TQ // 2), TQ // 2)
        kv_rows = pl.ds(pl.multiple_of(win0 // 2, min(window, TQ) // 2), tk // 2)
        q = _unpack_rows(qkv_ref[idx + (q_rows, slice(lo, lo + B_COLS))])
        k = _unpack_rows(qkv_ref[idx + (kv_rows, slice(dq + lo, dq + lo + B_COLS))])
        v = _unpack_rows(qkv_ref[idx + (kv_rows, slice(2 * dq + lo, 2 * dq + lo + B_COLS))])
        q_stack = _stack_heads([q] * B_HEADS, range(B_HEADS))
        return _scores(q_stack, k, bias_ref[var, c]), _with_ones(v.T)

    def finish(stage, started):
        t, c = stage
        pv, m, den = _softmax_pv(*started)
        rden = 1.0 / den
        lse = m + jnp.log2(den)
        outs, lse16 = acc.get(t, ([], jnp.zeros((N_HEADS, TQ), F32)))
        for j in range(B_HEADS):
            cols = slice(j * TQ, (j + 1) * TQ)
            outs.append((pv[j * HEAD_DIM:(j + 1) * HEAD_DIM, cols] * rden[:, cols]).astype(BF16))
            lse16 = jnp.where(head_row == B_HEADS * c + j, lse[:, cols], lse16)
        acc[t] = (outs, lse16)
        if c == B_STAGES - 1:
            pad = jnp.zeros((LANES - LSE_PARTS * N_HEADS, TQ), BF16)
            tile_t = jnp.concatenate(outs + _split_f32(lse16) + [pad], axis=0)
            s0, lo = tiles[t][3], 0
            tile = tile_t.T
            for ref in out_refs:
                piece = tile[:, lo:lo + ref.shape[1]]
                if ref.dtype == jnp.int32:
                    ref[pl.ds(pl.multiple_of(s0 // 2, TQ // 2), TQ // 2), :] = _pack_rows(piece)
                else:
                    ref[pl.ds(s0, TQ), :] = piece
                lo += ref.shape[1]

    _pipeline([(t, c) for t in range(len(tiles)) for c in range(B_STAGES)], start, finish, B_DEPTH)


def _attn_b_kernel(qkv_ref, o_ref, lse_ref, bias_ref, *stage, seq_len, tk, window, unit, offsets, dil):
    @pl.when(pl.program_id(0) == 0)
    def _():
        _fill_bias(bias_ref, B_HEADS, tk=tk, window=window, unit=unit, offsets=offsets)

    dq = N_HEADS * HEAD_DIM
    n_tiles = dil * seq_len // TQ
    out_refs = (o_ref, lse_ref) if dil == 1 else stage

    def tile_group(g, carry):
        tiles = []
        for u in range(B_TILES):
            t = g * B_TILES + u
            s0 = pl.multiple_of(t * TQ, TQ)
            if dil == 1:
                tiles.append(((), s0, s0, s0))
            else:
                q0 = pl.multiple_of((t // dil) * TQ, TQ)
                tiles.append(((t % dil,), q0, q0, s0))
        _attn_b_tiles(qkv_ref, tiles, bias_ref, out_refs, tk=tk, window=window, seq_len=seq_len)
        return carry

    lax.fori_loop(0, n_tiles // B_TILES, tile_group, 0)

    if dil > 1:
        (stage_ref,) = stage
        per_cls = REGROUP // dil
        perm = _grid_transpose_perm(REGROUP, per_cls, dil)
        for k in range(n_tiles * TQ // REGROUP):
            span, off = divmod(k * per_cls, TQ)
            rows = [stage_ref[(span * dil + r) * TQ + off:(span * dil + r) * TQ + off + per_cls, :]
                    for r in range(dil)]
            nat = _permute_rows(perm, jnp.concatenate(rows, axis=0))
            o_ref[k * REGROUP // 2:(k + 1) * REGROUP // 2, :] = _pack_rows(nat[:, :dq])
            lse_ref[k * REGROUP // 2:(k + 1) * REGROUP // 2, :] = _pack_rows(nat[:, dq:])


def _attention_b(qkv, *, dil, window):
    batch, _, packed_len, width = qkv.shape
    seq_len = 2 * packed_len
    dq = N_HEADS * HEAD_DIM
    seq = seq_len * dil
    tk = min(TQ + 2 * window, seq_len)
    offsets = (0,) if tk == seq_len else (0, window, 2 * window)
    if dil == 1:
        in_spec = pl.BlockSpec((None, None, packed_len, width), lambda b: (b, 0, 0, 0))
    else:
        in_spec = pl.BlockSpec((None, dil, packed_len, width), lambda b: (b, 0, 0, 0))
    kern = functools.partial(_attn_b_kernel, seq_len=seq_len, tk=tk, window=window, unit=dil,
                             offsets=offsets, dil=dil)
    return pl.pallas_call(
        kern,
        grid=(batch,),
        in_specs=[in_spec],
        out_specs=[pl.BlockSpec((seq // 2, dq), lambda b: (b, 0)), pl.BlockSpec((seq // 2, LANES), lambda b: (b, 0))],
        out_shape=[jax.ShapeDtypeStruct((batch * seq // 2, dq), jnp.int32),
                   jax.ShapeDtypeStruct((batch * seq // 2, LANES), jnp.int32)],
        scratch_shapes=[pltpu.VMEM((len(offsets), B_STAGES, tk, B_HEADS * TQ), F32)]
        + ([pltpu.VMEM((seq, dq + LANES), BF16)] if dil > 1 else []),
        compiler_params=_params("arbitrary"),
    )(qkv)


def _attn_a_kernel(sink_ref, q_ref, k_ref, v_ref, o_ref, bias_ref, *, seq_len, tk, window, offsets):
    @pl.when(pl.program_id(0) == 0)
    def _():
        _fill_bias(bias_ref, HEADS_PER_TILE, tk=tk, window=window, unit=1, offsets=offsets)

    lax.fori_loop(0, seq_len // (A_TILES * TQ),
                  functools.partial(_attn_a_tile_group, sink_ref, q_ref, k_ref, v_ref, o_ref, bias_ref,
                                    seq_len=seq_len, tk=tk, window=window), 0)


def _attn_a_tile_group(sink_ref, q_ref, k_ref, v_ref, o_ref, bias_ref, g, carry, *, seq_len, tk, window):
    qblk = lax.broadcasted_iota(jnp.int32, (1, A_HEADS * TQ), 1) // TQ
    rows = [pl.multiple_of((g * A_TILES + t) * TQ, TQ) for t in range(A_TILES)]
    tiles = [_window_start(q0, window, tk, seq_len) for q0 in rows]
    shared = {}
    blocks = {}

    def start(stage):
        t, c, half = stage
        win0, var = tiles[t]
        if (c, half) == (0, 0):
            q_rows = pl.ds(pl.multiple_of(rows[t] // 2, TQ // 2), TQ // 2)
            kv_rows = pl.ds(pl.multiple_of(win0 // 2, min(window, TQ) // 2), tk // 2)
            qs = [_unpack_rows(q_ref[q_rows, j * MXU_DIM:(j + 1) * MXU_DIM]) for j in range(HEADS_PER_TILE)]
            shared[t] = (qs, _unpack_rows(k_ref[kv_rows, :]), _unpack_rows(v_ref[kv_rows, :]).T)
        qs, k4, vt = shared[t]
        vt_c = _with_ones(vt[c * HEAD_DIM:(c + 1) * HEAD_DIM])
        q_stack = _stack_heads(qs[half * A_HEADS:(half + 1) * A_HEADS], [c] * A_HEADS)
        bias = bias_ref[var, c, :, half * A_HEADS * TQ:(half + 1) * A_HEADS * TQ]
        return _scores(q_stack, k4, bias), vt_c

    def finish(stage, started):
        t, c, half = stage
        pv, m, den = _softmax_pv(*started)
        sink = jnp.zeros_like(m)
        for jj in range(A_HEADS):
            sink = jnp.where(qblk == jj, sink_ref[HEADS_PER_TILE * c + half * A_HEADS + jj] * LOG2E, sink)
        gate = 1.0 / (den + jnp.exp2(sink - m))
        for jj in range(A_HEADS):
            cols = slice(jj * TQ, (jj + 1) * TQ)
            blocks[t, half * A_HEADS + jj, c] = (pv[:, cols] * gate[:, cols]).astype(o_ref.dtype)
        if (c, half) == (KV_A - 1, HEADS_PER_TILE // A_HEADS - 1):
            tile_t = jnp.concatenate(
                [blocks.pop((t, j, cc)) for j in range(HEADS_PER_TILE) for cc in range(KV_A)], axis=0)
            o_ref[pl.ds(rows[t], TQ), :] = tile_t.T

    _pipeline([(t, c, half) for t in range(A_TILES) for c in range(KV_A)
               for half in range(HEADS_PER_TILE // A_HEADS)], start, finish, A_DEPTH)
    return carry


def _attention_a(qkv, sink):
    b, packed_s, _ = qkv.shape
    s = 2 * packed_s
    dq = N_HEADS * HEAD_DIM
    dkv = KV_A * HEAD_DIM
    tk = TQ + 2 * WINDOW_A
    offsets = (0, WINDOW_A, 2 * WINDOW_A)
    kern = functools.partial(_attn_a_kernel, seq_len=s, tk=tk, window=WINDOW_A, offsets=offsets)
    return pl.pallas_call(
        kern,
        grid=(b,),
        in_specs=[pl.BlockSpec(memory_space=pltpu.SMEM),
                  pl.BlockSpec((None, packed_s, dq), lambda bi: (bi, 0, 0)),
                  pl.BlockSpec((None, packed_s, dkv), lambda bi: (bi, 0, dq // dkv)),
                  pl.BlockSpec((None, packed_s, dkv), lambda bi: (bi, 0, dq // dkv + 1))],
        out_specs=pl.BlockSpec((None, s, dq), lambda bi: (bi, 0, 0)),
        out_shape=jax.ShapeDtypeStruct((b, s, dq), BF16),
        scratch_shapes=[pltpu.VMEM((len(offsets), KV_A, tk, HEADS_PER_TILE * TQ), F32)],
        compiler_params=_params("arbitrary"),
    )(sink, qkv, qkv, qkv)


def _merge_groups(o_refs, lse_refs):
    src = lax.broadcasted_iota(jnp.int32, (LANES, LANES), 0)
    dst = lax.broadcasted_iota(jnp.int32, (LANES, LANES), 1)
    fold = jnp.where((src % N_HEADS == dst % N_HEADS) & (src < LSE_PARTS * N_HEADS), 1.0, 0.0).astype(BF16)
    src = lax.broadcasted_iota(jnp.int32, (LANES, D_MODEL), 0)
    dst = lax.broadcasted_iota(jnp.int32, (LANES, D_MODEL), 1)
    expand = jnp.where((src % N_HEADS == dst // HEAD_DIM) & (src < LSE_PARTS * N_HEADS), 1.0, 0.0).astype(BF16)
    lses = [jnp.dot(_unpack_rows(r[...]), fold, preferred_element_type=F32) for r in lse_refs]
    mx = functools.reduce(jnp.maximum, lses)
    es = [jnp.exp2(l - mx) for l in lses]
    den = functools.reduce(lambda a, b: a + b, es)
    piece_of_lane = lax.broadcasted_iota(jnp.int32, es[0].shape, 1) // N_HEADS
    o = None
    w_sum = None
    for e, o_ref in zip(es[:-1], o_refs[:-1]):
        packed = jnp.zeros(e.shape, BF16)
        for k, piece in enumerate(_split_f32(e / den)):
            packed = jnp.where(piece_of_lane == k, piece, packed)
        w = jnp.dot(packed, expand, preferred_element_type=F32)
        w_sum = w if w_sum is None else w_sum + w
        o_g = _unpack_rows(o_ref[...]).astype(F32)
        o = w * o_g if o is None else o + w * o_g
    o = o + (1.0 - w_sum) * _unpack_rows(o_refs[-1][...]).astype(F32)
    return o.astype(BF16)


def _post_kernel(*refs, n_groups, final, ff_chunks, side_cast):
    refs = list(refs)
    x_ref = refs.pop(0)
    o_refs = [refs.pop(0) for _ in range(n_groups)]
    lse_refs = [refs.pop(0) for _ in range(n_groups)] if n_groups > 1 else []
    wo_ref, gf_ref, wg_ref, wu_ref, wd_ref = refs[:5]
    refs = refs[5:]
    gfin_ref = refs.pop(0) if final else None
    if side_cast:
        src_ref, scale_ref, out_ref, dst_ref = refs
        dst_ref[...] = (src_ref[...] * scale_ref[...]).astype(dst_ref.dtype)
    else:
        (out_ref,) = refs

    o = _merge_groups(o_refs, lse_refs) if n_groups > 1 else o_refs[0][...]
    x1 = x_ref[...] + jnp.dot(o, wo_ref[...], preferred_element_type=F32)
    h = (x1 * gf_ref[...]).astype(BF16)
    r = lax.rsqrt(jnp.mean(x1 * x1, axis=-1, keepdims=True) + RMS_EPS)
    acc = x1
    lo = 0
    for fc in ff_chunks:
        gate = r * jnp.dot(h, wg_ref[:, lo:lo + fc], preferred_element_type=F32)
        up = r * jnp.dot(h, wu_ref[:, lo:lo + fc], preferred_element_type=F32)
        a = (gate / (1.0 + jnp.exp(-gate)) * up).astype(BF16)
        acc = acc + jnp.dot(a, wd_ref[lo:lo + fc, :], preferred_element_type=F32)
        lo += fc
    if final:
        acc = _rms(acc, gfin_ref[...])
    out_ref[...] = acc


def _ff_chunks(d_ff, n):
    tiles, rem = divmod(d_ff, MXU_DIM)
    assert rem == 0
    return tuple((tiles // n + (k < tiles % n)) * MXU_DIM for k in range(n))


def _layer_spec(shape, layer):
    return pl.BlockSpec((None,) + tuple(shape[1:]), lambda *_: (layer,) + (0,) * (len(shape) - 1),
                        pipeline_mode=pl.Buffered(1))


def _post(x2d, os, lses, wo, gf, wg, wu, wd, layer, gfin, tm, ff_chunks, side_cast=None):
    t, d = x2d.shape
    n_steps = t // tm
    ff_chunks = _ff_chunks(wg.shape[2], ff_chunks)
    final = gfin is not None
    row = lambda n: pl.BlockSpec((tm, n), lambda i: (i, 0))
    packed = lambda n: pl.BlockSpec((tm // 2, n), lambda i: (i, 0))
    in_specs = [row(d)] + ([packed(d)] * len(os) + [packed(LANES)] * len(lses) if lses else [row(d)])
    in_specs += [_const_spec((d, d))] + [_layer_spec(a.shape, layer) for a in (gf, wg, wu, wd)]
    args = [x2d, *os, *lses, wo, gf, wg, wu, wd]
    out_specs = [row(d)]
    out_shape = [jax.ShapeDtypeStruct((t, d), F32)]
    if final:
        in_specs.append(_const_spec((1, d)))
        args.append(gfin.reshape(1, d))
    if side_cast:
        w, scale = side_cast
        in_specs += [_row_chunk_spec(w, n_steps), _const_spec((1, w.shape[1]))]
        args += [w, scale.reshape(1, -1)]
        out_specs.append(_row_chunk_spec(w, n_steps))
        out_shape.append(jax.ShapeDtypeStruct(w.shape, BF16))
    return pl.pallas_call(
        functools.partial(_post_kernel, n_groups=len(os), final=final, ff_chunks=ff_chunks,
                          side_cast=bool(side_cast)),
        grid=(n_steps,),
        in_specs=in_specs,
        out_specs=out_specs,
        out_shape=out_shape,
        compiler_params=_params("parallel"),
    )(*args)


def _q_column_scale(n_cols, is_query_col):
    cols = np.arange(n_cols)
    return jnp.asarray(np.where(is_query_col(cols), HEAD_DIM ** -0.5 * LOG2E, 1.0), F32)


def kernel(x, norm_mix, norm_ffn, w_qkv_a, w_out_a, sink_a, w_qkv_b, w_out_b,
           w_gate, w_up, w_down, norm_final):
    b, s, d = x.shape
    t = b * s
    dq = N_HEADS * HEAD_DIM
    d_ff = w_gate.shape[2]

    wa = w_qkv_a[0] * _q_column_scale(w_qkv_a.shape[2], lambda c: c < dq)
    wq = wa[:, :dq].reshape(d, KV_A, HEADS_PER_TILE, HEAD_DIM).transpose(0, 2, 1, 3).reshape(d, dq)
    wa = jnp.concatenate([wq, wa[:, dq:]], axis=1).astype(BF16)
    wo_a = w_out_a[0].reshape(KV_A, HEADS_PER_TILE, HEAD_DIM, d).transpose(1, 0, 2, 3).reshape(dq, d)
    x2d = x.reshape(t, d)
    qkv, wg, wu, wd = _norm_matmul(
        x2d, norm_mix[0], wa, tm=2048,
        casts=(w_gate.reshape(-1, d_ff), w_up.reshape(-1, d_ff), w_down.reshape(-1, d)))
    ffn = (norm_ffn.reshape(-1, 1, d), wg.reshape(w_gate.shape), wu.reshape(w_up.shape), wd.reshape(w_down.shape))
    o_a = _attention_a(qkv.reshape(b, s // 2, -1), sink_a[0])
    wb_scale = _q_column_scale(w_qkv_b.shape[2], lambda c: c % (3 * dq) < dq)
    x2d, wb = _post(x2d, [o_a.reshape(t, d)], [], wo_a.astype(BF16), *ffn, 0, None, tm=1024, ff_chunks=4,
                    side_cast=(w_qkv_b[0], wb_scale))

    dils = tuple(dil for _, dil in DILATED_GROUPS)
    qkvs = _norm_matmul_dilated(x2d.reshape(b, s, d), norm_mix[1], wb, dils)
    os, lses = [], []
    for qkv_g, (win, dil) in zip(qkvs, DILATED_GROUPS):
        o_g, lse_g = _attention_b(qkv_g, dil=dil, window=win // (2 * dil))
        os.append(o_g)
        lses.append(lse_g)
    (out,) = _post(x2d, os, lses, w_out_b[0].astype(BF16), *ffn, 1, norm_final, tm=512, ff_chunks=2)
    return out.reshape(b, s, d)
```

```python
import functools

import jax
import jax.numpy as jnp
import numpy as np
from jax import lax
from jax.experimental import pallas as pl
from jax.experimental.pallas import tpu as pltpu

D_MODEL = 1024
HEAD_DIM = 64
N_HEADS = 16
KV_A = 4
WINDOW_A = 128
DILATED_GROUPS = ((128, 1), (512, 4), (2048, 16))
RMS_EPS = 1e-6
NEG = -1e30
LANES = 128
MXU_DIM = 256
HEADS_PER_TILE = MXU_DIM // HEAD_DIM
TQ = 128
A_TILES = 16
B_TILES = 16
A_DEPTH = 4
B_DEPTH = 6
A_HEADS = 4
B_HEADS = 2
B_COLS = B_HEADS * HEAD_DIM
B_STAGES = N_HEADS // B_HEADS
SPAN = 512
REGROUP = MXU_DIM
LSE_PARTS = 3
BF16_TILE_ROWS = 16
DEN_ROWS = BF16_TILE_ROWS
LOG2E = 1.4426950408889634
VMEM_LIMIT = 56 * 1024 * 1024

F32 = jnp.float32
BF16 = jnp.bfloat16


def _rms(x, g):
    ms = jnp.mean(x * x, axis=-1, keepdims=True)
    return x * lax.rsqrt(ms + RMS_EPS) * g


def _alibi_slopes(n):
    return [2.0 ** (-8.0 * (i + 1) / n) for i in range(n)]


def _const_spec(shape):
    return pl.BlockSpec(shape, lambda *_: (0,) * len(shape), pipeline_mode=pl.Buffered(1))


def _params(*sem):
    return pltpu.CompilerParams(dimension_semantics=sem, vmem_limit_bytes=VMEM_LIMIT)


def _grid_transpose_perm(n, n_major, n_minor):
    row = lax.broadcasted_iota(jnp.int32, (n, n), 0)
    col = lax.broadcasted_iota(jnp.int32, (n, n), 1)
    src = (row % n_minor) * n_major + row // n_minor
    return jnp.where(col == src, 1.0, 0.0).astype(BF16)


def _permute_rows(perm, x):
    return jnp.dot(perm, x, preferred_element_type=F32).astype(BF16)


def _pack_rows(x):
    return pltpu.bitcast(x, jnp.int32)


def _unpack_rows(x):
    return pltpu.bitcast(x, BF16)


def _row_chunk_spec(a2d, n_steps):
    rows, rem = divmod(a2d.shape[0], n_steps)
    assert rem == 0 and rows % BF16_TILE_ROWS == 0
    return pl.BlockSpec((rows, a2d.shape[1]), lambda i: (i, 0))


def _norm_matmul_kernel(x_ref, g_ref, w_ref, *rest):
    n_casts = (len(rest) - 1) // 2
    o_ref = rest[n_casts]
    h = _rms(x_ref[...], g_ref[...]).astype(BF16)
    o_ref[...] = _pack_rows(jnp.dot(h, w_ref[...], preferred_element_type=F32).astype(BF16))
    for src_ref, dst_ref in zip(rest[:n_casts], rest[n_casts + 1:]):
        dst_ref[...] = src_ref[...].astype(dst_ref.dtype)


def _norm_matmul(x2d, g, w, tm, casts=()):
    t, d = x2d.shape
    n = w.shape[1]
    n_steps = t // tm
    cast_specs = [_row_chunk_spec(a, n_steps) for a in casts]
    return pl.pallas_call(
        _norm_matmul_kernel,
        grid=(n_steps,),
        in_specs=[pl.BlockSpec((tm, d), lambda i: (i, 0)), _const_spec((1, d)), _const_spec((d, n))] + cast_specs,
        out_specs=[pl.BlockSpec((tm // 2, n), lambda i: (i, 0))] + cast_specs,
        out_shape=[jax.ShapeDtypeStruct((t // 2, n), jnp.int32)]
        + [jax.ShapeDtypeStruct(a.shape, BF16) for a in casts],
        compiler_params=_params("parallel"),
    )(x2d, g.reshape(1, d), w, *casts)


def _norm_matmul_dilated_kernel(x_ref, g_ref, w_ref, *o_refs, dils):
    h = _rms(x_ref[...], g_ref[...]).astype(BF16)
    n = o_refs[0].shape[-1]
    for gi, (dil, o_ref) in enumerate(zip(dils, o_refs)):
        hg = h if dil == 1 else _to_classes(h, dil)
        y = jnp.dot(hg, w_ref[:, gi * n:(gi + 1) * n], preferred_element_type=F32).astype(BF16)
        o_ref[0] = _pack_rows(y).reshape(o_ref.shape[1:])


def _to_classes(h, dil):
    per = REGROUP // dil
    perm = _grid_transpose_perm(REGROUP, dil, per)
    parts = [_permute_rows(perm, h[b * REGROUP:(b + 1) * REGROUP]) for b in range(SPAN // REGROUP)]
    return jnp.concatenate([p[r * per:(r + 1) * per] for r in range(dil) for p in parts], axis=0)


def _norm_matmul_dilated(x, g, w, dils):
    b, s, d = x.shape
    n = w.shape[1] // len(dils)
    out_specs = [pl.BlockSpec((1, dil, SPAN // dil // 2, n), lambda bi, i: (bi, 0, i, 0)) for dil in dils]
    out_shape = [jax.ShapeDtypeStruct((b, dil, s // dil // 2, n), jnp.int32) for dil in dils]
    return pl.pallas_call(
        functools.partial(_norm_matmul_dilated_kernel, dils=dils),
        grid=(b, s // SPAN),
        in_specs=[pl.BlockSpec((None, SPAN, d), lambda bi, i: (bi, i, 0)),
                  _const_spec((1, d)), _const_spec(w.shape)],
        out_specs=out_specs,
        out_shape=out_shape,
        compiler_params=_params("parallel", "parallel"),
    )(x, g.reshape(1, d), w)


def _lane_group(shape):
    return lax.broadcasted_iota(jnp.int32, shape, 1) // HEAD_DIM


def _fill_bias(bias_ref, heads, *, tk, window, unit, offsets):
    slopes = _alibi_slopes(N_HEADS)
    key = lax.broadcasted_iota(jnp.int32, (tk, TQ), 0)
    qry = lax.broadcasted_iota(jnp.int32, (tk, TQ), 1)
    for v, off in enumerate(offsets):
        dist = jnp.abs(key - qry - off)
        negd = jnp.where(dist <= window, -(dist.astype(F32) * float(unit)), NEG)
        for c in range(N_HEADS // heads):
            for j in range(heads):
                bias_ref[v, c, :, j * TQ:(j + 1) * TQ] = (slopes[heads * c + j] * LOG2E) * negd


def _stack_heads(tiles, groups):
    lg = _lane_group(tiles[0].shape)
    return jnp.concatenate(
        [jnp.where(lg == g, t, jnp.zeros_like(t)) for t, g in zip(tiles, groups)], axis=0)


def _scores(q_stack, k, bias):
    return lax.dot_general(k, q_stack, (((1,), (1,)), ((), ())), preferred_element_type=F32) + bias


def _with_ones(vt):
    return jnp.concatenate([vt, jnp.ones((DEN_ROWS, vt.shape[1]), vt.dtype)], axis=0)


def _softmax_pv(sc, vt1):
    m = jnp.max(sc, axis=0, keepdims=True)
    p = jnp.exp2(sc - m)
    pv = jnp.dot(vt1, p.astype(BF16), preferred_element_type=F32)
    rows = vt1.shape[0] - DEN_ROWS
    return pv[:rows], m, pv[rows:rows + 1]


def _pipeline(stages, start_fn, finish_fn, depth):
    started = [start_fn(st) for st in stages[:depth]]
    for k, st in enumerate(stages):
        cur = started.pop(0)
        if k + depth < len(stages):
            started.append(start_fn(stages[k + depth]))
        finish_fn(st, cur)


def _window_start(q0, window, tk, seq_len):
    start = jnp.clip(q0 - window, 0, seq_len - tk)
    return pl.multiple_of(start, min(window, TQ)), (q0 - start) // window


def _split_f32(x):
    parts = []
    for _ in range(LSE_PARTS):
        p = x.astype(BF16)
        parts.append(p)
        x = x - p.astype(F32)
    return parts


def _attn_b_tiles(qkv_ref, tiles, bias_ref, out_refs, *, tk, window, seq_len):
    dq = N_HEADS * HEAD_DIM
    head_row = lax.broadcasted_iota(jnp.int32, (N_HEADS, TQ), 0)
    windows = [_window_start(q0, window, tk, seq_len) for _, _, q0, _ in tiles]
    acc = {}

    def start(stage):
        t, c = stage
        idx, r0, _, _ = tiles[t]
        win0, var = windows[t]
        lo = c * B_COLS
        q_rows = pl.ds(pl.multiple_of(r0 // 2, TQ // 2), TQ // 2)
        kv_rows = pl.ds(pl.multiple_of(win0 // 2, min(window, TQ) // 2), tk // 2)
        q = _unpack_rows(qkv_ref[idx + (q_rows, slice(lo, lo + B_COLS))])
        k = _unpack_rows(qkv_ref[idx + (kv_rows, slice(dq + lo, dq + lo + B_COLS))])
        v = _unpack_rows(qkv_ref[idx + (kv_rows, slice(2 * dq + lo, 2 * dq + lo + B_COLS))])
        q_stack = _stack_heads([q] * B_HEADS, range(B_HEADS))
        return _scores(q_stack, k, bias_ref[var, c]), _with_ones(v.T)

    def finish(stage, started):
        t, c = stage
        pv, m, den = _softmax_pv(*started)
        rden = 1.0 / den
        lse = m + jnp.log2(den)
        outs, lse16 = acc.get(t, ([], jnp.zeros((N_HEADS, TQ), F32)))
        for j in range(B_HEADS):
            cols = slice(j * TQ, (j + 1) * TQ)
            outs.append((pv[j * HEAD_DIM:(j + 1) * HEAD_DIM, cols] * rden[:, cols]).astype(BF16))
            lse16 = jnp.where(head_row == B_HEADS * c + j, lse[:, cols], lse16)
        acc[t] = (outs, lse16)
        if c == B_STAGES - 1:
            pad = jnp.zeros((LANES - LSE_PARTS * N_HEADS, TQ), BF16)
            tile_t = jnp.concatenate(outs + _split_f32(lse16) + [pad], axis=0)
            s0, lo = tiles[t][3], 0
            tile = tile_t.T
            for ref in out_refs:
                piece = tile[:, lo:lo + ref.shape[1]]
                if ref.dtype == jnp.int32:
                    ref[pl.ds(pl.multiple_of(s0 // 2, TQ // 2), TQ // 2), :] = _pack_rows(piece)
                else:
                    ref[pl.ds(s0, TQ), :] = piece
                lo += ref.shape[1]

    _pipeline([(t, c) for t in range(len(tiles)) for c in range(B_STAGES)], start, finish, B_DEPTH)


def _attn_b_kernel(qkv_ref, o_ref, lse_ref, bias_ref, *stage, seq_len, tk, window, unit, offsets, dil):
    @pl.when(pl.program_id(0) == 0)
    def _():
        _fill_bias(bias_ref, B_HEADS, tk=tk, window=window, unit=unit, offsets=offsets)

    dq = N_HEADS * HEAD_DIM
    n_tiles = dil * seq_len // TQ
    out_refs = (o_ref, lse_ref) if dil == 1 else stage

    def tile_group(g, carry):
        tiles = []
        for u in range(B_TILES):
            t = g * B_TILES + u
            s0 = pl.multiple_of(t * TQ, TQ)
            if dil == 1:
                tiles.append(((), s0, s0, s0))
            else:
                q0 = pl.multiple_of((t // dil) * TQ, TQ)
                tiles.append(((t % dil,), q0, q0, s0))
        _attn_b_tiles(qkv_ref, tiles, bias_ref, out_refs, tk=tk, window=window, seq_len=seq_len)
        return carry

    lax.fori_loop(0, n_tiles // B_TILES, tile_group, 0)

    if dil > 1:
        (stage_ref,) = stage
        per_cls = REGROUP // dil
        perm = _grid_transpose_perm(REGROUP, per_cls, dil)
        for k in range(n_tiles * TQ // REGROUP):
            span, off = divmod(k * per_cls, TQ)
            rows = [stage_ref[(span * dil + r) * TQ + off:(span * dil + r) * TQ + off + per_cls, :]
                    for r in range(dil)]
            nat = _permute_rows(perm, jnp.concatenate(rows, axis=0))
            o_ref[k * REGROUP // 2:(k + 1) * REGROUP // 2, :] = _pack_rows(nat[:, :dq])
            lse_ref[k * REGROUP // 2:(k + 1) * REGROUP // 2, :] = _pack_rows(nat[:, dq:])


def _attention_b(qkv, *, dil, window):
    batch, _, packed_len, width = qkv.shape
    seq_len = 2 * packed_len
    dq = N_HEADS * HEAD_DIM
    seq = seq_len * dil
    tk = min(TQ + 2 * window, seq_len)
    offsets = (0,) if tk == seq_len else (0, window, 2 * window)
    if dil == 1:
        in_spec = pl.BlockSpec((None, None, packed_len, width), lambda b: (b, 0, 0, 0))
    else:
        in_spec = pl.BlockSpec((None, dil, packed_len, width), lambda b: (b, 0, 0, 0))
    kern = functools.partial(_attn_b_kernel, seq_len=seq_len, tk=tk, window=window, unit=dil,
                             offsets=offsets, dil=dil)
    return pl.pallas_call(
        kern,
        grid=(batch,),
        in_specs=[in_spec],
        out_specs=[pl.BlockSpec((seq // 2, dq), lambda b: (b, 0)), pl.BlockSpec((seq // 2, LANES), lambda b: (b, 0))],
        out_shape=[jax.ShapeDtypeStruct((batch * seq // 2, dq), jnp.int32),
                   jax.ShapeDtypeStruct((batch * seq // 2, LANES), jnp.int32)],
        scratch_shapes=[pltpu.VMEM((len(offsets), B_STAGES, tk, B_HEADS * TQ), F32)]
        + ([pltpu.VMEM((seq, dq + LANES), BF16)] if dil > 1 else []),
        compiler_params=_params("arbitrary"),
    )(qkv)


def _attn_a_kernel(sink_ref, q_ref, k_ref, v_ref, o_ref, bias_ref, *, seq_len, tk, window, offsets):
    @pl.when(pl.program_id(0) == 0)
    def _():
        _fill_bias(bias_ref, HEADS_PER_TILE, tk=tk, window=window, unit=1, offsets=offsets)

    lax.fori_loop(0, seq_len // (A_TILES * TQ),
                  functools.partial(_attn_a_tile_group, sink_ref, q_ref, k_ref, v_ref, o_ref, bias_ref,
                                    seq_len=seq_len, tk=tk, window=window), 0)


def _attn_a_tile_group(sink_ref, q_ref, k_ref, v_ref, o_ref, bias_ref, g, carry, *, seq_len, tk, window):
    qblk = lax.broadcasted_iota(jnp.int32, (1, A_HEADS * TQ), 1) // TQ
    rows = [pl.multiple_of((g * A_TILES + t) * TQ, TQ) for t in range(A_TILES)]
    tiles = [_window_start(q0, window, tk, seq_len) for q0 in rows]
    shared = {}
    blocks = {}

    def start(stage):
        t, c, half = stage
        win0, var = tiles[t]
        if (c, half) == (0, 0):
            q_rows = pl.ds(pl.multiple_of(rows[t] // 2, TQ // 2), TQ // 2)
            kv_rows = pl.ds(pl.multiple_of(win0 // 2, min(window, TQ) // 2), tk // 2)
            qs = [_unpack_rows(q_ref[q_rows, j * MXU_DIM:(j + 1) * MXU_DIM]) for j in range(HEADS_PER_TILE)]
            shared[t] = (qs, _unpack_rows(k_ref[kv_rows, :]), _unpack_rows(v_ref[kv_rows, :]).T)
        qs, k4, vt = shared[t]
        vt_c = _with_ones(vt[c * HEAD_DIM:(c + 1) * HEAD_DIM])
        q_stack = _stack_heads(qs[half * A_HEADS:(half + 1) * A_HEADS], [c] * A_HEADS)
        bias = bias_ref[var, c, :, half * A_HEADS * TQ:(half + 1) * A_HEADS * TQ]
        return _scores(q_stack, k4, bias), vt_c

    def finish(stage, started):
        t, c, half = stage
        pv, m, den = _softmax_pv(*started)
        sink = jnp.zeros_like(m)
        for jj in range(A_HEADS):
            sink = jnp.where(qblk == jj, sink_ref[HEADS_PER_TILE * c + half * A_HEADS + jj] * LOG2E, sink)
        gate = 1.0 / (den + jnp.exp2(sink - m))
        for jj in range(A_HEADS):
            cols = slice(jj * TQ, (jj + 1) * TQ)
            blocks[t, half * A_HEADS + jj, c] = (pv[:, cols] * gate[:, cols]).astype(o_ref.dtype)
        if (c, half) == (KV_A - 1, HEADS_PER_TILE // A_HEADS - 1):
            tile_t = jnp.concatenate(
                [blocks.pop((t, j, cc)) for j in range(HEADS_PER_TILE) for cc in range(KV_A)], axis=0)
            o_ref[pl.ds(rows[t], TQ), :] = tile_t.T

    _pipeline([(t, c, half) for t in range(A_TILES) for c in range(KV_A)
               for half in range(HEADS_PER_TILE // A_HEADS)], start, finish, A_DEPTH)
    return carry


def _attention_a(qkv, sink):
    b, packed_s, _ = qkv.shape
    s = 2 * packed_s
    dq = N_HEADS * HEAD_DIM
    dkv = KV_A * HEAD_DIM
    tk = TQ + 2 * WINDOW_A
    offsets = (0, WINDOW_A, 2 * WINDOW_A)
    kern = functools.partial(_attn_a_kernel, seq_len=s, tk=tk, window=WINDOW_A, offsets=offsets)
    return pl.pallas_call(
        kern,
        grid=(b,),
        in_specs=[pl.BlockSpec(memory_space=pltpu.SMEM),
                  pl.BlockSpec((None, packed_s, dq), lambda bi: (bi, 0, 0)),
                  pl.BlockSpec((None, packed_s, dkv), lambda bi: (bi, 0, dq // dkv)),
                  pl.BlockSpec((None, packed_s, dkv), lambda bi: (bi, 0, dq // dkv + 1))],
        out_specs=pl.BlockSpec((None, s, dq), lambda bi: (bi, 0, 0)),
        out_shape=jax.ShapeDtypeStruct((b, s, dq), BF16),
        scratch_shapes=[pltpu.VMEM((len(offsets), KV_A, tk, HEADS_PER_TILE * TQ), F32)],
        compiler_params=_params("arbitrary"),
    )(sink, qkv, qkv, qkv)


def _merge_groups(o_refs, lse_refs):
    src = lax.broadcasted_iota(jnp.int32, (LANES, LANES), 0)
    dst = lax.broadcasted_iota(jnp.int32, (LANES, LANES), 1)
    fold = jnp.where((src % N_HEADS == dst % N_HEADS) & (src < LSE_PARTS * N_HEADS), 1.0, 0.0).astype(BF16)
    src = lax.broadcasted_iota(jnp.int32, (LANES, D_MODEL), 0)
    dst = lax.broadcasted_iota(jnp.int32, (LANES, D_MODEL), 1)
    expand = jnp.where((src % N_HEADS == dst // HEAD_DIM) & (src < LSE_PARTS * N_HEADS), 1.0, 0.0).astype(BF16)
    lses = [jnp.dot(_unpack_rows(r[...]), fold, preferred_element_type=F32) for r in lse_refs]
    mx = functools.reduce(jnp.maximum, lses)
    es = [jnp.exp2(l - mx) for l in lses]
    den = functools.reduce(lambda a, b: a + b, es)
    piece_of_lane = lax.broadcasted_iota(jnp.int32, es[0].shape, 1) // N_HEADS
    packs = []
    for e in es[:-1]:
        packed = jnp.zeros(e.shape, BF16)
        for k, piece in enumerate(_split_f32(e / den)):
            packed = jnp.where(piece_of_lane == k, piece, packed)
        packs.append(packed)
    chunks = []
    for lo in range(0, D_MODEL, MXU_DIM):
        cols = slice(lo, lo + MXU_DIM)
        ws = [jnp.dot(p, expand[:, cols], preferred_element_type=F32) for p in packs]
        acc = (1.0 - functools.reduce(lambda a, b: a + b, ws)) * _unpack_rows(o_refs[-1][:, cols]).astype(F32)
        for w, o_ref in zip(ws, o_refs[:-1]):
            acc = acc + w * _unpack_rows(o_ref[:, cols]).astype(F32)
        chunks.append(acc.astype(BF16))
    return jnp.concatenate(chunks, axis=1)


def _post_kernel(*refs, n_groups, final, ff_chunks, side_cast):
    refs = list(refs)
    x_ref = refs.pop(0)
    o_refs = [refs.pop(0) for _ in range(n_groups)]
    lse_refs = [refs.pop(0) for _ in range(n_groups)] if n_groups > 1 else []
    wo_ref, gf_ref, wg_ref, wu_ref, wd_ref = refs[:5]
    refs = refs[5:]
    gfin_ref = refs.pop(0) if final else None
    if side_cast:
        src_ref, scale_ref, out_ref, dst_ref = refs
        dst_ref[...] = (src_ref[...] * scale_ref[...]).astype(dst_ref.dtype)
    else:
        (out_ref,) = refs

    o = _merge_groups(o_refs, lse_refs) if n_groups > 1 else o_refs[0][...]
    x1 = x_ref[...] + jnp.dot(o, wo_ref[...], preferred_element_type=F32)
    h = (x1 * gf_ref[...]).astype(BF16)
    r = lax.rsqrt(jnp.mean(x1 * x1, axis=-1, keepdims=True) + RMS_EPS)
    acc = x1
    lo = 0
    for fc in ff_chunks:
        gate = r * jnp.dot(h, wg_ref[:, lo:lo + fc], preferred_element_type=F32)
        up = r * jnp.dot(h, wu_ref[:, lo:lo + fc], preferred_element_type=F32)
        a = (gate / (1.0 + jnp.exp(-gate)) * up).astype(BF16)
        acc = acc + jnp.dot(a, wd_ref[lo:lo + fc, :], preferred_element_type=F32)
        lo += fc
    if final:
        acc = _rms(acc, gfin_ref[...])
    out_ref[...] = acc


def _ff_chunks(d_ff, n):
    tiles, rem = divmod(d_ff, MXU_DIM)
    assert rem == 0
    return tuple((tiles // n + (k < tiles % n)) * MXU_DIM for k in range(n))


def _layer_spec(shape, layer):
    return pl.BlockSpec((None,) + tuple(shape[1:]), lambda *_: (layer,) + (0,) * (len(shape) - 1),
                        pipeline_mode=pl.Buffered(1))


def _post(x2d, os, lses, wo, gf, wg, wu, wd, layer, gfin, tm, ff_chunks, side_cast=None):
    t, d = x2d.shape
    n_steps = t // tm
    ff_chunks = _ff_chunks(wg.shape[2], ff_chunks)
    final = gfin is not None
    row = lambda n: pl.BlockSpec((tm, n), lambda i: (i, 0))
    packed = lambda n: pl.BlockSpec((tm // 2, n), lambda i: (i, 0))
    in_specs = [row(d)] + ([packed(d)] * len(os) + [packed(LANES)] * len(lses) if lses else [row(d)])
    in_specs += [_const_spec((d, d))] + [_layer_spec(a.shape, layer) for a in (gf, wg, wu, wd)]
    args = [x2d, *os, *lses, wo, gf, wg, wu, wd]
    out_specs = [row(d)]
    out_shape = [jax.ShapeDtypeStruct((t, d), F32)]
    if final:
        in_specs.append(_const_spec((1, d)))
        args.append(gfin.reshape(1, d))
    if side_cast:
        w, scale = side_cast
        in_specs += [_row_chunk_spec(w, n_steps), _const_spec((1, w.shape[1]))]
        args += [w, scale.reshape(1, -1)]
        out_specs.append(_row_chunk_spec(w, n_steps))
        out_shape.append(jax.ShapeDtypeStruct(w.shape, BF16))
    return pl.pallas_call(
        functools.partial(_post_kernel, n_groups=len(os), final=final, ff_chunks=ff_chunks,
                          side_cast=bool(side_cast)),
        grid=(n_steps,),
        in_specs=in_specs,
        out_specs=out_specs,
        out_shape=out_shape,
        compiler_params=_params("parallel"),
    )(*args)


def _q_column_scale(n_cols, is_query_col):
    cols = np.arange(n_cols)
    return jnp.asarray(np.where(is_query_col(cols), HEAD_DIM ** -0.5 * LOG2E, 1.0), F32)


def kernel(x, norm_mix, norm_ffn, w_qkv_a, w_out_a, sink_a, w_qkv_b, w_out_b,
           w_gate, w_up, w_down, norm_final):
    b, s, d = x.shape
    t = b * s
    dq = N_HEADS * HEAD_DIM
    d_ff = w_gate.shape[2]

    wa = w_qkv_a[0] * _q_column_scale(w_qkv_a.shape[2], lambda c: c < dq)
    wq = wa[:, :dq].reshape(d, KV_A, HEADS_PER_TILE, HEAD_DIM).transpose(0, 2, 1, 3).reshape(d, dq)
    wa = jnp.concatenate([wq, wa[:, dq:]], axis=1).astype(BF16)
    wo_a = w_out_a[0].reshape(KV_A, HEADS_PER_TILE, HEAD_DIM, d).transpose(1, 0, 2, 3).reshape(dq, d)
    x2d = x.reshape(t, d)
    qkv, wg, wu, wd = _norm_matmul(
        x2d, norm_mix[0], wa, tm=2048,
        casts=(w_gate.reshape(-1, d_ff), w_up.reshape(-1, d_ff), w_down.reshape(-1, d)))
    ffn = (norm_ffn.reshape(-1, 1, d), wg.reshape(w_gate.shape), wu.reshape(w_up.shape), wd.reshape(w_down.shape))
    o_a = _attention_a(qkv.reshape(b, s // 2, -1), sink_a[0])
    wb_scale = _q_column_scale(w_qkv_b.shape[2], lambda c: c % (3 * dq) < dq)
    x2d, wb = _post(x2d, [o_a.reshape(t, d)], [], wo_a.astype(BF16), *ffn, 0, None, tm=1024, ff_chunks=4,
                    side_cast=(w_qkv_b[0], wb_scale))

    dils = tuple(dil for _, dil in DILATED_GROUPS)
    qkvs = _norm_matmul_dilated(x2d.reshape(b, s, d), norm_mix[1], wb, dils)
    os, lses = [], []
    for qkv_g, (win, dil) in zip(qkvs, DILATED_GROUPS):
        o_g, lse_g = _attention_b(qkv_g, dil=dil, window=win // (2 * dil))
        os.append(o_g)
        lses.append(lse_g)
    (out,) = _post(x2d, os, lses, w_out_b[0].astype(BF16), *ffn, 1, norm_final, tm=512, ff_chunks=2)
    return out.reshape(b, s, d)
```
